```python
import math
import jax
import jax.numpy as jnp
from jax import lax
import numpy as np

D_MODEL = 2048
BATCH = 2
SEQ = 4096
DEPTH = 1
DEC_BATCH = 8
DEC_SEQ = 4
PAST_LEN = 16384
PAGE_SIZE = 128

HEAD_DIM = 128
MIX_WIDTH = D_MODEL
ATT_WIDTH = MIX_WIDTH // 2
N_ATT_HEADS = ATT_WIDTH // HEAD_DIM
SGU_WIDTH = MIX_WIDTH - ATT_WIDTH
N_SGU_GROUPS = SGU_WIDTH // HEAD_DIM
CHUNK = 128
BRANCHES = ((128, 1), (512, 4), (2048, 16))
ATT_WINDOW = max(w for w, _ in BRANCHES)
BAND = 128
N_KEYS = 128
N_EXPERTS = N_KEYS * N_KEYS
PEER_HEADS = 8
PEER_TOPK = 16
D_QUERY = 256
D_QHALF = D_QUERY // 2
TOK_BLOCK = 128
EPS = 1e-6
IN_SPLITS = (ATT_WIDTH, 2 * ATT_WIDTH, 3 * ATT_WIDTH, 3 * ATT_WIDTH + SGU_WIDTH)
IN_WIDTH = 3 * ATT_WIDTH + 2 * SGU_WIDTH

kernel_name = 'hybrid_dilated_sgu_peer_step'


def rmsnorm(x, g):
    xf = x.astype(jnp.float32)
    y = xf * lax.rsqrt(jnp.mean(xf * xf, axis=-1, keepdims=True) + EPS)
    return (y * g.astype(jnp.float32)).astype(x.dtype)


def layernorm(x, g, b):
    xf = x.astype(jnp.float32)
    mu = jnp.mean(xf, axis=-1, keepdims=True)
    var = jnp.mean(jnp.square(xf - mu), axis=-1, keepdims=True)
    y = (xf - mu) * lax.rsqrt(var + EPS)
    return (y * g.astype(jnp.float32) + b.astype(jnp.float32)).astype(x.dtype)


def alibi_slopes(n):
    return 2.0 ** (-8.0 * jnp.arange(1, n + 1, dtype=jnp.float32) / n)


def adaln(c, w_ada, b_ada):
    m = (jax.nn.silu(c) @ w_ada + b_ada)[:, None, :]
    return jnp.split(m, 6, axis=-1)


def dilated_branch_prompt(q, k, v, dil, n_steps, slopes):
    B, S, H, Dh = q.shape
    span = BAND * dil
    Sp = -(-S // span) * span
    L = Sp // dil
    nb = L // BAND

    def to_blocks(t):
        t = jnp.pad(t, ((0, 0), (0, Sp - S), (0, 0), (0, 0)))
        t = t.reshape(B, L, dil, H, Dh).transpose(0, 2, 1, 3, 4)
        return t.reshape(B, dil, nb, BAND, H, Dh)

    def with_prev(t):
        prev = jnp.pad(t, ((0, 0), (0, 0), (1, 0), (0, 0), (0, 0), (0, 0)))[:, :, :-1]
        return jnp.concatenate([prev, t], axis=3)

    qb = to_blocks(q)
    kw = with_prev(to_blocks(k))
    vw = with_prev(to_blocks(v))
    s = jnp.einsum('brnqhc,brnkhc->brnhqk', qb, kw).astype(jnp.float32) * (HEAD_DIM ** -0.5)
    a = jnp.arange(BAND)[:, None]
    kk = jnp.arange(2 * BAND)[None, :]
    steps = BAND + a - kk
    valid = (steps >= 0) & (steps <= n_steps)
    not_before_start = (jnp.arange(nb)[:, None, None] > 0) | (kk >= BAND)[None]
    mask = valid[None] & not_before_start
    bias = -slopes[:, None, None] * (dil * steps).astype(jnp.float32)[None]
    s = jnp.where(mask[:, None], s + bias, -jnp.inf)
    lse = jax.nn.logsumexp(s, axis=-1)
    p = jnp.exp(s - lse[..., None])
    o = jnp.einsum('brnhqk,brnkhc->brnqhc', p.astype(vw.dtype), vw).astype(jnp.float32)
    o = o.reshape(B, dil, L, H, Dh).transpose(0, 2, 1, 3, 4).reshape(B, Sp, H, Dh)[:, :S]
    lse = lse.transpose(0, 1, 2, 4, 3).reshape(B, dil, L, H).transpose(0, 2, 1, 3).reshape(B, Sp, H)[:, :S]
    return o, lse


def dilated_branch_sample(q, k_all, v_all, dil, n_steps, slopes, buf_len):
    T = q.shape[1]
    steps = jnp.arange(n_steps + 1)
    idx = buf_len + jnp.arange(T)[:, None] - dil * steps[None, :]
    valid = idx >= 0
    idx = jnp.maximum(idx, 0)
    kg = jnp.take(k_all, idx, axis=1)
    vg = jnp.take(v_all, idx, axis=1)
    s = jnp.einsum('bthc,btkhc->bthk', q, kg).astype(jnp.float32) * (HEAD_DIM ** -0.5)
    bias = -slopes[:, None] * (dil * steps).astype(jnp.float32)[None, :]
    s = jnp.where(valid[None, :, None, :], s + bias[None, None], -jnp.inf)
    lse = jax.nn.logsumexp(s, axis=-1)
    p = jnp.exp(s - lse[..., None])
    o = jnp.einsum('bthk,btkhc->bthc', p.astype(vg.dtype), vg).astype(jnp.float32)
    return o, lse


def combine_branches(res, dtype):
    o = jnp.stack([r[0] for r in res])
    lse = jnp.stack([r[1] for r in res])
    w = jax.nn.softmax(lse, axis=0)
    o = jnp.sum(w[..., None] * o, axis=0)
    return o.reshape(o.shape[0], o.shape[1], ATT_WIDTH).astype(dtype)


def sgu_mix_prompt(vn, ws, bs):
    B, S, _ = vn.shape
    vr = vn.reshape(B, S // CHUNK, CHUNK, N_SGU_GROUPS, HEAD_DIM)
    m = jnp.einsum('gts,bnsgc->bntgc', ws, vr) + bs.T[:, :, None]
    return m.reshape(B, S, SGU_WIDTH)


def sgu_mix_sample(vn, ws, bs):
    B, T, _ = vn.shape
    vr = vn.reshape(B, T, N_SGU_GROUPS, HEAD_DIM)
    m = jnp.einsum('gts,bsgc->btgc', ws[:, :T, :T], vr) + bs[:, :T].T[:, :, None]
    return m.reshape(B, T, SGU_WIDTH)


def peer_ffn(h, w_query, sub_keys, expert_u, expert_v):
    T = h.shape[0]
    Tp = -(-T // TOK_BLOCK) * TOK_BLOCK
    hb_all = jnp.pad(h, ((0, Tp - T), (0, 0))).reshape(Tp // TOK_BLOCK, TOK_BLOCK, D_MODEL)
    kk = PEER_TOPK * PEER_TOPK

    def block(hb):
        q = (hb @ w_query).reshape(TOK_BLOCK, PEER_HEADS, 2, D_QHALF)
        sc = jnp.einsum('thpc,pnc->thpn', q, sub_keys).astype(jnp.float32)
        top_s, top_i = lax.top_k(sc, PEER_TOPK)
        cand_s = (top_s[:, :, 0, :, None] + top_s[:, :, 1, None, :]).reshape(TOK_BLOCK, PEER_HEADS, kk)
        cand_i = (top_i[:, :, 0, :, None] * N_KEYS + top_i[:, :, 1, None, :]).reshape(TOK_BLOCK, PEER_HEADS, kk)
        best_s, best_j = lax.top_k(cand_s, PEER_TOPK)
        idx = jnp.take_along_axis(cand_i, best_j, axis=-1)
        gate = jax.nn.softmax(best_s, axis=-1)
        u = expert_u[idx]
        act = jax.nn.gelu(jnp.einsum('thkd,td->thk', u, hb).astype(jnp.float32), approximate=False)
        coef = (gate * act).astype(hb.dtype)
        return jnp.einsum('thk,thkd->td', coef, expert_v[idx])

    return lax.map(block, hb_all).reshape(Tp, D_MODEL)[:T]


def layer_forward(x, c, mixer, w_ada, b_ada, g_norm1, w_in, ln_v_g, ln_v_b, w_out, g_norm2,
                  w_query, sub_keys, expert_u, expert_v):
    B, S, _ = x.shape
    sh1, sc1, gt1, sh2, sc2, gt2 = adaln(c, w_ada, b_ada)
    h = rmsnorm(x, g_norm1) * (1.0 + sc1) + sh1
    q, k, v, u, g = jnp.split(h @ w_in, IN_SPLITS, axis=-1)
    heads = lambda t: t.reshape(B, S, N_ATT_HEADS, HEAD_DIM)
    u = jax.nn.gelu(u, approximate=False)
    vn = layernorm(jax.nn.gelu(g, approximate=False), ln_v_g, ln_v_b)
    att, gate_v, states = mixer(heads(q), heads(k), heads(v), vn)
    mix = jnp.concatenate([att, u * gate_v], axis=-1)
    x = x + gt1 * (mix @ w_out)
    h2 = rmsnorm(x, g_norm2) * (1.0 + sc2) + sh2
    ffn = peer_ffn(h2.reshape(B * S, D_MODEL), w_query, sub_keys, expert_u, expert_v).reshape(B, S, D_MODEL)
    x = x + gt2 * ffn
    return x, states


def setup_inputs(seed: int = 0) -> dict:
    key = jax.random.key(seed)
    ks = jax.random.split(key, 24)
    f32 = jnp.float32

    def nrm(k, shape, s=1.0):
        return s * jax.random.normal(k, shape, dtype=f32)

    buf = min(ATT_WINDOW, PAST_LEN)
    return {
        'x_prompt': nrm(ks[0], (BATCH, SEQ, D_MODEL)),
        'x_sample': nrm(ks[1], (DEC_BATCH, DEC_SEQ, D_MODEL)),
        'cache_k': nrm(ks[2], (DEPTH, DEC_BATCH, buf, N_ATT_HEADS, HEAD_DIM)),
        'cache_v': nrm(ks[3], (DEPTH, DEC_BATCH, buf, N_ATT_HEADS, HEAD_DIM)),
        'c_prompt': nrm(ks[4], (BATCH, D_MODEL)),
        'c_sample': nrm(ks[5], (DEC_BATCH, D_MODEL)),
        'w_ada': nrm(ks[6], (DEPTH, D_MODEL, 6 * D_MODEL), 0.5 * D_MODEL ** -0.5),
        'b_ada': nrm(ks[7], (DEPTH, 6 * D_MODEL), 0.01),
        'g_norm1': 1.0 + nrm(ks[8], (DEPTH, D_MODEL), 0.02),
        'w_in': nrm(ks[9], (DEPTH, D_MODEL, IN_WIDTH), D_MODEL ** -0.5),
        'ln_v_g': 1.0 + nrm(ks[10], (DEPTH, SGU_WIDTH), 0.02),
        'ln_v_b': nrm(ks[11], (DEPTH, SGU_WIDTH), 0.01),
        'w_spatial': nrm(ks[12], (DEPTH, N_SGU_GROUPS, CHUNK, CHUNK), CHUNK ** -0.5),
        'b_spatial': 1.0 + nrm(ks[13], (DEPTH, N_SGU_GROUPS, CHUNK), 0.02),
        'w_out': nrm(ks[14], (DEPTH, MIX_WIDTH, D_MODEL), MIX_WIDTH ** -0.5),
        'g_norm2': 1.0 + nrm(ks[15], (DEPTH, D_MODEL), 0.02),
        'w_query': nrm(ks[16], (DEPTH, D_MODEL, PEER_HEADS * D_QUERY), D_MODEL ** -0.5),
        'sub_keys': nrm(ks[17], (DEPTH, 2, N_KEYS, D_QHALF), D_QHALF ** -0.5),
        'expert_u': nrm(ks[18], (DEPTH, N_EXPERTS, D_MODEL), D_MODEL ** -0.5),
        'expert_v': nrm(ks[19], (DEPTH, N_EXPERTS, D_MODEL), 0.5),
        'g_final': 1.0 + nrm(ks[20], (D_MODEL,), 0.02),
    }


def reference(x_prompt, x_sample, cache_k, cache_v, c_prompt, c_sample, w_ada, b_ada, g_norm1, w_in,
              ln_v_g, ln_v_b, w_spatial, b_spatial, w_out, g_norm2, w_query, sub_keys, expert_u, expert_v,
              g_final):
    slopes = alibi_slopes(N_ATT_HEADS)
    tri = jnp.tril(jnp.ones((CHUNK, CHUNK), dtype=bool))
    buf_len = cache_k.shape[2]
    win_p = min(ATT_WINDOW, x_prompt.shape[1])
    xp, xs = x_prompt, x_sample
    kp_all, vp_all, ks_all, vs_all, gs_all = [], [], [], [], []
    for l in range(DEPTH):
        ws = jnp.where(tri, w_spatial[l], 0.0).astype(w_spatial.dtype)
        bs = b_spatial[l]
        ck, cv = cache_k[l], cache_v[l]

        def prompt_mixer(q, k, v, vn):
            res = [dilated_branch_prompt(q, k, v, d, w // d, slopes) for w, d in BRANCHES]
            att = combine_branches(res, q.dtype)
            return att, sgu_mix_prompt(vn, ws, bs), (k[:, -win_p:], v[:, -win_p:])

        def sample_mixer(q, k, v, vn):
            k_all = jnp.concatenate([ck.astype(k.dtype), k], axis=1)
            v_all = jnp.concatenate([cv.astype(v.dtype), v], axis=1)
            res = [dilated_branch_sample(q, k_all, v_all, d, w // d, slopes, buf_len) for w, d in BRANCHES]
            att = combine_branches(res, q.dtype)
            return att, sgu_mix_sample(vn, ws, bs), (k, v, vn)

        params = (w_ada[l], b_ada[l], g_norm1[l], w_in[l], ln_v_g[l], ln_v_b[l], w_out[l], g_norm2[l],
                  w_query[l], sub_keys[l], expert_u[l], expert_v[l])
        xp, (kp, vp) = layer_forward(xp, c_prompt, prompt_mixer, *params)
        xs, (ksn, vsn, gsn) = layer_forward(xs, c_sample, sample_mixer, *params)
        kp_all.append(kp)
        vp_all.append(vp)
        ks_all.append(ksn)
        vs_all.append(vsn)
        gs_all.append(gsn)
    y_prompt = rmsnorm(xp, g_final)
    y_sample = rmsnorm(xs, g_final)
    return (y_prompt, y_sample, jnp.stack(kp_all), jnp.stack(vp_all), jnp.stack(ks_all), jnp.stack(vs_all), jnp.stack(gs_all))
```

```python
import functools

import numpy as np
import jax
import jax.numpy as jnp
from jax import lax
from jax.experimental import pallas as pl
from jax.experimental.pallas import tpu as pltpu

F32 = jnp.float32
BF16 = jnp.bfloat16

EPS = 1e-6
LANES = 128
HEAD_DIM = 128
N_HEADS = 8
ATT_WIDTH = N_HEADS * HEAD_DIM
BRANCHES = ((128, 1), (512, 4), (2048, 16))
BAND = 128
N_KEYS = 128
TOPK = 16
STAT_REP = LANES // N_HEADS
NEG = -1e30
VMEM_LIMIT = 56 * 1024 * 1024


def _params(sem):
    return pltpu.CompilerParams(dimension_semantics=sem, vmem_limit_bytes=VMEM_LIMIT)


def _gelu(x):
    return 0.5 * x * (1.0 + lax.erf(x * (2.0 ** -0.5)))


def _rms(x, g):
    return x * lax.rsqrt(jnp.mean(x * x, axis=-1, keepdims=True) + EPS) * g


def _adaln_kernel(c_ref, w_ref, b_ref, o_ref):
    c = c_ref[...]
    a = c / (1.0 + jnp.exp(-c))
    o_ref[...] = jnp.dot(a.astype(BF16), w_ref[...].astype(BF16),
                         preferred_element_type=F32) + b_ref[...]


def _adaln(c_all, w_ada, b_ada):
    rows, d = c_all.shape
    n = w_ada.shape[1]
    tn = 1024
    return pl.pallas_call(
        _adaln_kernel,
        grid=(n // tn,),
        in_specs=[pl.BlockSpec((rows, d), lambda j: (0, 0)),
                  pl.BlockSpec((d, tn), lambda j: (0, j)),
                  pl.BlockSpec((1, tn), lambda j: (0, j))],
        out_specs=pl.BlockSpec((rows, tn), lambda j: (0, j)),
        out_shape=jax.ShapeDtypeStruct((rows, n), F32),
        compiler_params=_params(("arbitrary",)),
        name="adaln",
    )(c_all, w_ada, b_ada)


def _inproj_kernel(x_ref, sh_ref, sc_ref, g1_ref, w_ref, lng_ref, lnb_ref,
                   q_ref, k_ref, v_ref, u_ref, vn_ref, h_scr):
    j = pl.program_id(1)

    @pl.when(j == 0)
    def _():
        y = _rms(x_ref[...], g1_ref[...])
        h_scr[...] = (y * (1.0 + sc_ref[...]) + sh_ref[...]).astype(BF16)

    p = jnp.dot(h_scr[...], w_ref[...], preferred_element_type=F32)

    @pl.when(j == 0)
    def _():
        q_ref[...] = (p * (HEAD_DIM ** -0.5)).astype(BF16)

    @pl.when(j == 1)
    def _():
        k_ref[...] = p

    @pl.when(j == 2)
    def _():
        v_ref[...] = p

    @pl.when(j == 3)
    def _():
        u_ref[...] = _gelu(p).astype(BF16)

    @pl.when(j == 4)
    def _():
        g = _gelu(p)
        mu = jnp.mean(g, axis=-1, keepdims=True)
        gc = g - mu
        var = jnp.mean(gc * gc, axis=-1, keepdims=True)
        vn_ref[...] = gc * lax.rsqrt(var + EPS) * lng_ref[...] + lnb_ref[...]


def _mod_spec(mod, tiles_per_row):
    _, r, d = mod.shape
    return pl.BlockSpec((None, r, d), lambda i, *_: (i // tiles_per_row, 0, 0))


def _inproj(x, sh1, sc1, g1, w_in, lng, lnb, tm, tiles_per_mod):
    t, d = x.shape
    sec = ATT_WIDTH
    n_sec = w_in.shape[1] // sec
    tok = lambda i, j: (i, 0)
    const = lambda i, j: (0, 0)
    out_spec = pl.BlockSpec((tm, sec), tok)
    return pl.pallas_call(
        _inproj_kernel,
        grid=(t // tm, n_sec),
        in_specs=[pl.BlockSpec((tm, d), tok),
                  _mod_spec(sh1, tiles_per_mod), _mod_spec(sc1, tiles_per_mod),
                  pl.BlockSpec((1, d), const),
                  pl.BlockSpec((d, sec), lambda i, j: (0, j)),
                  pl.BlockSpec((1, sec), const), pl.BlockSpec((1, sec), const)],
        out_specs=[out_spec] * 5,
        out_shape=[jax.ShapeDtypeStruct((t, sec), BF16),
                   jax.ShapeDtypeStruct((t, sec), F32),
                   jax.ShapeDtypeStruct((t, sec), F32),
                   jax.ShapeDtypeStruct((t, sec), BF16),
                   jax.ShapeDtypeStruct((t, sec), F32)],
        scratch_shapes=[pltpu.VMEM((tm, d), BF16)],
        compiler_params=_params(("parallel", "arbitrary")),
        name="inproj",
    )(x, sh1, sc1, g1, w_in, lng, lnb)


def _att_kernel(q_ref, kc_ref, kp_ref, vc_ref, vp_ref, bias_ref, o_ref, lse_ref, *, mask_first):
    n = pl.program_id(2)
    lane_head = lax.broadcasted_iota(jnp.int32, (BAND, LANES), 1) // STAT_REP
    is_prev = lax.broadcasted_iota(jnp.int32, (BAND, 2 * BAND), 1) < BAND
    stats = jnp.zeros((BAND, LANES), F32)
    for h in range(N_HEADS):
        cs = slice(h * HEAD_DIM, (h + 1) * HEAD_DIM)
        kw = jnp.concatenate([kp_ref[:, cs], kc_ref[:, cs]], axis=0).astype(BF16)
        vw = jnp.concatenate([vp_ref[:, cs], vc_ref[:, cs]], axis=0).astype(BF16)
        s = lax.dot_general(q_ref[:, cs], kw, (((1,), (1,)), ((), ())),
                            preferred_element_type=F32)
        s = s + bias_ref[h]
        if mask_first:
            s = jnp.where(jnp.logical_and(n == 0, is_prev), NEG, s)
        m = jnp.max(s, axis=-1, keepdims=True)
        p = jnp.exp(s - m)
        l = jnp.sum(p, axis=-1, keepdims=True)
        o = jnp.dot(p.astype(BF16), vw, preferred_element_type=F32)
        o_ref[:, cs] = o / l
        stats = jnp.where(lane_head == h, m + jnp.log(l), stats)
    lse_ref[...] = stats


def _att_bias(dil):
    a = np.arange(BAND)[:, None]
    kk = np.arange(2 * BAND)[None, :]
    steps = BAND + a - kk
    valid = (steps >= 0) & (steps <= BAND)
    slopes = 2.0 ** (-8.0 * np.arange(1, N_HEADS + 1) / N_HEADS)
    bias = -slopes[:, None, None] * (dil * steps).astype(np.float64)[None]
    return jnp.asarray(np.where(valid[None], bias, NEG), dtype=F32)


def _attention(q, kc, kp, vc, vp, dil, grid, cur_map, prev_map, mask_first):
    blk = lambda m: pl.BlockSpec((None, BAND, ATT_WIDTH), m)
    b, rows, cols = q.shape
    return pl.pallas_call(
        functools.partial(_att_kernel, mask_first=mask_first),
        grid=grid,
        in_specs=[blk(cur_map), blk(cur_map), blk(prev_map), blk(cur_map), blk(prev_map),
                  pl.BlockSpec((N_HEADS, BAND, 2 * BAND), lambda *_: (0, 0, 0))],
        out_specs=[blk(cur_map), pl.BlockSpec((None, BAND, LANES), cur_map)],
        out_shape=[jax.ShapeDtypeStruct((b, rows, cols), F32),
                   jax.ShapeDtypeStruct((b, rows, cols // N_HEADS), F32)],
        compiler_params=_params(("parallel", "parallel", "arbitrary")),
        name=f"attention_d{dil}",
    )(q, kc, kp, vc, vp, _att_bias(dil))


def _prompt_attention(q, k, v, batch, seq):
    res = []
    for win, dil in BRANCHES:
        sub = seq // dil
        view = lambda t: t.reshape(batch, sub, dil * ATT_WIDTH)
        cur = lambda b, r, n: (b, n, r)
        prev = lambda b, r, n: (b, jnp.maximum(n - 1, 0), r)
        o, lse = _attention(view(q), view(k), view(k), view(v), view(v), dil,
                            (batch, dil, sub // BAND), cur, prev, True)
        res.append((o.reshape(batch * seq, ATT_WIDTH), lse.reshape(batch * seq, LANES)))
    return res


def _sample_attention(q, k, v, cache_k, cache_v, n_batch, n_new):
    buf = cache_k.shape[1]
    tok = lambda t: t.reshape(n_batch, n_new, ATT_WIDTH)
    res = []
    for win, dil in BRANCHES:
        sub = buf // dil
        prev_blk = sub // BAND - 1
        if dil == 1:
            pad = lambda t: jnp.pad(tok(t), ((0, 0), (0, BAND - n_new), (0, 0)))
            grid = (n_batch, 1, 1)
            cur = lambda b, r, n: (b, 0, 0)
            pick = lambda o: o[:, :n_new]
        else:
            pad = lambda t: jnp.pad(tok(t)[:, :, None, :], ((0, 0), (0, 0), (0, BAND - 1), (0, 0))
                                    ).reshape(n_batch, n_new * BAND, ATT_WIDTH)
            grid = (n_batch, n_new, 1)
            cur = lambda b, r, n: (b, r, 0)
            pick = lambda o: o.reshape(n_batch, n_new, BAND, -1)[:, :, 0]
        prev = lambda b, r, n, pb=prev_blk: (b, pb, r)
        view = lambda c: c.reshape(n_batch, sub, dil * ATT_WIDTH)
        o, lse = _attention(pad(q), pad(k), view(cache_k), pad(v), view(cache_v), dil,
                            grid, cur, prev, False)
        res.append((pick(o).reshape(n_batch * n_new, ATT_WIDTH),
                    pick(lse).reshape(n_batch * n_new, LANES)))
    return res


def _mix_kernel(o1_ref, o2_ref, o3_ref, l1_ref, l2_ref, l3_ref, u_ref, vn_ref, x_ref,
                gt1_ref, sh2_ref, sc2_ref, g2_ref, wmix_ref, bmix_ref, wout_ref, exp_ref,
                x1_ref, h2_ref, h2t_ref, mix_scr):
    tm = x_ref.shape[0]
    l1, l2, l3 = l1_ref[...], l2_ref[...], l3_ref[...]
    lm = jnp.maximum(jnp.maximum(l1, l2), l3)
    e1, e2, e3 = jnp.exp(l1 - lm), jnp.exp(l2 - lm), jnp.exp(l3 - lm)
    den = e1 + e2 + e3
    att = jnp.zeros((tm, ATT_WIDTH), F32)
    for e, o_ref in ((e1, o1_ref), (e2, o2_ref), (e3, o3_ref)):
        w = e / den
        w_hi = w.astype(BF16)
        w_lo = (w - w_hi.astype(F32)).astype(BF16)
        wide = (jnp.dot(w_hi, exp_ref[...], preferred_element_type=F32)
                + jnp.dot(w_lo, exp_ref[...], preferred_element_type=F32))
        att = att + wide * o_ref[...]
    mix_scr[:, :ATT_WIDTH] = att.astype(BF16)
    for c in range(tm // BAND):
        rows = slice(c * BAND, (c + 1) * BAND)
        for g in range(N_HEADS):
            cs = slice(g * HEAD_DIM, (g + 1) * HEAD_DIM)
            m = jnp.dot(wmix_ref[g], vn_ref[rows, cs].astype(BF16),
                        preferred_element_type=F32) + bmix_ref[g]
            mix_scr[rows, ATT_WIDTH + g * HEAD_DIM:ATT_WIDTH + (g + 1) * HEAD_DIM] = (
                u_ref[rows, cs].astype(F32) * m).astype(BF16)
    y = jnp.dot(mix_scr[...], wout_ref[...], preferred_element_type=F32)
    x1 = x_ref[...] + gt1_ref[...] * y
    x1_ref[...] = x1
    h2 = _rms(x1, g2_ref[...]) * (1.0 + sc2_ref[...]) + sh2_ref[...]
    h2_ref[...] = h2.astype(BF16)
    h2t_ref[...] = h2.T.astype(BF16)


def _expand_matrix():
    lane = np.arange(LANES)[:, None]
    col = np.arange(ATT_WIDTH)[None, :]
    return jnp.asarray(lane == STAT_REP * (col // HEAD_DIM), dtype=BF16)


def _mix(att_res, u, vn, x, gt1, sh2, sc2, g2, wmix, bmix, w_out, tm, tiles_per_mod):
    t, d = x.shape
    tok = lambda i: (i, 0)
    c2 = lambda i: (0, 0)
    c3 = lambda i: (0, 0, 0)
    o_spec = pl.BlockSpec((tm, ATT_WIDTH), tok)
    l_spec = pl.BlockSpec((tm, LANES), tok)
    (o1, l1), (o2, l2), (o3, l3) = att_res
    return pl.pallas_call(
        _mix_kernel,
        grid=(t // tm,),
        in_specs=[o_spec, o_spec, o_spec, l_spec, l_spec, l_spec,
                  pl.BlockSpec((tm, ATT_WIDTH), tok), pl.BlockSpec((tm, ATT_WIDTH), tok),
                  pl.BlockSpec((tm, d), tok),
                  _mod_spec(gt1, tiles_per_mod), _mod_spec(sh2, tiles_per_mod),
                  _mod_spec(sc2, tiles_per_mod),
                  pl.BlockSpec((1, d), c2),
                  pl.BlockSpec(wmix.shape, c3), pl.BlockSpec(bmix.shape, c3),
                  pl.BlockSpec(w_out.shape, c2), pl.BlockSpec((LANES, ATT_WIDTH), c2)],
        out_specs=[pl.BlockSpec((tm, d), tok), pl.BlockSpec((tm, d), tok),
                   pl.BlockSpec((d, tm), lambda i: (0, i))],
        out_shape=[jax.ShapeDtypeStruct((t, d), F32),
                   jax.ShapeDtypeStruct((t, d), BF16),
                   jax.ShapeDtypeStruct((d, t), BF16)],
        scratch_shapes=[pltpu.VMEM((tm, d), BF16)],
        compiler_params=_params(("parallel",)),
        name="mix_outproj",
    )(o1, o2, o3, l1, l2, l3, u, vn, x, gt1, sh2, sc2, g2, wmix, bmix, w_out, _expand_matrix())


def _cand_layout():
    rows = []
    rows += [(0, j) for j in range(16)]
    rows += [(1, j) for j in range(8)]
    rows += [(8 + r, 0) for r in range(8)]
    rows += [(2, j) if j < 5 else None for j in range(8)]
    rows += [(3, j) if j < 4 else None for j in range(8)]
    rows += [(4, j) if j < 3 else None for j in range(8)]
    rows += [(r, 0) if 5 <= r < 8 else None for r in range(8)]
    rows += [(r, 1) if 5 <= r < 8 else None for r in range(8)]
    return rows


N_CAND_ROWS = 72
BIG_FLAT = 1e9


def _cand_flat():
    flat = np.full((N_CAND_ROWS, LANES), BIG_FLAT, np.float32)
    for r, cell in enumerate(_cand_layout()):
        if cell is not None:
            flat[r, :] = 16 * cell[0] + cell[1]
    return jnp.asarray(flat)


def _top16(s, row_id, top_ref):
    rank = jnp.full(s.shape, float(TOPK), F32)
    for it in range(TOPK):
        m = jnp.max(s, axis=0, keepdims=True)
        idx = jnp.min(jnp.where(s == m, row_id, float(N_KEYS)), axis=0, keepdims=True)
        sel = row_id == idx
        rank = jnp.where(sel, float(it), rank)
        s = jnp.where(sel, -jnp.inf, s)
        top_ref[it:it + 1, :] = m
    return rank


def _select_kernel(h2_ref, wq_ref, keys_ref, flat_ref, cnt_ref, g1_ref, r2_ref, e2_ref,
                   qp_scr, top1_scr, top2_scr):
    tm = h2_ref.shape[0]
    qp_scr[...] = jnp.dot(h2_ref[...], wq_ref[...], preferred_element_type=F32).astype(BF16)
    row_id = lax.broadcasted_iota(jnp.int32, (N_KEYS, LANES), 0).astype(F32)
    row8 = lax.broadcasted_iota(jnp.int32, (8, LANES), 0)
    flat = flat_ref[...]
    valid = flat < BIG_FLAT
    for lc in range(tm // LANES):
        toks = slice(lc * LANES, (lc + 1) * LANES)

        def head_body(h, carry, toks=toks):
            sides = []
            for p, top_scr in ((0, top1_scr), (1, top2_scr)):
                col = pl.multiple_of((2 * h + p) * N_KEYS, N_KEYS)
                s = lax.dot_general(keys_ref[p], qp_scr[toks, pl.ds(col, N_KEYS)],
                                    (((1,), (1,)), ((), ())), preferred_element_type=F32)
                sides.append((s, _top16(s, row_id, top_scr)))
            (s1, rank1), (s2, rank2) = sides
            a1 = top1_scr[...]
            a2 = top2_scr[...]
            v = jnp.concatenate([
                a1[0:1] + a2[0:16],
                a1[1:2] + a2[0:8],
                a1[8:16] + a2[0:1],
                a1[2:3] + a2[0:8],
                a1[3:4] + a2[0:8],
                a1[4:5] + a2[0:8],
                a1[0:8] + a2[0:1],
                a1[0:8] + a2[1:2]], axis=0)
            vw = jnp.where(valid, v, -jnp.inf)
            selm = jnp.zeros(v.shape, F32)
            for _ in range(TOPK):
                m = jnp.max(vw, axis=0, keepdims=True)
                f = jnp.min(jnp.where(vw == m, flat, BIG_FLAT), axis=0, keepdims=True)
                sel = flat == f
                selm = jnp.where(sel, 1.0, selm)
                vw = jnp.where(sel, -jnp.inf, vw)
            vmax = a1[0:1] + a2[0:1]
            z = jnp.sum(jnp.where(selm > 0.0, jnp.exp(v - vmax), 0.0), axis=0, keepdims=True)
            cnt_lo = selm[56:64] + selm[64:72]
            for i, (lo, hi) in enumerate(((0, 16), (16, 24), (32, 40), (40, 48), (48, 56))):
                ci = jnp.sum(selm[lo:hi], axis=0, keepdims=True)
                cnt_lo = jnp.where(row8 == i, ci, cnt_lo)
            cnt_hi = selm[24:32]
            cnt_a = jnp.zeros((N_KEYS, LANES), F32)
            for i in range(TOPK):
                ci = cnt_lo[i:i + 1] if i < 8 else cnt_hi[i - 8:i - 7]
                cnt_a = jnp.where(rank1 == float(i), ci, cnt_a)
            cnt_ref[h, :, toks] = cnt_a
            g1_ref[h, :, toks] = jnp.exp(s1 - a1[0:1]) / z
            r2_ref[h, :, toks] = rank2
            e2_ref[h, :, toks] = jnp.exp(s2 - a2[0:1])
            return carry

        lax.fori_loop(0, N_HEADS, head_body, 0)


def _select(h2, w_query, sub_keys, tm):
    t, d = h2.shape
    out_spec = pl.BlockSpec((N_HEADS, N_KEYS, tm), lambda i: (0, 0, i))
    out_shape = jax.ShapeDtypeStruct((N_HEADS, N_KEYS, t), F32)
    return pl.pallas_call(
        _select_kernel,
        grid=(t // tm,),
        in_specs=[pl.BlockSpec((tm, d), lambda i: (i, 0)),
                  pl.BlockSpec(w_query.shape, lambda i: (0, 0)),
                  pl.BlockSpec(sub_keys.shape, lambda i: (0, 0, 0)),
                  pl.BlockSpec((N_CAND_ROWS, LANES), lambda i: (0, 0))],
        out_specs=[out_spec] * 4,
        out_shape=[out_shape] * 4,
        scratch_shapes=[pltpu.VMEM((tm, w_query.shape[1]), BF16),
                        pltpu.VMEM((TOPK, LANES), F32), pltpu.VMEM((TOPK, LANES), F32)],
        compiler_params=_params(("parallel",)),
        name="peer_select",
    )(h2, w_query, sub_keys, _cand_flat())


def _peer_kernel(h2t_ref, u_ref, vt_ref, cnt_ref, g1_ref, r2_ref, e2_ref, out_ref,
                 act_scr, p_scr, acc_scr):
    j = pl.program_id(1)
    te, tt = act_scr.shape
    n_a = te // N_KEYS

    @pl.when(j == 0)
    def _():
        acc_scr[...] = jnp.zeros_like(acc_scr)

    act_scr[...] = jnp.dot(u_ref[...], h2t_ref[...], preferred_element_type=F32)

    def lane_body(lc, carry):
        toks = pl.ds(pl.multiple_of(lc * LANES, LANES), LANES)
        for ai in range(n_a):
            rows = slice(ai * N_KEYS, (ai + 1) * N_KEYS)
            coef = jnp.zeros((N_KEYS, LANES), F32)
            for h in range(N_HEADS):
                cnt = cnt_ref[h, ai:ai + 1, toks]
                g1 = g1_ref[h, ai:ai + 1, toks]
                coef = coef + jnp.where(r2_ref[h, :, toks] < cnt, e2_ref[h, :, toks], 0.0) * g1
            p_scr[rows, toks] = (_gelu(act_scr[rows, toks]) * coef).astype(BF16)
        return carry

    lax.fori_loop(0, tt // LANES, lane_body, 0)
    acc_scr[...] += jnp.dot(vt_ref[...], p_scr[...], preferred_element_type=F32)

    @pl.when(j == pl.num_programs(1) - 1)
    def _():
        out_ref[...] = acc_scr[...]


def _peer(h2t, u, vt, sel, tt, te):
    d, t = h2t.shape
    n_exp = u.shape[0]
    a_spec = pl.BlockSpec((N_HEADS, te // N_KEYS, tt), lambda i, j: (0, j, i))
    b_spec = pl.BlockSpec((N_HEADS, N_KEYS, tt), lambda i, j: (0, 0, i))
    return pl.pallas_call(
        _peer_kernel,
        grid=(t // tt, n_exp // te),
        in_specs=[pl.BlockSpec((d, tt), lambda i, j: (0, i)),
                  pl.BlockSpec((te, d), lambda i, j: (j, 0)),
                  pl.BlockSpec((d, te), lambda i, j: (0, j)),
                  a_spec, a_spec, b_spec, b_spec],
        out_specs=pl.BlockSpec((d, tt), lambda i, j: (0, i)),
        out_shape=jax.ShapeDtypeStruct((d, t), F32),
        scratch_shapes=[pltpu.VMEM((te, tt), F32), pltpu.VMEM((te, tt), BF16),
                        pltpu.VMEM((d, tt), F32)],
        compiler_params=_params(("parallel", "arbitrary")),
        name="peer_dense",
    )(h2t, u, vt, *sel)


def _final_kernel(x1_ref, ffnt_ref, gt2_ref, gf_ref, y_ref):
    x2 = x1_ref[...] + gt2_ref[...] * ffnt_ref[...].T
    y_ref[...] = _rms(x2, gf_ref[...])


def _final(x1, ffnt, gt2, g_final, tm, tiles_per_mod):
    t, d = x1.shape
    return pl.pallas_call(
        _final_kernel,
        grid=(t // tm,),
        in_specs=[pl.BlockSpec((tm, d), lambda i: (i, 0)),
                  pl.BlockSpec((d, tm), lambda i: (0, i)),
                  _mod_spec(gt2, tiles_per_mod),
                  pl.BlockSpec((1, d), lambda i: (0, 0))],
        out_specs=pl.BlockSpec((tm, d), lambda i: (i, 0)),
        out_shape=jax.ShapeDtypeStruct((t, d), F32),
        compiler_params=_params(("parallel",)),
        name="final_norm",
    )(x1, ffnt, gt2, g_final)


def _trunk(x, mods, att_fn, w, wmix, bmix, tm_in, tm_mix, tm_sel, tt, te, tiles_per_mod):
    sh1, sc1, gt1, sh2, sc2, gt2 = mods
    q, k, v, u, vn = _inproj(x, sh1, sc1, w["g1"], w["w_in"], w["lng"], w["lnb"], tm_in,
                             tiles_per_mod(tm_in))
    att_res = att_fn(q, k, v)
    x1, h2, h2t = _mix(att_res, u, vn, x, gt1, sh2, sc2, w["g2"], wmix, bmix, w["w_out"],
                       tm_mix, tiles_per_mod(tm_mix))
    sel = _select(h2, w["w_query"], w["sub_keys"], tm_sel)
    ffnt = _peer(h2t, w["u"], w["vt"], sel, tt, te)
    y = _final(x1, ffnt, gt2, w["g_final"], tm_mix, tiles_per_mod(tm_mix))
    return y, k, v, vn


def kernel(x_prompt, x_sample, cache_k, cache_v, c_prompt, c_sample, w_ada, b_ada, g_norm1, w_in,
           ln_v_g, ln_v_b, w_spatial, b_spatial, w_out, g_norm2, w_query, sub_keys, expert_u,
           expert_v, g_final):
    depth = w_ada.shape[0]
    assert depth == 1, "single-layer trunk"
    batch, seq, d = x_prompt.shape
    n_dec, n_new, _ = x_sample.shape
    buf = cache_k.shape[2]
    assert seq % (BAND * BRANCHES[-1][1]) == 0 and buf == BRANCHES[-1][0] and n_new <= 4
    n_tok_s = n_dec * n_new
    pad_s = BAND
    assert n_tok_s <= pad_s

    n_c = batch + n_dec
    c_all = jnp.pad(jnp.concatenate([c_prompt, c_sample], axis=0), ((0, -n_c % 8), (0, 0)))
    mod = _adaln(c_all, w_ada[0], b_ada[0][None, :])
    mods = [mod[:, i * d:(i + 1) * d] for i in range(6)]
    mods_p = [m[:batch, None, :] for m in mods]
    mods_s = [jnp.pad(jnp.repeat(m[batch:n_c], n_new, axis=0), ((0, pad_s - n_tok_s), (0, 0)))[None]
              for m in mods]

    tri = jnp.tril(jnp.ones((BAND, BAND), dtype=bool))
    ws = jnp.where(tri, w_spatial[0], 0.0)
    bs = b_spatial[0]
    wmix_p = ws.astype(BF16)
    bmix_p = jnp.broadcast_to(bs[:, :, None], (N_HEADS, BAND, HEAD_DIM))
    eye = jnp.eye(pad_s // n_new, dtype=F32)
    wmix_s = jax.vmap(lambda m: jnp.kron(eye, m))(ws[:, :n_new, :n_new]).astype(BF16)
    bmix_s = jnp.broadcast_to(jnp.tile(bs[:, :n_new], (1, pad_s // n_new))[:, :, None],
                              (N_HEADS, pad_s, HEAD_DIM))

    w = dict(g1=g_norm1[0][None, :], w_in=w_in[0].astype(BF16), lng=ln_v_g[0][None, :],
             lnb=ln_v_b[0][None, :], g2=g_norm2[0][None, :], w_out=w_out[0].astype(BF16),
             w_query=w_query[0].astype(BF16), sub_keys=sub_keys[0].astype(BF16),
             u=expert_u[0].astype(BF16), vt=expert_v[0].T.astype(BF16), g_final=g_final[None, :])

    xp = x_prompt.reshape(batch * seq, d)
    y_p, k_p, v_p, _ = _trunk(
        xp, mods_p, lambda q, k, v: _prompt_attention(q, k, v, batch, seq), w, wmix_p, bmix_p,
        tm_in=512, tm_mix=256, tm_sel=256, tt=512, te=1024, tiles_per_mod=lambda tm: seq // tm)

    xs = jnp.pad(x_sample.reshape(n_tok_s, d), ((0, pad_s - n_tok_s), (0, 0)))
    ck = cache_k[0].reshape(n_dec, buf, ATT_WIDTH)
    cv = cache_v[0].reshape(n_dec, buf, ATT_WIDTH)

    def sample_att(q, k, v):
        res = _sample_attention(q[:n_tok_s], k[:n_tok_s], v[:n_tok_s], ck, cv, n_dec, n_new)
        padr = lambda t: jnp.pad(t, ((0, pad_s - n_tok_s), (0, 0)))
        return [(padr(o), padr(l)) for o, l in res]

    y_s, k_s, v_s, vn_s = _trunk(
        xs, mods_s, sample_att, w, wmix_s, bmix_s,
        tm_in=pad_s, tm_mix=pad_s, tm_sel=pad_s, tt=pad_s, te=1024, tiles_per_mod=lambda tm: 1)

    win = min(BRANCHES[-1][0], seq)
    heads = lambda t, b, s: t.reshape(b, s, N_HEADS, HEAD_DIM)
    k_win = heads(k_p, batch, seq)[:, -win:][None]
    v_win = heads(v_p, batch, seq)[:, -win:][None]
    return (y_p.reshape(batch, seq, d),
            y_s[:n_tok_s].reshape(n_dec, n_new, d),
            k_win, v_win,
            heads(k_s[:n_tok_s], n_dec, n_new)[None],
            heads(v_s[:n_tok_s], n_dec, n_new)[None],
            vn_s[:n_tok_s].reshape(n_dec, n_new, ATT_WIDTH)[None])
```

```python
import functools

import numpy as np
import jax
import jax.numpy as jnp
from jax import lax
from jax.experimental import pallas as pl
from jax.experimental.pallas import tpu as pltpu

F32 = jnp.float32
BF16 = jnp.bfloat16

EPS = 1e-6
LANES = 128
HEAD_DIM = 128
N_HEADS = 8
ATT_WIDTH = N_HEADS * HEAD_DIM
BRANCHES = ((128, 1), (512, 4), (2048, 16))
BAND = 128
N_KEYS = 128
TOPK = 16
STAT_REP = LANES // N_HEADS
NEG = -1e30
VMEM_LIMIT = 56 * 1024 * 1024
COEF_LANE_CHUNKS = 4


def _params(sem, flags=None):
    return pltpu.CompilerParams(dimension_semantics=sem, vmem_limit_bytes=VMEM_LIMIT, flags=flags)


def _gelu(x):
    return 0.5 * x * (1.0 + lax.erf(x * (2.0 ** -0.5)))


def _rms(x, g):
    return x * lax.rsqrt(jnp.mean(x * x, axis=-1, keepdims=True) + EPS) * g


def _adaln_kernel(c_ref, w_ref, b_ref, o_ref):
    c = c_ref[...]
    a = c / (1.0 + jnp.exp(-c))
    o_ref[...] = jnp.dot(a.astype(BF16), w_ref[...].astype(BF16),
                         preferred_element_type=F32) + b_ref[...]


def _adaln(c_all, w_ada, b_ada):
    rows, d = c_all.shape
    n = w_ada.shape[1]
    tn = 1024
    return pl.pallas_call(
        _adaln_kernel,
        grid=(n // tn,),
        in_specs=[pl.BlockSpec((rows, d), lambda j: (0, 0)),
                  pl.BlockSpec((d, tn), lambda j: (0, j)),
                  pl.BlockSpec((1, tn), lambda j: (0, j))],
        out_specs=pl.BlockSpec((rows, tn), lambda j: (0, j)),
        out_shape=jax.ShapeDtypeStruct((rows, n), F32),
        compiler_params=_params(("arbitrary",)),
        name="adaln",
    )(c_all, w_ada, b_ada)


def _inproj_kernel(x_ref, sh_ref, sc_ref, g1_ref, w_ref, lng_ref, lnb_ref,
                   q_ref, k_ref, v_ref, u_ref, vn_ref, h_scr):
    j = pl.program_id(1)

    @pl.when(j == 0)
    def _():
        y = _rms(x_ref[...], g1_ref[...])
        h_scr[...] = (y * (1.0 + sc_ref[...]) + sh_ref[...]).astype(BF16)

    p = jnp.dot(h_scr[...], w_ref[...], preferred_element_type=F32)

    @pl.when(j == 0)
    def _():
        q_ref[...] = (p * (HEAD_DIM ** -0.5)).astype(BF16)

    @pl.when(j == 1)
    def _():
        k_ref[...] = p

    @pl.when(j == 2)
    def _():
        v_ref[...] = p

    @pl.when(j == 3)
    def _():
        u_ref[...] = _gelu(p).astype(BF16)

    @pl.when(j == 4)
    def _():
        g = _gelu(p)
        mu = jnp.mean(g, axis=-1, keepdims=True)
        gc = g - mu
        var = jnp.mean(gc * gc, axis=-1, keepdims=True)
        vn_ref[...] = gc * lax.rsqrt(var + EPS) * lng_ref[...] + lnb_ref[...]


def _mod_spec(mod, tiles_per_row):
    _, r, d = mod.shape
    return pl.BlockSpec((None, r, d), lambda i, *_: (i // tiles_per_row, 0, 0))


def _inproj(x, sh1, sc1, g1, w_in, lng, lnb, tm, tiles_per_mod):
    t, d = x.shape
    sec = ATT_WIDTH
    n_sec = w_in.shape[1] // sec
    tok = lambda i, j: (i, 0)
    const = lambda i, j: (0, 0)
    out_spec = pl.BlockSpec((tm, sec), tok)
    return pl.pallas_call(
        _inproj_kernel,
        grid=(t // tm, n_sec),
        in_specs=[pl.BlockSpec((tm, d), tok),
                  _mod_spec(sh1, tiles_per_mod), _mod_spec(sc1, tiles_per_mod),
                  pl.BlockSpec((1, d), const),
                  pl.BlockSpec((d, sec), lambda i, j: (0, j)),
                  pl.BlockSpec((1, sec), const), pl.BlockSpec((1, sec), const)],
        out_specs=[out_spec] * 5,
        out_shape=[jax.ShapeDtypeStruct((t, sec), BF16),
                   jax.ShapeDtypeStruct((t, sec), F32),
                   jax.ShapeDtypeStruct((t, sec), F32),
                   jax.ShapeDtypeStruct((t, sec), BF16),
                   jax.ShapeDtypeStruct((t, sec), F32)],
        scratch_shapes=[pltpu.VMEM((tm, d), BF16)],
        compiler_params=_params(("parallel", "arbitrary")),
        name="inproj",
    )(x, sh1, sc1, g1, w_in, lng, lnb)


def _att_kernel(q_ref, kc_ref, kp_ref, vc_ref, vp_ref, bias_ref, o_ref, lse_ref, *, mask_first):
    n = pl.program_id(2)
    lane_head = lax.broadcasted_iota(jnp.int32, (BAND, LANES), 1) // STAT_REP
    is_prev = lax.broadcasted_iota(jnp.int32, (BAND, 2 * BAND), 1) < BAND
    stats = jnp.zeros((BAND, LANES), F32)
    for h in range(N_HEADS):
        cs = slice(h * HEAD_DIM, (h + 1) * HEAD_DIM)
        kw = jnp.concatenate([kp_ref[:, cs], kc_ref[:, cs]], axis=0).astype(BF16)
        vw = jnp.concatenate([vp_ref[:, cs], vc_ref[:, cs]], axis=0).astype(BF16)
        s = lax.dot_general(q_ref[:, cs], kw, (((1,), (1,)), ((), ())),
                            preferred_element_type=F32)
        s = s + bias_ref[h]
        if mask_first:
            s = jnp.where(jnp.logical_and(n == 0, is_prev), NEG, s)
        m = jnp.max(s, axis=-1, keepdims=True)
        p = jnp.exp(s - m)
        l = jnp.sum(p, axis=-1, keepdims=True)
        o = jnp.dot(p.astype(BF16), vw, preferred_element_type=F32)
        o_ref[:, cs] = o / l
        stats = jnp.where(lane_head == h, m + jnp.log(l), stats)
    lse_ref[...] = stats


def _att_bias(dil):
    a = np.arange(BAND)[:, None]
    kk = np.arange(2 * BAND)[None, :]
    steps = BAND + a - kk
    valid = (steps >= 0) & (steps <= BAND)
    slopes = 2.0 ** (-8.0 * np.arange(1, N_HEADS + 1) / N_HEADS)
    bias = -slopes[:, None, None] * (dil * steps).astype(np.float64)[None]
    return jnp.asarray(np.where(valid[None], bias, NEG), dtype=F32)


def _attention(q, kc, kp, vc, vp, dil, grid, cur_map, prev_map, mask_first):
    blk = lambda m: pl.BlockSpec((None, BAND, ATT_WIDTH), m)
    b, rows, cols = q.shape
    return pl.pallas_call(
        functools.partial(_att_kernel, mask_first=mask_first),
        grid=grid,
        in_specs=[blk(cur_map), blk(cur_map), blk(prev_map), blk(cur_map), blk(prev_map),
                  pl.BlockSpec((N_HEADS, BAND, 2 * BAND), lambda *_: (0, 0, 0))],
        out_specs=[blk(cur_map), pl.BlockSpec((None, BAND, LANES), cur_map)],
        out_shape=[jax.ShapeDtypeStruct((b, rows, cols), F32),
                   jax.ShapeDtypeStruct((b, rows, cols // N_HEADS), F32)],
        compiler_params=_params(("parallel", "parallel", "arbitrary")),
        name=f"attention_d{dil}",
    )(q, kc, kp, vc, vp, _att_bias(dil))


def _prompt_attention(q, k, v, batch, seq):
    res = []
    for win, dil in BRANCHES:
        sub = seq // dil
        view = lambda t: t.reshape(batch, sub, dil * ATT_WIDTH)
        cur = lambda b, r, n: (b, n, r)
        prev = lambda b, r, n: (b, jnp.maximum(n - 1, 0), r)
        o, lse = _attention(view(q), view(k), view(k), view(v), view(v), dil,
                            (batch, dil, sub // BAND), cur, prev, True)
        res.append((o.reshape(batch * seq, ATT_WIDTH), lse.reshape(batch * seq, LANES)))
    return res


def _sample_attention(q, k, v, cache_k, cache_v, n_batch, n_new):
    buf = cache_k.shape[1]
    tok = lambda t: t.reshape(n_batch, n_new, ATT_WIDTH)
    res = []
    for win, dil in BRANCHES:
        sub = buf // dil
        prev_blk = sub // BAND - 1
        if dil == 1:
            pad = lambda t: jnp.pad(tok(t), ((0, 0), (0, BAND - n_new), (0, 0)))
            grid = (n_batch, 1, 1)
            cur = lambda b, r, n: (b, 0, 0)
            pick = lambda o: o[:, :n_new]
        else:
            pad = lambda t: jnp.pad(tok(t)[:, :, None, :], ((0, 0), (0, 0), (0, BAND - 1), (0, 0))
                                    ).reshape(n_batch, n_new * BAND, ATT_WIDTH)
            grid = (n_batch, n_new, 1)
            cur = lambda b, r, n: (b, r, 0)
            pick = lambda o: o.reshape(n_batch, n_new, BAND, -1)[:, :, 0]
        prev = lambda b, r, n, pb=prev_blk: (b, pb, r)
        view = lambda c: c.reshape(n_batch, sub, dil * ATT_WIDTH)
        o, lse = _attention(pad(q), pad(k), view(cache_k), pad(v), view(cache_v), dil,
                            grid, cur, prev, False)
        res.append((pick(o).reshape(n_batch * n_new, ATT_WIDTH),
                    pick(lse).reshape(n_batch * n_new, LANES)))
    return res


def _mix_kernel(o1_ref, o2_ref, o3_ref, l1_ref, l2_ref, l3_ref, u_ref, vn_ref, x_ref,
                gt1_ref, sh2_ref, sc2_ref, g2_ref, wmix_ref, bmix_ref, wout_ref, exp_ref,
                x1_ref, h2_ref, h2t_ref, mix_scr):
    tm = x_ref.shape[0]
    l1, l2, l3 = l1_ref[...], l2_ref[...], l3_ref[...]
    lm = jnp.maximum(jnp.maximum(l1, l2), l3)
    e1, e2, e3 = jnp.exp(l1 - lm), jnp.exp(l2 - lm), jnp.exp(l3 - lm)
    den = e1 + e2 + e3
    att = jnp.zeros((tm, ATT_WIDTH), F32)
    for e, o_ref in ((e1, o1_ref), (e2, o2_ref), (e3, o3_ref)):
        w = e / den
        w_hi = w.astype(BF16)
        w_lo = (w - w_hi.astype(F32)).astype(BF16)
        wide = (jnp.dot(w_hi, exp_ref[...], preferred_element_type=F32)
                + jnp.dot(w_lo, exp_ref[...], preferred_element_type=F32))
        att = att + wide * o_ref[...]
    mix_scr[:, :ATT_WIDTH] = att.astype(BF16)
    for c in range(tm // BAND):
        rows = slice(c * BAND, (c + 1) * BAND)
        for g in range(N_HEADS):
            cs = slice(g * HEAD_DIM, (g + 1) * HEAD_DIM)
            m = jnp.dot(wmix_ref[g], vn_ref[rows, cs].astype(BF16),
                        preferred_element_type=F32) + bmix_ref[g]
            mix_scr[rows, ATT_WIDTH + g * HEAD_DIM:ATT_WIDTH + (g + 1) * HEAD_DIM] = (
                u_ref[rows, cs].astype(F32) * m).astype(BF16)
    y = jnp.dot(mix_scr[...], wout_ref[...], preferred_element_type=F32)
    x1 = x_ref[...] + gt1_ref[...] * y
    x1_ref[...] = x1
    h2 = _rms(x1, g2_ref[...]) * (1.0 + sc2_ref[...]) + sh2_ref[...]
    h2_ref[...] = h2.astype(BF16)
    h2t_ref[...] = h2.T.astype(BF16)


def _expand_matrix():
    lane = np.arange(LANES)[:, None]
    col = np.arange(ATT_WIDTH)[None, :]
    return jnp.asarray(lane == STAT_REP * (col // HEAD_DIM), dtype=BF16)


def _mix(att_res, u, vn, x, gt1, sh2, sc2, g2, wmix, bmix, w_out, tm, tiles_per_mod):
    t, d = x.shape
    tok = lambda i: (i, 0)
    c2 = lambda i: (0, 0)
    c3 = lambda i: (0, 0, 0)
    o_spec = pl.BlockSpec((tm, ATT_WIDTH), tok)
    l_spec = pl.BlockSpec((tm, LANES), tok)
    (o1, l1), (o2, l2), (o3, l3) = att_res
    return pl.pallas_call(
        _mix_kernel,
        grid=(t // tm,),
        in_specs=[o_spec, o_spec, o_spec, l_spec, l_spec, l_spec,
                  pl.BlockSpec((tm, ATT_WIDTH), tok), pl.BlockSpec((tm, ATT_WIDTH), tok),
                  pl.BlockSpec((tm, d), tok),
                  _mod_spec(gt1, tiles_per_mod), _mod_spec(sh2, tiles_per_mod),
                  _mod_spec(sc2, tiles_per_mod),
                  pl.BlockSpec((1, d), c2),
                  pl.BlockSpec(wmix.shape, c3), pl.BlockSpec(bmix.shape, c3),
                  pl.BlockSpec(w_out.shape, c2), pl.BlockSpec((LANES, ATT_WIDTH), c2)],
        out_specs=[pl.BlockSpec((tm, d), tok), pl.BlockSpec((tm, d), tok),
                   pl.BlockSpec((d, tm), lambda i: (0, i))],
        out_shape=[jax.ShapeDtypeStruct((t, d), F32),
                   jax.ShapeDtypeStruct((t, d), BF16),
                   jax.ShapeDtypeStruct((d, t), BF16)],
        scratch_shapes=[pltpu.VMEM((tm, d), BF16)],
        compiler_params=_params(("parallel",)),
        name="mix_outproj",
    )(o1, o2, o3, l1, l2, l3, u, vn, x, gt1, sh2, sc2, g2, wmix, bmix, w_out, _expand_matrix())


def _cand_layout():
    rows = []
    rows += [(0, j) for j in range(16)]
    rows += [(1, j) for j in range(8)]
    rows += [(8 + r, 0) for r in range(8)]
    rows += [(2, j) if j < 5 else None for j in range(8)]
    rows += [(3, j) if j < 4 else None for j in range(8)]
    rows += [(4, j) if j < 3 else None for j in range(8)]
    rows += [(r, 0) if 5 <= r < 8 else None for r in range(8)]
    rows += [(r, 1) if 5 <= r < 8 else None for r in range(8)]
    return rows


N_CAND_ROWS = 72
BIG_FLAT = 1e9


def _cand_flat():
    flat = np.full((N_CAND_ROWS, LANES), BIG_FLAT, np.float32)
    for r, cell in enumerate(_cand_layout()):
        if cell is not None:
            flat[r, :] = 16 * cell[0] + cell[1]
    return jnp.asarray(flat)


def _top16(s, row_id, top_ref):
    rank = jnp.full(s.shape, float(TOPK), F32)
    for it in range(TOPK):
        m = jnp.max(s, axis=0, keepdims=True)
        idx = jnp.min(jnp.where(s == m, row_id, float(N_KEYS)), axis=0, keepdims=True)
        sel = row_id == idx
        rank = jnp.where(sel, float(it), rank)
        s = jnp.where(sel, -jnp.inf, s)
        top_ref[it:it + 1, :] = m
    return rank


def _select_kernel(h2_ref, wq_ref, keys_ref, flat_ref, cnt_ref, g1_ref, r2_ref, e2_ref,
                   qp_scr, top1_scr, top2_scr):
    tm = h2_ref.shape[0]
    qp_scr[...] = jnp.dot(h2_ref[...], wq_ref[...], preferred_element_type=F32).astype(BF16)
    row_id = lax.broadcasted_iota(jnp.int32, (N_KEYS, LANES), 0).astype(F32)
    row8 = lax.broadcasted_iota(jnp.int32, (8, LANES), 0)
    flat = flat_ref[...]
    valid = flat < BIG_FLAT
    for lc in range(tm // LANES):
        toks = slice(lc * LANES, (lc + 1) * LANES)

        def head_body(h, carry, toks=toks, lc=lc):
            sides = []
            for p, top_scr in ((0, top1_scr), (1, top2_scr)):
                col = pl.multiple_of((2 * h + p) * N_KEYS, N_KEYS)
                s = lax.dot_general(keys_ref[p], qp_scr[toks, pl.ds(col, N_KEYS)],
                                    (((1,), (1,)), ((), ())), preferred_element_type=F32)
                sides.append((s, _top16(s, row_id, top_scr)))
            (s1, rank1), (s2, rank2) = sides
            a1 = top1_scr[...]
            a2 = top2_scr[...]
            v = jnp.concatenate([
                a1[0:1] + a2[0:16],
                a1[1:2] + a2[0:8],
                a1[8:16] + a2[0:1],
                a1[2:3] + a2[0:8],
                a1[3:4] + a2[0:8],
                a1[4:5] + a2[0:8],
                a1[0:8] + a2[0:1],
                a1[0:8] + a2[1:2]], axis=0)
            vw = jnp.where(valid, v, -jnp.inf)
            selm = jnp.zeros(v.shape, F32)
            for _ in range(TOPK):
                m = jnp.max(vw, axis=0, keepdims=True)
                f = jnp.min(jnp.where(vw == m, flat, BIG_FLAT), axis=0, keepdims=True)
                sel = flat == f
                selm = jnp.where(sel, 1.0, selm)
                vw = jnp.where(sel, -jnp.inf, vw)
            vmax = a1[0:1] + a2[0:1]
            z = jnp.sum(jnp.where(selm > 0.0, jnp.exp(v - vmax), 0.0), axis=0, keepdims=True)
            cnt_lo = selm[56:64] + selm[64:72]
            for i, (lo, hi) in enumerate(((0, 16), (16, 24), (32, 40), (40, 48), (48, 56))):
                ci = jnp.sum(selm[lo:hi], axis=0, keepdims=True)
                cnt_lo = jnp.where(row8 == i, ci, cnt_lo)
            cnt_hi = selm[24:32]
            cnt_a = jnp.zeros((N_KEYS, LANES), F32)
            for i in range(TOPK):
                ci = cnt_lo[i:i + 1] if i < 8 else cnt_hi[i - 8:i - 7]
                cnt_a = jnp.where(rank1 == float(i), ci, cnt_a)
            cnt_ref[h, lc] = cnt_a
            g1_ref[h, lc] = jnp.exp(s1 - a1[0:1]) / z
            r2_ref[h, lc] = rank2
            e2_ref[h, lc] = jnp.exp(s2 - a2[0:1])
            return carry

        lax.fori_loop(0, N_HEADS, head_body, 0)


def _select(h2, w_query, sub_keys, tm):
    t, d = h2.shape
    out_spec = pl.BlockSpec((N_HEADS, tm // LANES, N_KEYS, LANES), lambda i: (0, i, 0, 0))
    out_shape = jax.ShapeDtypeStruct((N_HEADS, t // LANES, N_KEYS, LANES), F32)
    return pl.pallas_call(
        _select_kernel,
        grid=(t // tm,),
        in_specs=[pl.BlockSpec((tm, d), lambda i: (i, 0)),
                  pl.BlockSpec(w_query.shape, lambda i: (0, 0)),
                  pl.BlockSpec(sub_keys.shape, lambda i: (0, 0, 0)),
                  pl.BlockSpec((N_CAND_ROWS, LANES), lambda i: (0, 0))],
        out_specs=[out_spec] * 4,
        out_shape=[out_shape] * 4,
        scratch_shapes=[pltpu.VMEM((tm, w_query.shape[1]), BF16),
                        pltpu.VMEM((TOPK, LANES), F32), pltpu.VMEM((TOPK, LANES), F32)],
        compiler_params=_params(("parallel",)),
        name="peer_select",
    )(h2, w_query, sub_keys, _cand_flat())


def _coef_kernel(cnt_ref, g1_ref, r2_ref, e2_ref, c_ref):
    n_l, n_a = cnt_ref.shape[1], cnt_ref.shape[2]
    for lc in range(n_l):
        for ai in range(n_a):
            rows = slice(ai * N_KEYS, (ai + 1) * N_KEYS)
            coef = jnp.zeros((N_KEYS, LANES), F32)
            for h in range(N_HEADS):
                cnt = cnt_ref[h, lc, ai:ai + 1, :]
                g1 = g1_ref[h, lc, ai:ai + 1, :]
                coef = coef + jnp.where(r2_ref[h, lc] < cnt, e2_ref[h, lc], 0.0) * g1
            c_ref[lc, rows, :] = coef.astype(BF16)


def _coef(sel, te, n_exp):
    n_lc = sel[0].shape[1]
    n_l = min(n_lc, COEF_LANE_CHUNKS)
    a_spec = pl.BlockSpec((N_HEADS, n_l, te // N_KEYS, LANES), lambda i, j: (0, i, j, 0))
    b_spec = pl.BlockSpec((N_HEADS, n_l, N_KEYS, LANES), lambda i, j: (0, i, 0, 0))
    return pl.pallas_call(
        _coef_kernel,
        grid=(n_lc // n_l, n_exp // te),
        in_specs=[a_spec, a_spec, b_spec, b_spec],
        out_specs=pl.BlockSpec((n_l, te, LANES), lambda i, j: (i, j, 0)),
        out_shape=jax.ShapeDtypeStruct((n_lc, n_exp, LANES), BF16),
        compiler_params=_params(("parallel", "arbitrary")),
        name="peer_coef",
    )(*sel)


def _peer_kernel(h2t_ref, u_ref, vt_ref, c_ref, out_ref):
    j = pl.program_id(1)
    tt = h2t_ref.shape[1]

    @pl.when(j == 0)
    def _():
        out_ref[...] = jnp.zeros_like(out_ref)

    act = jnp.dot(u_ref[...], h2t_ref[...], preferred_element_type=F32)
    p = jnp.concatenate(
        [_gelu(act[:, lc * LANES:(lc + 1) * LANES]).astype(BF16) * c_ref[lc]
         for lc in range(tt // LANES)], axis=1)
    out_ref[...] += jnp.dot(vt_ref[...], p, preferred_element_type=F32)


def _peer(h2t, u, vt, sel, tt, te):
    d, t = h2t.shape
    n_exp = u.shape[0]
    coef = _coef(sel, te, n_exp)
    return pl.pallas_call(
        _peer_kernel,
        grid=(t // tt, n_exp // te),
        in_specs=[pl.BlockSpec((d, tt), lambda i, j: (0, i)),
                  pl.BlockSpec((te, d), lambda i, j: (j, 0)),
                  pl.BlockSpec((d, te), lambda i, j: (0, j)),
                  pl.BlockSpec((tt // LANES, te, LANES), lambda i, j: (i, j, 0))],
        out_specs=pl.BlockSpec((d, tt), lambda i, j: (0, i)),
        out_shape=jax.ShapeDtypeStruct((d, t), F32),
        compiler_params=_params(("parallel", "arbitrary")),
        name="peer_dense",
    )(h2t, u, vt, coef)


def _final_kernel(x1_ref, ffnt_ref, gt2_ref, gf_ref, y_ref):
    x2 = x1_ref[...] + gt2_ref[...] * ffnt_ref[...].T
    y_ref[...] = _rms(x2, gf_ref[...])


def _final(x1, ffnt, gt2, g_final, tm, tiles_per_mod):
    t, d = x1.shape
    return pl.pallas_call(
        _final_kernel,
        grid=(t // tm,),
        in_specs=[pl.BlockSpec((tm, d), lambda i: (i, 0)),
                  pl.BlockSpec((d, tm), lambda i: (0, i)),
                  _mod_spec(gt2, tiles_per_mod),
                  pl.BlockSpec((1, d), lambda i: (0, 0))],
        out_specs=pl.BlockSpec((tm, d), lambda i: (i, 0)),
        out_shape=jax.ShapeDtypeStruct((t, d), F32),
        compiler_params=_params(("parallel",)),
        name="final_norm",
    )(x1, ffnt, gt2, g_final)


def _trunk(x, mods, att_fn, w, wmix, bmix, tm_in, tm_mix, tm_sel, tt, te, tiles_per_mod):
    sh1, sc1, gt1, sh2, sc2, gt2 = mods
    q, k, v, u, vn = _inproj(x, sh1, sc1, w["g1"], w["w_in"], w["lng"], w["lnb"], tm_in,
                             tiles_per_mod(tm_in))
    att_res = att_fn(q, k, v)
    x1, h2, h2t = _mix(att_res, u, vn, x, gt1, sh2, sc2, w["g2"], wmix, bmix, w["w_out"],
                       tm_mix, tiles_per_mod(tm_mix))
    sel = _select(h2, w["w_query"], w["sub_keys"], tm_sel)
    ffnt = _peer(h2t, w["u"], w["vt"], sel, tt, te)
    y = _final(x1, ffnt, gt2, w["g_final"], tm_mix, tiles_per_mod(tm_mix))
    return y, k, v, vn


def kernel(x_prompt, x_sample, cache_k, cache_v, c_prompt, c_sample, w_ada, b_ada, g_norm1, w_in,
           ln_v_g, ln_v_b, w_spatial, b_spatial, w_out, g_norm2, w_query, sub_keys, expert_u,
           expert_v, g_final):
    depth = w_ada.shape[0]
    assert depth == 1, "single-layer trunk"
    batch, seq, d = x_prompt.shape
    n_dec, n_new, _ = x_sample.shape
    buf = cache_k.shape[2]
    assert seq % (BAND * BRANCHES[-1][1]) == 0 and buf == BRANCHES[-1][0] and n_new <= 4
    n_tok_s = n_dec * n_new
    pad_s = BAND
    assert n_tok_s <= pad_s

    n_c = batch + n_dec
    c_all = jnp.pad(jnp.concatenate([c_prompt, c_sample], axis=0), ((0, -n_c % 8), (0, 0)))
    mod = _adaln(c_all, w_ada[0], b_ada[0][None, :])
    mods = [mod[:, i * d:(i + 1) * d] for i in range(6)]
    mods_p = [m[:batch, None, :] for m in mods]
    mods_s = [jnp.pad(jnp.repeat(m[batch:n_c], n_new, axis=0), ((0, pad_s - n_tok_s), (0, 0)))[None]
              for m in mods]

    tri = jnp.tril(jnp.ones((BAND, BAND), dtype=bool))
    ws = jnp.where(tri, w_spatial[0], 0.0)
    bs = b_spatial[0]
    wmix_p = ws.astype(BF16)
    bmix_p = jnp.broadcast_to(bs[:, :, None], (N_HEADS, BAND, HEAD_DIM))
    eye = jnp.eye(pad_s // n_new, dtype=F32)
    wmix_s = jax.vmap(lambda m: jnp.kron(eye, m))(ws[:, :n_new, :n_new]).astype(BF16)
    bmix_s = jnp.broadcast_to(jnp.tile(bs[:, :n_new], (1, pad_s // n_new))[:, :, None],
                              (N_HEADS, pad_s, HEAD_DIM))

    w = dict(g1=g_norm1[0][None, :], w_in=w_in[0].astype(BF16), lng=ln_v_g[0][None, :],
             lnb=ln_v_b[0][None, :], g2=g_norm2[0][None, :], w_out=w_out[0].astype(BF16),
             w_query=w_query[0].astype(BF16), sub_keys=sub_keys[0].astype(BF16),
             u=expert_u[0].astype(BF16), vt=expert_v[0].T.astype(BF16), g_final=g_final[None, :])

    xp = x_prompt.reshape(batch * seq, d)
    y_p, k_p, v_p, _ = _trunk(
        xp, mods_p, lambda q, k, v: _prompt_attention(q, k, v, batch, seq), w, wmix_p, bmix_p,
        tm_in=512, tm_mix=256, tm_sel=256, tt=512, te=1024, tiles_per_mod=lambda tm: seq // tm)

    xs = jnp.pad(x_sample.reshape(n_tok_s, d), ((0, pad_s - n_tok_s), (0, 0)))
    ck = cache_k[0].reshape(n_dec, buf, ATT_WIDTH)
    cv = cache_v[0].reshape(n_dec, buf, ATT_WIDTH)

    def sample_att(q, k, v):
        res = _sample_attention(q[:n_tok_s], k[:n_tok_s], v[:n_tok_s], ck, cv, n_dec, n_new)
        padr = lambda t: jnp.pad(t, ((0, pad_s - n_tok_s), (0, 0)))
        return [(padr(o), padr(l)) for o, l in res]

    y_s, k_s, v_s, vn_s = _trunk(
        xs, mods_s, sample_att, w, wmix_s, bmix_s,
        tm_in=pad_s, tm_mix=pad_s, tm_sel=pad_s, tt=pad_s, te=1024, tiles_per_mod=lambda tm: 1)

    win = min(BRANCHES[-1][0], seq)
    heads = lambda t, b, s: t.reshape(b, s, N_HEADS, HEAD_DIM)
    k_win = heads(k_p, batch, seq)[:, -win:][None]
    v_win = heads(v_p, batch, seq)[:, -win:][None]
    return (y_p.reshape(batch, seq, d),
            y_s[:n_tok_s].reshape(n_dec, n_new, d),
            k_win, v_win,
            heads(k_s[:n_tok_s], n_dec, n_new)[None],
            heads(v_s[:n_tok_s], n_dec, n_new)[None],
            vn_s[:n_tok_s].reshape(n_dec, n_new, ATT_WIDTH)[None])
```

```python
import functools

import numpy as np
import jax
import jax.numpy as jnp
from jax import lax
from jax.experimental import pallas as pl
from jax.experimental.pallas import tpu as pltpu

F32 = jnp.float32
BF16 = jnp.bfloat16

EPS = 1e-6
LANES = 128
SUBLANES = 8
HEAD_DIM = 128
N_HEADS = 8
ATT_WIDTH = N_HEADS * HEAD_DIM
BRANCHES = ((128, 1), (512, 4), (2048, 16))
BAND = 128
SPAN = BAND * BRANCHES[-1][1]
N_KEYS = 128
TOPK = 16
NEG = -1e30
VMEM_LIMIT = 56 * 1024 * 1024
COEF_LANE_CHUNKS = 4
SLOTS = SUBLANES


def _params(sem, flags=None):
    return pltpu.CompilerParams(dimension_semantics=sem, vmem_limit_bytes=VMEM_LIMIT, flags=flags)


def _gelu(x):
    return 0.5 * x * (1.0 + lax.erf(x * (2.0 ** -0.5)))


def _rms(x, g):
    return x * lax.rsqrt(jnp.mean(x * x, axis=-1, keepdims=True) + EPS) * g


def _slope(h):
    return 2.0 ** (-8.0 * (h + 1) / N_HEADS)


def _adaln_kernel(c_ref, w_ref, b_ref, o_ref):
    c = c_ref[...]
    a = c / (1.0 + jnp.exp(-c))
    o_ref[...] = jnp.dot(a.astype(BF16), w_ref[...].astype(BF16),
                         preferred_element_type=F32) + b_ref[...]


def _adaln(c_all, w_ada, b_ada):
    rows, d = c_all.shape
    n = w_ada.shape[1]
    tn = 1024
    return pl.pallas_call(
        _adaln_kernel,
        grid=(n // tn,),
        in_specs=[pl.BlockSpec((rows, d), lambda j: (0, 0)),
                  pl.BlockSpec((d, tn), lambda j: (0, j)),
                  pl.BlockSpec((1, tn), lambda j: (0, j))],
        out_specs=pl.BlockSpec((rows, tn), lambda j: (0, j)),
        out_shape=jax.ShapeDtypeStruct((rows, n), F32),
        compiler_params=_params(("arbitrary",)),
        name="adaln",
    )(c_all, w_ada, b_ada)


def _inproj_kernel(x_ref, sh_ref, sc_ref, g1_ref, w_ref, lng_ref, lnb_ref,
                   q_ref, k_ref, v_ref, u_ref, vn_ref, *rest, windows):
    if windows:
        kwin_ref, vwin_ref, h_scr = rest
    else:
        (h_scr,) = rest
    j = pl.program_id(1)

    @pl.when(j == 0)
    def _():
        y = _rms(x_ref[...], g1_ref[...])
        h_scr[...] = (y * (1.0 + sc_ref[...]) + sh_ref[...]).astype(BF16)

    p = jnp.dot(h_scr[...], w_ref[...], preferred_element_type=F32)

    def heads_out(ref, val, win_ref=None):
        for h in range(N_HEADS):
            piece = val[:, h * HEAD_DIM:(h + 1) * HEAD_DIM]
            ref[h] = piece
            if win_ref is not None:
                win_ref[:, h, :] = piece

    @pl.when(j == 0)
    def _():
        heads_out(q_ref, p * (HEAD_DIM ** -0.5))

    @pl.when(j == 1)
    def _():
        heads_out(k_ref, p, kwin_ref if windows else None)

    @pl.when(j == 2)
    def _():
        heads_out(v_ref, p, vwin_ref if windows else None)

    @pl.when(j == 3)
    def _():
        u_ref[...] = _gelu(p).astype(BF16)

    @pl.when(j == 4)
    def _():
        g = _gelu(p)
        mu = jnp.mean(g, axis=-1, keepdims=True)
        gc = g - mu
        var = jnp.mean(gc * gc, axis=-1, keepdims=True)
        vn_ref[...] = gc * lax.rsqrt(var + EPS) * lng_ref[...] + lnb_ref[...]


def _mod_spec(mod, tiles_per_group):
    _, r, d = mod.shape
    return pl.BlockSpec((None, r, d), lambda i, *_: (i // tiles_per_group, 0, 0))


def _inproj(x, sh1, sc1, g1, w_in, lng, lnb, tm, n_groups, window):
    t, d = x.shape
    sec = ATT_WIDTH
    n_sec = w_in.shape[1] // sec
    rows = t // n_groups
    tpg = rows // tm
    tok = lambda i, j: (i, 0)
    const = lambda i, j: (0, 0)
    out_spec = pl.BlockSpec((tm, sec), tok)
    head_spec = pl.BlockSpec((None, N_HEADS, tm, HEAD_DIM), lambda i, j: (i // tpg, 0, i % tpg, 0))
    head_shape = jax.ShapeDtypeStruct((n_groups, N_HEADS, rows, HEAD_DIM), F32)
    out_specs = [head_spec] * 3 + [out_spec] * 2
    out_shape = [head_shape] * 3 + [jax.ShapeDtypeStruct((t, sec), BF16),
                                    jax.ShapeDtypeStruct((t, sec), F32)]
    if window:
        wt = window // tm
        win_spec = pl.BlockSpec((tm, N_HEADS, HEAD_DIM), lambda i, j: (
            (i // tpg) * wt + jnp.maximum(i % tpg - (tpg - wt), 0), 0, 0))
        win_shape = jax.ShapeDtypeStruct((n_groups * window, N_HEADS, HEAD_DIM), F32)
        out_specs += [win_spec] * 2
        out_shape += [win_shape] * 2
    return pl.pallas_call(
        functools.partial(_inproj_kernel, windows=bool(window)),
        grid=(t // tm, n_sec),
        in_specs=[pl.BlockSpec((tm, d), tok),
                  _mod_spec(sh1, tpg), _mod_spec(sc1, tpg),
                  pl.BlockSpec((1, d), const),
                  pl.BlockSpec((d, sec), lambda i, j: (0, j)),
                  pl.BlockSpec((1, sec), const), pl.BlockSpec((1, sec), const)],
        out_specs=out_specs,
        out_shape=out_shape,
        scratch_shapes=[pltpu.VMEM((tm, d), BF16)],
        compiler_params=_params(("arbitrary", "arbitrary")),
        name="inproj",
    )(x, sh1, sc1, g1, w_in, lng, lnb)


def _att_bias():
    a = np.arange(BAND)[:, None]
    kk = np.arange(2 * BAND)[None, :]
    steps = BAND + a - kk
    valid = (steps >= 0) & (steps <= BAND)
    slopes = np.array([_slope(h) for h in range(N_HEADS)])
    out = []
    for _, dil in BRANCHES:
        bias = -slopes[:, None, None] * (dil * steps).astype(np.float64)[None]
        out.append(np.where(valid[None], bias, NEG))
    return jnp.asarray(np.stack(out), dtype=F32)


def _att_prompt_kernel(q_ref, kc_ref, kp_ref, vc_ref, vp_ref, bias_ref, o_ref, m_scr, l_scr, acc_scr):
    first_span = pl.program_id(2) == 0
    span = q_ref.shape[0]
    is_prev = lax.broadcasted_iota(jnp.int32, (BAND, 2 * BAND), 1) < BAND

    def rows_at(start, dil):
        return pl.ds(start, BAND) if dil == 1 else pl.ds(start, BAND, stride=dil)

    for bi, (_, dil) in enumerate(BRANCHES):
        for r in range(dil):
            for n in range(span // (BAND * dil)):
                start = r + dil * BAND * n
                rows = rows_at(start, dil)
                if n > 0:
                    prev = rows_at(start - dil * BAND, dil)
                    kp, vp = kc_ref[prev, :], vc_ref[prev, :]
                else:
                    prev = rows_at(span - dil * BAND + r, dil)
                    kp, vp = kp_ref[prev, :], vp_ref[prev, :]
                kw = jnp.concatenate([kp, kc_ref[rows, :]], axis=0).astype(BF16)
                vw = jnp.concatenate([vp, vc_ref[rows, :]], axis=0).astype(BF16)
                s = lax.dot_general(q_ref[rows, :].astype(BF16), kw, (((1,), (1,)), ((), ())),
                                    preferred_element_type=F32) + bias_ref[bi]
                if n == 0:
                    s = jnp.where(jnp.logical_and(first_span, is_prev), NEG, s)
                m_b = jnp.max(s, axis=-1, keepdims=True)
                p = jnp.exp(s - m_b)
                l_b = jnp.sum(p, axis=-1, keepdims=True)
                pv = jnp.dot(p.astype(BF16), vw, preferred_element_type=F32)
                if bi == 0:
                    m_scr[rows, :] = jnp.broadcast_to(m_b, (BAND, LANES))
                    l_scr[rows, :] = jnp.broadcast_to(l_b, (BAND, LANES))
                    acc_scr[rows, :] = pv
                else:
                    m_o = m_scr[rows, :]
                    m_n = jnp.maximum(m_o, m_b)
                    e_o = jnp.exp(m_o - m_n)
                    e_b = jnp.exp(m_b - m_n)
                    m_scr[rows, :] = m_n
                    l_scr[rows, :] = e_o * l_scr[rows, :] + e_b * l_b
                    acc_scr[rows, :] = e_o * acc_scr[rows, :] + e_b * pv
    o_ref[...] = (acc_scr[...] / l_scr[...]).astype(o_ref.dtype)


def _prompt_attention(q, k, v):
    b, nh, seq, hd = q.shape
    cur = lambda bb, h, s: (bb, h, s, 0)
    prev = lambda bb, h, s: (bb, h, jnp.maximum(s - 1, 0), 0)
    blk = lambda m: pl.BlockSpec((None, None, SPAN, hd), m)
    return pl.pallas_call(
        _att_prompt_kernel,
        grid=(b, nh, seq // SPAN),
        in_specs=[blk(cur), blk(cur), blk(prev), blk(cur), blk(prev),
                  pl.BlockSpec((len(BRANCHES), None, BAND, 2 * BAND), lambda bb, h, s: (0, h, 0, 0))],
        out_specs=blk(cur),
        out_shape=jax.ShapeDtypeStruct(q.shape, BF16),
        scratch_shapes=[pltpu.VMEM((SPAN, LANES), F32)] * 3,
        compiler_params=_params(("parallel", "parallel", "arbitrary")),
        name="attention_prompt",
    )(q, k, k, v, v, _att_bias())


def _att_sample_kernel(q_ref, kn_ref, vn_ref, kt_ref, vt_ref, kr_ref, vr_ref, o_ref, *, n_new):
    o_ref[...] = jnp.zeros_like(o_ref)
    col = lax.broadcasted_iota(jnp.int32, (BAND, 1), 0).astype(F32)
    slot = lax.broadcasted_iota(jnp.int32, (SLOTS, 1), 0)
    tail = kt_ref.shape[0] // N_HEADS
    for h in range(N_HEADS):
        slope = _slope(h)
        knh, vnh = kn_ref[h], vn_ref[h]
        for t in range(n_new):
            q_row = q_ref[h, t:t + 1, :]
            parts = []
            for _, dil in BRANCHES:
                a = t if dil == 1 else 0
                if dil == 1:
                    rows = pl.ds((tail - BAND) * N_HEADS + h, BAND, stride=N_HEADS)
                    kp, vp = kt_ref[rows, :], vt_ref[rows, :]
                elif dil * BAND == tail:
                    rows = pl.ds(t * N_HEADS + h, BAND, stride=dil * N_HEADS)
                    kp, vp = kt_ref[rows, :], vt_ref[rows, :]
                else:
                    kp, vp = kr_ref[:, t * N_HEADS + h, :], vr_ref[:, t * N_HEADS + h, :]
                steps = float(BAND + a) - col
                s_p = jnp.sum(kp * q_row, axis=1, keepdims=True) - (slope * dil) * steps
                if a > 0:
                    s_p = jnp.where(steps <= float(BAND), s_p, NEG)
                s_n = jnp.sum(knh * q_row, axis=1, keepdims=True)
                if dil == 1:
                    s_n = jnp.where(slot <= t, s_n - slope * (t - slot).astype(F32), NEG)
                else:
                    s_n = jnp.where(slot == t, s_n, NEG)
                m = jnp.maximum(jnp.max(s_p, axis=0, keepdims=True), jnp.max(s_n, axis=0, keepdims=True))
                p_p = jnp.exp(s_p - m)
                p_n = jnp.exp(s_n - m)
                l = jnp.sum(p_p, axis=0, keepdims=True) + jnp.sum(p_n, axis=0, keepdims=True)
                acc = (jnp.sum(p_p * vp, axis=0, keepdims=True)
                       + jnp.sum(p_n * vnh, axis=0, keepdims=True))
                parts.append((m, l, acc))
            m_all = functools.reduce(jnp.maximum, [m for m, _, _ in parts])
            l_all = sum(l * jnp.exp(m - m_all) for m, l, _ in parts)
            acc_all = sum(acc * jnp.exp(m - m_all) for m, _, acc in parts)
            o_ref[h, t:t + 1, :] = acc_all / l_all


def _sample_attention(q, k, v, cache_k, cache_v, n_new):
    n_batch, buf, nh, hd = cache_k.shape
    res16 = BRANCHES[-1][1]
    tail = 4 * BAND
    assert buf == res16 * BAND and n_new * nh <= 32
    rows2d = lambda c: c.reshape(n_batch, buf * nh, hd)
    resid = lambda c: c.reshape(n_batch, BAND, res16 * nh, hd)
    new_spec = pl.BlockSpec((nh, SLOTS, hd), lambda b: (0, b, 0))
    tail_spec = pl.BlockSpec((None, tail * nh, hd), lambda b: (b, buf // tail - 1, 0))
    res_spec = pl.BlockSpec((None, BAND, 32, hd), lambda b: (b, 0, 0, 0))
    return pl.pallas_call(
        functools.partial(_att_sample_kernel, n_new=n_new),
        grid=(n_batch,),
        in_specs=[new_spec, new_spec, new_spec, tail_spec, tail_spec, res_spec, res_spec],
        out_specs=pl.BlockSpec((None, nh, SLOTS, hd), lambda b: (b, 0, 0, 0)),
        out_shape=jax.ShapeDtypeStruct((n_batch, nh, SLOTS, hd), F32),
        compiler_params=_params(("parallel",)),
        name="attention_sample",
    )(q, k, v, rows2d(cache_k), rows2d(cache_v), resid(cache_k), resid(cache_v))


def _mix_kernel(att_ref, u_ref, vn_ref, x_ref, gt1_ref, sh2_ref, sc2_ref, g2_ref, wmix_ref, bmix_ref,
                wout_ref, x1_ref, h2_ref, h2t_ref, mix_scr):
    tm = x_ref.shape[0]
    for h in range(N_HEADS):
        mix_scr[:, h * HEAD_DIM:(h + 1) * HEAD_DIM] = att_ref[h].astype(BF16)
    for c in range(tm // BAND):
        rows = slice(c * BAND, (c + 1) * BAND)
        for g in range(N_HEADS):
            cs = slice(g * HEAD_DIM, (g + 1) * HEAD_DIM)
            m = jnp.dot(wmix_ref[g], vn_ref[rows, cs].astype(BF16),
                        preferred_element_type=F32) + bmix_ref[g]
            mix_scr[rows, ATT_WIDTH + g * HEAD_DIM:ATT_WIDTH + (g + 1) * HEAD_DIM] = (
                u_ref[rows, cs].astype(F32) * m).astype(BF16)
    y = jnp.dot(mix_scr[...], wout_ref[...], preferred_element_type=F32)
    x1 = x_ref[...] + gt1_ref[...] * y
    x1_ref[...] = x1
    h2 = _rms(x1, g2_ref[...]) * (1.0 + sc2_ref[...]) + sh2_ref[...]
    h2_ref[...] = h2.astype(BF16)
    h2t_ref[...] = h2.T.astype(BF16)


def _mix(att, u, vn, x, gt1, sh2, sc2, g2, wmix, bmix, w_out, tm):
    t, d = x.shape
    tpg = att.shape[2] // tm
    tok = lambda i: (i, 0)
    c2 = lambda i: (0, 0)
    c3 = lambda i: (0, 0, 0)
    return pl.pallas_call(
        _mix_kernel,
        grid=(t // tm,),
        in_specs=[pl.BlockSpec((None, N_HEADS, tm, HEAD_DIM), lambda i: (i // tpg, 0, i % tpg, 0)),
                  pl.BlockSpec((tm, ATT_WIDTH), tok), pl.BlockSpec((tm, ATT_WIDTH), tok),
                  pl.BlockSpec((tm, d), tok),
                  _mod_spec(gt1, tpg), _mod_spec(sh2, tpg), _mod_spec(sc2, tpg),
                  pl.BlockSpec((1, d), c2),
                  pl.BlockSpec(wmix.shape, c3), pl.BlockSpec(bmix.shape, c3),
                  pl.BlockSpec(w_out.shape, c2)],
        out_specs=[pl.BlockSpec((tm, d), tok), pl.BlockSpec((tm, d), tok),
                   pl.BlockSpec((d, tm), lambda i: (0, i))],
        out_shape=[jax.ShapeDtypeStruct((t, d), F32),
                   jax.ShapeDtypeStruct((t, d), BF16),
                   jax.ShapeDtypeStruct((d, t), BF16)],
        scratch_shapes=[pltpu.VMEM((tm, d), BF16)],
        compiler_params=_params(("parallel",)),
        name="mix_outproj",
    )(att, u, vn, x, gt1, sh2, sc2, g2, wmix, bmix, w_out)


def _cand_layout():
    rows = []
    rows += [(0, j) for j in range(16)]
    rows += [(1, j) for j in range(8)]
    rows += [(8 + r, 0) for r in range(8)]
    rows += [(2, j) if j < 5 else None for j in range(8)]
    rows += [(3, j) if j < 4 else None for j in range(8)]
    rows += [(4, j) if j < 3 else None for j in range(8)]
    rows += [(r, 0) if 5 <= r < 8 else None for r in range(8)]
    rows += [(r, 1) if 5 <= r < 8 else None for r in range(8)]
    return rows


N_CAND_ROWS = 72
BIG_FLAT = 1e9


def _cand_flat():
    flat = np.full((N_CAND_ROWS, LANES), BIG_FLAT, np.float32)
    for r, cell in enumerate(_cand_layout()):
        if cell is not None:
            flat[r, :] = 16 * cell[0] + cell[1]
    return jnp.asarray(flat)


def _top16(s, row_id, top_ref):
    rank = jnp.full(s.shape, float(TOPK), F32)
    for it in range(TOPK):
        m = jnp.max(s, axis=0, keepdims=True)
        idx = jnp.min(jnp.where(s == m, row_id, float(N_KEYS)), axis=0, keepdims=True)
        sel = row_id == idx
        rank = jnp.where(sel, float(it), rank)
        s = jnp.where(sel, -jnp.inf, s)
        top_ref[it:it + 1, :] = m
    return rank


def _select_kernel(h2_ref, wq_ref, keys_ref, flat_ref, cnt_ref, g1_ref, r2_ref, e2_ref,
                   qp_scr, top1_scr, top2_scr):
    tm = h2_ref.shape[0]
    qp_scr[...] = jnp.dot(h2_ref[...], wq_ref[...], preferred_element_type=F32).astype(BF16)
    row_id = lax.broadcasted_iota(jnp.int32, (N_KEYS, LANES), 0).astype(F32)
    row8 = lax.broadcasted_iota(jnp.int32, (8, LANES), 0)
    flat = flat_ref[...]
    valid = flat < BIG_FLAT
    for lc in range(tm // LANES):
        toks = slice(lc * LANES, (lc + 1) * LANES)

        def head_body(h, carry, toks=toks, lc=lc):
            sides = []
            for p, top_scr in ((0, top1_scr), (1, top2_scr)):
                col = pl.multiple_of((2 * h + p) * N_KEYS, N_KEYS)
                s = lax.dot_general(keys_ref[p], qp_scr[toks, pl.ds(col, N_KEYS)],
                                    (((1,), (1,)), ((), ())), preferred_element_type=F32)
                sides.append((s, _top16(s, row_id, top_scr)))
            (s1, rank1), (s2, rank2) = sides
            a1 = top1_scr[...]
            a2 = top2_scr[...]
            v = jnp.concatenate([
                a1[0:1] + a2[0:16],
                a1[1:2] + a2[0:8],
                a1[8:16] + a2[0:1],
                a1[2:3] + a2[0:8],
                a1[3:4] + a2[0:8],
                a1[4:5] + a2[0:8],
                a1[0:8] + a2[0:1],
                a1[0:8] + a2[1:2]], axis=0)
            vw = jnp.where(valid, v, -jnp.inf)
            selm = jnp.zeros(v.shape, F32)
            for _ in range(TOPK):
                m = jnp.max(vw, axis=0, keepdims=True)
                f = jnp.min(jnp.where(vw == m, flat, BIG_FLAT), axis=0, keepdims=True)
                sel = flat == f
                selm = jnp.where(sel, 1.0, selm)
                vw = jnp.where(sel, -jnp.inf, vw)
            vmax = a1[0:1] + a2[0:1]
            z = jnp.sum(jnp.where(selm > 0.0, jnp.exp(v - vmax), 0.0), axis=0, keepdims=True)
            cnt_lo = selm[56:64] + selm[64:72]
            for i, (lo, hi) in enumerate(((0, 16), (16, 24), (32, 40), (40, 48), (48, 56))):
                ci = jnp.sum(selm[lo:hi], axis=0, keepdims=True)
                cnt_lo = jnp.where(row8 == i, ci, cnt_lo)
            cnt_hi = selm[24:32]
            cnt_a = jnp.zeros((N_KEYS, LANES), F32)
            for i in range(TOPK):
                ci = cnt_lo[i:i + 1] if i < 8 else cnt_hi[i - 8:i - 7]
                cnt_a = jnp.where(rank1 == float(i), ci, cnt_a)
            cnt_ref[h, lc] = cnt_a
            g1_ref[h, lc] = jnp.exp(s1 - a1[0:1]) / z
            r2_ref[h, lc] = rank2
            e2_ref[h, lc] = jnp.exp(s2 - a2[0:1])
            return carry

        lax.fori_loop(0, N_HEADS, head_body, 0)


def _select(h2, w_query, sub_keys, tm):
    t, d = h2.shape
    out_spec = pl.BlockSpec((N_HEADS, tm // LANES, N_KEYS, LANES), lambda i: (0, i, 0, 0))
    out_shape = jax.ShapeDtypeStruct((N_HEADS, t // LANES, N_KEYS, LANES), F32)
    return pl.pallas_call(
        _select_kernel,
        grid=(t // tm,),
        in_specs=[pl.BlockSpec((tm, d), lambda i: (i, 0)),
                  pl.BlockSpec(w_query.shape, lambda i: (0, 0)),
                  pl.BlockSpec(sub_keys.shape, lambda i: (0, 0, 0)),
                  pl.BlockSpec((N_CAND_ROWS, LANES), lambda i: (0, 0))],
        out_specs=[out_spec] * 4,
        out_shape=[out_shape] * 4,
        scratch_shapes=[pltpu.VMEM((tm, w_query.shape[1]), BF16),
                        pltpu.VMEM((TOPK, LANES), F32), pltpu.VMEM((TOPK, LANES), F32)],
        compiler_params=_params(("parallel",)),
        name="peer_select",
    )(h2, w_query, sub_keys, _cand_flat())


def _coef_kernel(cnt_ref, g1_ref, r2_ref, e2_ref, c_ref):
    n_l, n_a = cnt_ref.shape[1], cnt_ref.shape[2]
    for lc in range(n_l):
        for ai in range(n_a):
            rows = slice(ai * N_KEYS, (ai + 1) * N_KEYS)
            coef = jnp.zeros((N_KEYS, LANES), F32)
            for h in range(N_HEADS):
                cnt = cnt_ref[h, lc, ai:ai + 1, :]
                g1 = g1_ref[h, lc, ai:ai + 1, :]
                coef = coef + jnp.where(r2_ref[h, lc] < cnt, e2_ref[h, lc], 0.0) * g1
            c_ref[lc, rows, :] = coef.astype(BF16)


def _coef(sel, te, n_exp):
    n_lc = sel[0].shape[1]
    n_l = min(n_lc, COEF_LANE_CHUNKS)
    a_spec = pl.BlockSpec((N_HEADS, n_l, te // N_KEYS, LANES), lambda i, j: (0, i, j, 0))
    b_spec = pl.BlockSpec((N_HEADS, n_l, N_KEYS, LANES), lambda i, j: (0, i, 0, 0))
    return pl.pallas_call(
        _coef_kernel,
        grid=(n_lc // n_l, n_exp // te),
        in_specs=[a_spec, a_spec, b_spec, b_spec],
        out_specs=pl.BlockSpec((n_l, te, LANES), lambda i, j: (i, j, 0)),
        out_shape=jax.ShapeDtypeStruct((n_lc, n_exp, LANES), BF16),
        compiler_params=_params(("parallel", "arbitrary")),
        name="peer_coef",
    )(*sel)


def _peer_kernel(h2t_ref, u_ref, vt_ref, c_ref, out_ref):
    j = pl.program_id(1)
    tt = h2t_ref.shape[1]

    @pl.when(j == 0)
    def _():
        out_ref[...] = jnp.zeros_like(out_ref)

    act = jnp.dot(u_ref[...], h2t_ref[...], preferred_element_type=F32)
    p = jnp.concatenate(
        [_gelu(act[:, lc * LANES:(lc + 1) * LANES]).astype(BF16) * c_ref[lc]
         for lc in range(tt // LANES)], axis=1)
    out_ref[...] += jnp.dot(vt_ref[...], p, preferred_element_type=F32)


def _peer(h2t, u, vt, sel, tt, te):
    d, t = h2t.shape
    n_exp = u.shape[0]
    coef = _coef(sel, te, n_exp)
    return pl.pallas_call(
        _peer_kernel,
        grid=(t // tt, n_exp // te),
        in_specs=[pl.BlockSpec((d, tt), lambda i, j: (0, i)),
                  pl.BlockSpec((te, d), lambda i, j: (j, 0)),
                  pl.BlockSpec((d, te), lambda i, j: (0, j)),
                  pl.BlockSpec((tt // LANES, te, LANES), lambda i, j: (i, j, 0))],
        out_specs=pl.BlockSpec((d, tt), lambda i, j: (0, i)),
        out_shape=jax.ShapeDtypeStruct((d, t), F32),
        compiler_params=_params(("parallel", "arbitrary")),
        name="peer_dense",
    )(h2t, u, vt, coef)


def _final_kernel(x1_ref, ffnt_ref, gt2_ref, gf_ref, y_ref):
    x2 = x1_ref[...] + gt2_ref[...] * ffnt_ref[...].T
    y_ref[...] = _rms(x2, gf_ref[...])


def _final(x1, ffnt, gt2, g_final, tm, tiles_per_group):
    t, d = x1.shape
    return pl.pallas_call(
        _final_kernel,
        grid=(t // tm,),
        in_specs=[pl.BlockSpec((tm, d), lambda i: (i, 0)),
                  pl.BlockSpec((d, tm), lambda i: (0, i)),
                  _mod_spec(gt2, tiles_per_group),
                  pl.BlockSpec((1, d), lambda i: (0, 0))],
        out_specs=pl.BlockSpec((tm, d), lambda i: (i, 0)),
        out_shape=jax.ShapeDtypeStruct((t, d), F32),
        compiler_params=_params(("parallel",)),
        name="final_norm",
    )(x1, ffnt, gt2, g_final)


def _trunk(x, mods, n_groups, window, att_fn, w, wmix, bmix, tm_in, tm_mix, tm_sel, tt, te):
    sh1, sc1, gt1, sh2, sc2, gt2 = mods
    rows = x.shape[0] // n_groups
    q, k, v, u, vn, *wins = _inproj(x, sh1, sc1, w["g1"], w["w_in"], w["lng"], w["lnb"], tm_in,
                                    n_groups, window)
    att = att_fn(q, k, v)
    x1, h2, h2t = _mix(att, u, vn, x, gt1, sh2, sc2, w["g2"], wmix, bmix, w["w_out"], tm_mix)
    sel = _select(h2, w["w_query"], w["sub_keys"], tm_sel)
    ffnt = _peer(h2t, w["u"], w["vt"], sel, tt, te)
    y = _final(x1, ffnt, gt2, w["g_final"], tm_mix, rows // tm_mix)
    return y, k, v, vn, wins


def kernel(x_prompt, x_sample, cache_k, cache_v, c_prompt, c_sample, w_ada, b_ada, g_norm1, w_in,
           ln_v_g, ln_v_b, w_spatial, b_spatial, w_out, g_norm2, w_query, sub_keys, expert_u,
           expert_v, g_final):
    depth = w_ada.shape[0]
    assert depth == 1, "single-layer trunk"
    batch, seq, d = x_prompt.shape
    n_dec, n_new, _ = x_sample.shape
    buf = cache_k.shape[2]
    assert seq % SPAN == 0 and buf == SPAN and n_new <= SLOTS
    win = min(BRANCHES[-1][0], seq)
    pad_s = BAND
    assert n_dec * SLOTS <= pad_s

    n_c = batch + n_dec
    c_all = jnp.pad(jnp.concatenate([c_prompt, c_sample], axis=0), ((0, -n_c % SUBLANES), (0, 0)))
    mod = _adaln(c_all, w_ada[0], b_ada[0][None, :])
    mods = [mod[:, i * d:(i + 1) * d] for i in range(6)]
    mods_p = [m[:batch, None, :] for m in mods]
    mods_s = [jnp.pad(jnp.repeat(m[batch:n_c], SLOTS, axis=0), ((0, pad_s - n_dec * SLOTS), (0, 0)))[None]
              for m in mods]

    tri = jnp.tril(jnp.ones((BAND, BAND), dtype=bool))
    ws = jnp.where(tri, w_spatial[0], 0.0)
    bs = b_spatial[0]
    wmix_p = ws.astype(BF16)
    bmix_p = jnp.broadcast_to(bs[:, :, None], (N_HEADS, BAND, HEAD_DIM))
    eye = jnp.eye(pad_s // SLOTS, dtype=F32)
    slot_pad = ((0, 0), (0, SLOTS - n_new))
    ws_new = jnp.pad(ws[:, :n_new, :n_new], ((0, 0),) + slot_pad[1:] + slot_pad[1:])
    wmix_s = jax.vmap(lambda m: jnp.kron(eye, m))(ws_new).astype(BF16)
    bmix_s = jnp.broadcast_to(jnp.tile(jnp.pad(bs[:, :n_new], slot_pad), (1, pad_s // SLOTS))[:, :, None],
                              (N_HEADS, pad_s, HEAD_DIM))

    w = dict(g1=g_norm1[0][None, :], w_in=w_in[0].astype(BF16), lng=ln_v_g[0][None, :],
             lnb=ln_v_b[0][None, :], g2=g_norm2[0][None, :], w_out=w_out[0].astype(BF16),
             w_query=w_query[0].astype(BF16), sub_keys=sub_keys[0].astype(BF16),
             u=expert_u[0].astype(BF16), vt=expert_v[0].astype(BF16).T, g_final=g_final[None, :])

    xp = x_prompt.reshape(batch * seq, d)
    y_p, _, _, _, (k_win, v_win) = _trunk(
        xp, mods_p, batch, win, _prompt_attention, w, wmix_p, bmix_p,
        tm_in=512, tm_mix=256, tm_sel=256, tt=512, te=1024)

    xs = jnp.pad(x_sample, ((0, 0), (0, SLOTS - n_new), (0, 0))).reshape(n_dec * SLOTS, d)
    xs = jnp.pad(xs, ((0, pad_s - n_dec * SLOTS), (0, 0)))

    def sample_att(q, k, v):
        o = _sample_attention(q[0], k[0], v[0], cache_k[0], cache_v[0], n_new)
        o = o.transpose(1, 0, 2, 3).reshape(N_HEADS, n_dec * SLOTS, HEAD_DIM)
        return jnp.pad(o, ((0, 0), (0, pad_s - n_dec * SLOTS), (0, 0)))[None]

    y_s, k_s, v_s, vn_s, _ = _trunk(
        xs, mods_s, 1, 0, sample_att, w, wmix_s, bmix_s,
        tm_in=pad_s, tm_mix=pad_s, tm_sel=pad_s, tt=pad_s, te=1024)

    new_rows = lambda t: t[:n_dec * SLOTS].reshape(n_dec, SLOTS, -1)[:, :n_new]
    new_heads = lambda t: t[0, :, :n_dec * SLOTS].reshape(N_HEADS, n_dec, SLOTS, HEAD_DIM)[:, :, :n_new
                                                                                         ].transpose(1, 2, 0, 3)
    return (y_p.reshape(batch, seq, d),
            new_rows(y_s),
            k_win.reshape(1, batch, win, N_HEADS, HEAD_DIM),
            v_win.reshape(1, batch, win, N_HEADS, HEAD_DIM),
            new_heads(k_s)[None],
            new_heads(v_s)[None],
            new_rows(vn_s)[None])
```

```python
import functools

import numpy as np
import jax
import jax.numpy as jnp
from jax import lax
from jax.experimental import pallas as pl
from jax.experimental.pallas import tpu as pltpu

F32 = jnp.float32
BF16 = jnp.bfloat16

EPS = 1e-6
LANES = 128
SUBLANES = 8
HEAD_DIM = 128
N_HEADS = 8
ATT_WIDTH = N_HEADS * HEAD_DIM
BRANCHES = ((128, 1), (512, 4), (2048, 16))
BAND = 128
SPAN = BAND * BRANCHES[-1][1]
N_KEYS = 128
TOPK = 16
NEG = -1e30
VMEM_LIMIT = 56 * 1024 * 1024
COEF_LANE_CHUNKS = 4
SLOTS = SUBLANES


def _params(sem, flags=None):
    return pltpu.CompilerParams(dimension_semantics=sem, vmem_limit_bytes=VMEM_LIMIT, flags=flags)


def _gelu(x):
    return 0.5 * x * (1.0 + lax.erf(x * (2.0 ** -0.5)))


def _rms(x, g):
    return x * lax.rsqrt(jnp.mean(x * x, axis=-1, keepdims=True) + EPS) * g


def _slope(h):
    return 2.0 ** (-8.0 * (h + 1) / N_HEADS)


def _adaln_kernel(c_ref, w_ref, b_ref, o_ref):
    c = c_ref[...]
    a = c / (1.0 + jnp.exp(-c))
    o_ref[...] = jnp.dot(a.astype(BF16), w_ref[...].astype(BF16),
                         preferred_element_type=F32) + b_ref[...]


def _adaln(c_all, w_ada, b_ada):
    rows, d = c_all.shape
    n = w_ada.shape[1]
    tn = 1024
    return pl.pallas_call(
        _adaln_kernel,
        grid=(n // tn,),
        in_specs=[pl.BlockSpec((rows, d), lambda j: (0, 0)),
                  pl.BlockSpec((d, tn), lambda j: (0, j)),
                  pl.BlockSpec((1, tn), lambda j: (0, j))],
        out_specs=pl.BlockSpec((rows, tn), lambda j: (0, j)),
        out_shape=jax.ShapeDtypeStruct((rows, n), F32),
        compiler_params=_params(("arbitrary",)),
        name="adaln",
    )(c_all, w_ada, b_ada)


def _inproj_kernel(x_ref, sh_ref, sc_ref, g1_ref, w_ref, lng_ref, lnb_ref,
                   q_ref, k_ref, v_ref, u_ref, vn_ref, *rest, windows):
    if windows:
        kwin_ref, vwin_ref, h_scr = rest
    else:
        (h_scr,) = rest
    j = pl.program_id(1)

    @pl.when(j == 0)
    def _():
        y = _rms(x_ref[...], g1_ref[...])
        h_scr[...] = (y * (1.0 + sc_ref[...]) + sh_ref[...]).astype(BF16)

    p = jnp.dot(h_scr[...], w_ref[...], preferred_element_type=F32)

    def heads_out(ref, val, win_ref=None):
        for h in range(N_HEADS):
            piece = val[:, h * HEAD_DIM:(h + 1) * HEAD_DIM]
            ref[h] = piece
            if win_ref is not None:
                win_ref[:, h, :] = piece

    @pl.when(j == 0)
    def _():
        heads_out(q_ref, p * (HEAD_DIM ** -0.5))

    @pl.when(j == 1)
    def _():
        heads_out(k_ref, p, kwin_ref if windows else None)

    @pl.when(j == 2)
    def _():
        heads_out(v_ref, p, vwin_ref if windows else None)

    @pl.when(j == 3)
    def _():
        u_ref[...] = _gelu(p).astype(BF16)

    @pl.when(j == 4)
    def _():
        g = _gelu(p)
        mu = jnp.mean(g, axis=-1, keepdims=True)
        gc = g - mu
        var = jnp.mean(gc * gc, axis=-1, keepdims=True)
        vn_ref[...] = gc * lax.rsqrt(var + EPS) * lng_ref[...] + lnb_ref[...]


def _mod_spec(mod, tiles_per_group):
    _, r, d = mod.shape
    return pl.BlockSpec((None, r, d), lambda i, *_: (i // tiles_per_group, 0, 0))


def _inproj(x, sh1, sc1, g1, w_in, lng, lnb, tm, n_groups, window):
    t, d = x.shape
    sec = ATT_WIDTH
    n_sec = w_in.shape[1] // sec
    rows = t // n_groups
    tpg = rows // tm
    tok = lambda i, j: (i, 0)
    const = lambda i, j: (0, 0)
    out_spec = pl.BlockSpec((tm, sec), tok)
    head_spec = pl.BlockSpec((None, N_HEADS, tm, HEAD_DIM), lambda i, j: (i // tpg, 0, i % tpg, 0))
    head_shape = jax.ShapeDtypeStruct((n_groups, N_HEADS, rows, HEAD_DIM), F32)
    out_specs = [head_spec] * 3 + [out_spec] * 2
    out_shape = [head_shape] * 3 + [jax.ShapeDtypeStruct((t, sec), BF16),
                                    jax.ShapeDtypeStruct((t, sec), F32)]
    if window:
        wt = window // tm
        win_spec = pl.BlockSpec((tm, N_HEADS, HEAD_DIM), lambda i, j: (
            (i // tpg) * wt + jnp.maximum(i % tpg - (tpg - wt), 0), 0, 0))
        win_shape = jax.ShapeDtypeStruct((n_groups * window, N_HEADS, HEAD_DIM), F32)
        out_specs += [win_spec] * 2
        out_shape += [win_shape] * 2
    return pl.pallas_call(
        functools.partial(_inproj_kernel, windows=bool(window)),
        grid=(t // tm, n_sec),
        in_specs=[pl.BlockSpec((tm, d), tok),
                  _mod_spec(sh1, tpg), _mod_spec(sc1, tpg),
                  pl.BlockSpec((1, d), const),
                  pl.BlockSpec((d, sec), lambda i, j: (0, j)),
                  pl.BlockSpec((1, sec), const), pl.BlockSpec((1, sec), const)],
        out_specs=out_specs,
        out_shape=out_shape,
        scratch_shapes=[pltpu.VMEM((tm, d), BF16)],
        compiler_params=_params(("arbitrary", "arbitrary")),
        name="inproj",
    )(x, sh1, sc1, g1, w_in, lng, lnb)


def _att_bias():
    a = np.arange(BAND)[:, None]
    kk = np.arange(2 * BAND)[None, :]
    steps = BAND + a - kk
    valid = (steps >= 0) & (steps <= BAND)
    slopes = np.array([_slope(h) for h in range(N_HEADS)])
    out = []
    for _, dil in BRANCHES:
        bias = -slopes[:, None, None] * (dil * steps).astype(np.float64)[None]
        out.append(np.where(valid[None], bias, NEG))
    return jnp.asarray(np.stack(out), dtype=F32)


def _att_prompt_kernel(q_ref, kc_ref, kp_ref, vc_ref, vp_ref, bias_ref, o_ref, m_scr, l_scr, acc_scr):
    first_span = pl.program_id(2) == 0
    span = q_ref.shape[0]
    is_prev = lax.broadcasted_iota(jnp.int32, (BAND, 2 * BAND), 1) < BAND

    def rows_at(start, dil):
        return pl.ds(start, BAND) if dil == 1 else pl.ds(start, BAND, stride=dil)

    for bi, (_, dil) in enumerate(BRANCHES):
        for r in range(dil):
            for n in range(span // (BAND * dil)):
                start = r + dil * BAND * n
                rows = rows_at(start, dil)
                if n > 0:
                    prev = rows_at(start - dil * BAND, dil)
                    kp, vp = kc_ref[prev, :], vc_ref[prev, :]
                else:
                    prev = rows_at(span - dil * BAND + r, dil)
                    kp, vp = kp_ref[prev, :], vp_ref[prev, :]
                kw = jnp.concatenate([kp, kc_ref[rows, :]], axis=0).astype(BF16)
                vw = jnp.concatenate([vp, vc_ref[rows, :]], axis=0).astype(BF16)
                s = lax.dot_general(q_ref[rows, :].astype(BF16), kw, (((1,), (1,)), ((), ())),
                                    preferred_element_type=F32) + bias_ref[bi]
                if n == 0:
                    s = jnp.where(jnp.logical_and(first_span, is_prev), NEG, s)
                m_b = jnp.max(s, axis=-1, keepdims=True)
                p = jnp.exp(s - m_b)
                l_b = jnp.sum(p, axis=-1, keepdims=True)
                pv = jnp.dot(p.astype(BF16), vw, preferred_element_type=F32)
                if bi == 0:
                    m_scr[rows, :] = jnp.broadcast_to(m_b, (BAND, LANES))
                    l_scr[rows, :] = jnp.broadcast_to(l_b, (BAND, LANES))
                    acc_scr[rows, :] = pv
                else:
                    m_o = m_scr[rows, :]
                    m_n = jnp.maximum(m_o, m_b)
                    e_o = jnp.exp(m_o - m_n)
                    e_b = jnp.exp(m_b - m_n)
                    m_scr[rows, :] = m_n
                    l_scr[rows, :] = e_o * l_scr[rows, :] + e_b * l_b
                    acc_scr[rows, :] = e_o * acc_scr[rows, :] + e_b * pv
    o_ref[...] = (acc_scr[...] / l_scr[...]).astype(o_ref.dtype)


def _prompt_attention(q, k, v):
    b, nh, seq, hd = q.shape
    cur = lambda bb, h, s: (bb, h, s, 0)
    prev = lambda bb, h, s: (bb, h, jnp.maximum(s - 1, 0), 0)
    blk = lambda m: pl.BlockSpec((None, None, SPAN, hd), m)
    return pl.pallas_call(
        _att_prompt_kernel,
        grid=(b, nh, seq // SPAN),
        in_specs=[blk(cur), blk(cur), blk(prev), blk(cur), blk(prev),
                  pl.BlockSpec((len(BRANCHES), None, BAND, 2 * BAND), lambda bb, h, s: (0, h, 0, 0))],
        out_specs=blk(cur),
        out_shape=jax.ShapeDtypeStruct(q.shape, BF16),
        scratch_shapes=[pltpu.VMEM((SPAN, LANES), F32)] * 3,
        compiler_params=_params(("parallel", "parallel", "arbitrary")),
        name="attention_prompt",
    )(q, k, k, v, v, _att_bias())


def _att_sample_kernel(q_ref, kn_ref, vn_ref, kt_ref, vt_ref, kr_ref, vr_ref, o_ref, *, n_new):
    o_ref[...] = jnp.zeros_like(o_ref)
    col = lax.broadcasted_iota(jnp.int32, (BAND, 1), 0).astype(F32)
    slot = lax.broadcasted_iota(jnp.int32, (SLOTS, 1), 0)
    tail = kt_ref.shape[0] // N_HEADS
    for h in range(N_HEADS):
        slope = _slope(h)
        knh, vnh = kn_ref[h], vn_ref[h]
        for t in range(n_new):
            q_row = q_ref[h, t:t + 1, :]
            parts = []
            for _, dil in BRANCHES:
                a = t if dil == 1 else 0
                if dil == 1:
                    rows = pl.ds((tail - BAND) * N_HEADS + h, BAND, stride=N_HEADS)
                    kp, vp = kt_ref[rows, :], vt_ref[rows, :]
                elif dil * BAND == tail:
                    rows = pl.ds(t * N_HEADS + h, BAND, stride=dil * N_HEADS)
                    kp, vp = kt_ref[rows, :], vt_ref[rows, :]
                else:
                    kp, vp = kr_ref[:, t * N_HEADS + h, :], vr_ref[:, t * N_HEADS + h, :]
                steps = float(BAND + a) - col
                s_p = jnp.sum(kp * q_row, axis=1, keepdims=True) - (slope * dil) * steps
                if a > 0:
                    s_p = jnp.where(steps <= float(BAND), s_p, NEG)
                s_n = jnp.sum(knh * q_row, axis=1, keepdims=True)
                if dil == 1:
                    s_n = jnp.where(slot <= t, s_n - slope * (t - slot).astype(F32), NEG)
                else:
                    s_n = jnp.where(slot == t, s_n, NEG)
                m = jnp.maximum(jnp.max(s_p, axis=0, keepdims=True), jnp.max(s_n, axis=0, keepdims=True))
                p_p = jnp.exp(s_p - m)
                p_n = jnp.exp(s_n - m)
                l = jnp.sum(p_p, axis=0, keepdims=True) + jnp.sum(p_n, axis=0, keepdims=True)
                acc = (jnp.sum(p_p * vp, axis=0, keepdims=True)
                       + jnp.sum(p_n * vnh, axis=0, keepdims=True))
                parts.append((m, l, acc))
            m_all = functools.reduce(jnp.maximum, [m for m, _, _ in parts])
            l_all = sum(l * jnp.exp(m - m_all) for m, l, _ in parts)
            acc_all = sum(acc * jnp.exp(m - m_all) for m, _, acc in parts)
            o_ref[h, t:t + 1, :] = acc_all / l_all


def _sample_attention(q, k, v, cache_k, cache_v, n_new):
    n_batch, buf, nh, hd = cache_k.shape
    res16 = BRANCHES[-1][1]
    tail = 4 * BAND
    assert buf == res16 * BAND and n_new * nh <= 32
    rows2d = lambda c: c.reshape(n_batch, buf * nh, hd)
    resid = lambda c: c.reshape(n_batch, BAND, res16 * nh, hd)
    new_spec = pl.BlockSpec((nh, SLOTS, hd), lambda b: (0, b, 0))
    tail_spec = pl.BlockSpec((None, tail * nh, hd), lambda b: (b, buf // tail - 1, 0))
    res_spec = pl.BlockSpec((None, BAND, 32, hd), lambda b: (b, 0, 0, 0))
    return pl.pallas_call(
        functools.partial(_att_sample_kernel, n_new=n_new),
        grid=(n_batch,),
        in_specs=[new_spec, new_spec, new_spec, tail_spec, tail_spec, res_spec, res_spec],
        out_specs=pl.BlockSpec((None, nh, SLOTS, hd), lambda b: (b, 0, 0, 0)),
        out_shape=jax.ShapeDtypeStruct((n_batch, nh, SLOTS, hd), F32),
        compiler_params=_params(("parallel",)),
        name="attention_sample",
    )(q, k, v, rows2d(cache_k), rows2d(cache_v), resid(cache_k), resid(cache_v))


def _mix_kernel(att_ref, u_ref, vn_ref, x_ref, gt1_ref, sh2_ref, sc2_ref, g2_ref, wmix_ref, bmix_ref,
                wout_ref, x1_ref, h2_ref, h2t_ref, mix_scr):
    tm = x_ref.shape[0]
    for h in range(N_HEADS):
        mix_scr[:, h * HEAD_DIM:(h + 1) * HEAD_DIM] = att_ref[h].astype(BF16)
    for c in range(tm // BAND):
        rows = slice(c * BAND, (c + 1) * BAND)
        for g in range(N_HEADS):
            cs = slice(g * HEAD_DIM, (g + 1) * HEAD_DIM)
            m = jnp.dot(wmix_ref[g], vn_ref[rows, cs].astype(BF16),
                        preferred_element_type=F32) + bmix_ref[g]
            mix_scr[rows, ATT_WIDTH + g * HEAD_DIM:ATT_WIDTH + (g + 1) * HEAD_DIM] = (
                u_ref[rows, cs].astype(F32) * m).astype(BF16)
    y = jnp.dot(mix_scr[...], wout_ref[...], preferred_element_type=F32)
    x1 = x_ref[...] + gt1_ref[...] * y
    x1_ref[...] = x1
    h2 = _rms(x1, g2_ref[...]) * (1.0 + sc2_ref[...]) + sh2_ref[...]
    h2_ref[...] = h2.astype(BF16)
    h2t_ref[...] = h2.T.astype(BF16)


def _mix(att, u, vn, x, gt1, sh2, sc2, g2, wmix, bmix, w_out, tm):
    t, d = x.shape
    tpg = att.shape[2] // tm
    tok = lambda i: (i, 0)
    c2 = lambda i: (0, 0)
    c3 = lambda i: (0, 0, 0)
    return pl.pallas_call(
        _mix_kernel,
        grid=(t // tm,),
        in_specs=[pl.BlockSpec((None, N_HEADS, tm, HEAD_DIM), lambda i: (i // tpg, 0, i % tpg, 0)),
                  pl.BlockSpec((tm, ATT_WIDTH), tok), pl.BlockSpec((tm, ATT_WIDTH), tok),
                  pl.BlockSpec((tm, d), tok),
                  _mod_spec(gt1, tpg), _mod_spec(sh2, tpg), _mod_spec(sc2, tpg),
                  pl.BlockSpec((1, d), c2),
                  pl.BlockSpec(wmix.shape, c3), pl.BlockSpec(bmix.shape, c3),
                  pl.BlockSpec(w_out.shape, c2)],
        out_specs=[pl.BlockSpec((tm, d), tok), pl.BlockSpec((tm, d), tok),
                   pl.BlockSpec((d, tm), lambda i: (0, i))],
        out_shape=[jax.ShapeDtypeStruct((t, d), F32),
                   jax.ShapeDtypeStruct((t, d), BF16),
                   jax.ShapeDtypeStruct((d, t), BF16)],
        scratch_shapes=[pltpu.VMEM((tm, d), BF16)],
        compiler_params=_params(("parallel",)),
        name="mix_outproj",
    )(att, u, vn, x, gt1, sh2, sc2, g2, wmix, bmix, w_out)


def _cand_layout():
    rows = []
    rows += [(0, j) for j in range(16)]
    rows += [(1, j) for j in range(8)]
    rows += [(8 + r, 0) for r in range(8)]
    rows += [(2, j) if j < 5 else None for j in range(8)]
    rows += [(3, j) if j < 4 else None for j in range(8)]
    rows += [(4, j) if j < 3 else None for j in range(8)]
    rows += [(r, 0) if 5 <= r < 8 else None for r in range(8)]
    rows += [(r, 1) if 5 <= r < 8 else None for r in range(8)]
    return rows


N_CAND_ROWS = 72
BIG_FLAT = 1e9


def _cand_flat():
    flat = np.full((N_CAND_ROWS, LANES), BIG_FLAT, np.float32)
    for r, cell in enumerate(_cand_layout()):
        if cell is not None:
            flat[r, :] = 16 * cell[0] + cell[1]
    return jnp.asarray(flat)


def _top16(s, row_id, top_ref, exact):
    rank = jnp.full(s.shape, float(TOPK), F32)
    for it in range(TOPK):
        m = jnp.max(s, axis=0, keepdims=True)
        sel = s == m
        if exact:
            idx = jnp.min(jnp.where(sel, row_id, float(N_KEYS)), axis=0, keepdims=True)
            sel = row_id == idx
        rank = jnp.where(sel, float(it), rank)
        s = jnp.where(sel, -jnp.inf, s)
        top_ref[it:it + 1, :] = m
    return rank


def _select_block(h, lc, toks, exact, keys_ref, qp_scr, top1_scr, top2_scr, flat, row_id, row8,
                  cnt_ref, g1_ref, r2_ref, e2_ref):
    valid = flat < BIG_FLAT
    sides = []
    for p, top_scr in ((0, top1_scr), (1, top2_scr)):
        col = pl.multiple_of((2 * h + p) * N_KEYS, N_KEYS)
        s = lax.dot_general(keys_ref[p], qp_scr[toks, pl.ds(col, N_KEYS)],
                            (((1,), (1,)), ((), ())), preferred_element_type=F32)
        sides.append((s, _top16(s, row_id, top_scr, exact)))
    (s1, rank1), (s2, rank2) = sides
    a1 = top1_scr[...]
    a2 = top2_scr[...]
    v = jnp.concatenate([
        a1[0:1] + a2[0:16],
        a1[1:2] + a2[0:8],
        a1[8:16] + a2[0:1],
        a1[2:3] + a2[0:8],
        a1[3:4] + a2[0:8],
        a1[4:5] + a2[0:8],
        a1[0:8] + a2[0:1],
        a1[0:8] + a2[1:2]], axis=0)
    vw = jnp.where(valid, v, -jnp.inf)
    selm = jnp.zeros(v.shape, F32)
    for _ in range(TOPK):
        m = jnp.max(vw, axis=0, keepdims=True)
        sel = vw == m
        if exact:
            f = jnp.min(jnp.where(sel, flat, BIG_FLAT), axis=0, keepdims=True)
            sel = flat == f
        selm = jnp.where(sel, 1.0, selm)
        vw = jnp.where(sel, -jnp.inf, vw)
    vmax = a1[0:1] + a2[0:1]
    z = jnp.sum(jnp.where(selm > 0.0, jnp.exp(v - vmax), 0.0), axis=0, keepdims=True)
    cnt_lo = selm[56:64] + selm[64:72]
    for i, (lo, hi) in enumerate(((0, 16), (16, 24), (32, 40), (40, 48), (48, 56))):
        ci = jnp.sum(selm[lo:hi], axis=0, keepdims=True)
        cnt_lo = jnp.where(row8 == i, ci, cnt_lo)
    cnt_hi = selm[24:32]
    cnt_a = jnp.zeros((N_KEYS, LANES), F32)
    for i in range(TOPK):
        ci = cnt_lo[i:i + 1] if i < 8 else cnt_hi[i - 8:i - 7]
        cnt_a = jnp.where(rank1 == float(i), ci, cnt_a)
    cnt_ref[h, lc] = cnt_a
    g1_ref[h, lc] = jnp.exp(s1 - a1[0:1]) / z
    r2_ref[h, lc] = rank2.astype(r2_ref.dtype)
    e2_ref[h, lc] = jnp.exp(s2 - a2[0:1]).astype(e2_ref.dtype)
    if exact:
        return None
    n1 = jnp.sum(jnp.where(rank1 < float(TOPK), 1.0, 0.0), axis=0, keepdims=True)
    n2 = jnp.sum(jnp.where(rank2 < float(TOPK), 1.0, 0.0), axis=0, keepdims=True)
    n3 = jnp.sum(selm, axis=0, keepdims=True)
    return jnp.max(jnp.maximum(jnp.maximum(n1, n2), n3))


def _select_kernel(h2_ref, wq_ref, keys_ref, flat_ref, cnt_ref, g1_ref, r2_ref, e2_ref,
                   qp_scr, top1_scr, top2_scr):
    tm = h2_ref.shape[0]
    qp_scr[...] = jnp.dot(h2_ref[...], wq_ref[...], preferred_element_type=F32).astype(BF16)
    row_id = lax.broadcasted_iota(jnp.int32, (N_KEYS, LANES), 0).astype(F32)
    row8 = lax.broadcasted_iota(jnp.int32, (8, LANES), 0)
    flat = flat_ref[...]
    for lc in range(tm // LANES):
        toks = slice(lc * LANES, (lc + 1) * LANES)
        block = functools.partial(
            _select_block, lc=lc, toks=toks, keys_ref=keys_ref, qp_scr=qp_scr, top1_scr=top1_scr,
            top2_scr=top2_scr, flat=flat, row_id=row_id, row8=row8, cnt_ref=cnt_ref, g1_ref=g1_ref,
            r2_ref=r2_ref, e2_ref=e2_ref)

        def head_body(h, carry, block=block):
            most = block(h, exact=False)

            @pl.when(most > float(TOPK) + 0.5)
            def _():
                block(h, exact=True)

            return carry

        lax.fori_loop(0, N_HEADS, head_body, 0)


def _select(h2, w_query, sub_keys, tm):
    t, d = h2.shape
    out_spec = pl.BlockSpec((N_HEADS, tm // LANES, N_KEYS, LANES), lambda i: (0, i, 0, 0))
    slab = lambda dt: jax.ShapeDtypeStruct((N_HEADS, t // LANES, N_KEYS, LANES), dt)
    return pl.pallas_call(
        _select_kernel,
        grid=(t // tm,),
        in_specs=[pl.BlockSpec((tm, d), lambda i: (i, 0)),
                  pl.BlockSpec(w_query.shape, lambda i: (0, 0)),
                  pl.BlockSpec(sub_keys.shape, lambda i: (0, 0, 0)),
                  pl.BlockSpec((N_CAND_ROWS, LANES), lambda i: (0, 0))],
        out_specs=[out_spec] * 4,
        out_shape=[slab(F32)] * 4,
        scratch_shapes=[pltpu.VMEM((tm, w_query.shape[1]), BF16),
                        pltpu.VMEM((TOPK, LANES), F32), pltpu.VMEM((TOPK, LANES), F32)],
        compiler_params=_params(("parallel",)),
        name="peer_select",
    )(h2, w_query, sub_keys, _cand_flat())


def _coef_kernel(cnt_ref, g1_ref, r2_ref, e2_ref, c_ref):
    n_l, n_a = cnt_ref.shape[1], cnt_ref.shape[2]
    for lc in range(n_l):
        for ai in range(n_a):
            rows = slice(ai * N_KEYS, (ai + 1) * N_KEYS)
            coef = jnp.zeros((N_KEYS, LANES), F32)
            for h in range(N_HEADS):
                cnt = cnt_ref[h, lc, ai:ai + 1, :]
                g1 = g1_ref[h, lc, ai:ai + 1, :]
                coef = coef + jnp.where(r2_ref[h, lc] < cnt, e2_ref[h, lc], 0.0) * g1
            c_ref[lc, rows, :] = coef.astype(BF16)


def _coef(sel, te, n_exp):
    n_lc = sel[0].shape[1]
    n_l = min(n_lc, COEF_LANE_CHUNKS)
    a_spec = pl.BlockSpec((N_HEADS, n_l, te // N_KEYS, LANES), lambda i, j: (0, i, j, 0))
    b_spec = pl.BlockSpec((N_HEADS, n_l, N_KEYS, LANES), lambda i, j: (0, i, 0, 0))
    return pl.pallas_call(
        _coef_kernel,
        grid=(n_lc // n_l, n_exp // te),
        in_specs=[a_spec, a_spec, b_spec, b_spec],
        out_specs=pl.BlockSpec((n_l, te, LANES), lambda i, j: (i, j, 0)),
        out_shape=jax.ShapeDtypeStruct((n_lc, n_exp, LANES), BF16),
        compiler_params=_params(("parallel", "arbitrary")),
        name="peer_coef",
    )(*sel)


def _peer_kernel(h2t_ref, u_ref, vt_ref, c_ref, out_ref):
    j = pl.program_id(1)
    tt = h2t_ref.shape[1]

    @pl.when(j == 0)
    def _():
        out_ref[...] = jnp.zeros_like(out_ref)

    act = jnp.dot(u_ref[...], h2t_ref[...], preferred_element_type=F32)
    p = jnp.concatenate(
        [_gelu(act[:, lc * LANES:(lc + 1) * LANES]).astype(BF16) * c_ref[lc]
         for lc in range(tt // LANES)], axis=1)
    out_ref[...] += jnp.dot(vt_ref[...], p, preferred_element_type=F32)


def _peer(h2t, u, vt, sel, tt, te):
    d, t = h2t.shape
    n_exp = u.shape[0]
    coef = _coef(sel, te, n_exp)
    return pl.pallas_call(
        _peer_kernel,
        grid=(t // tt, n_exp // te),
        in_specs=[pl.BlockSpec((d, tt), lambda i, j: (0, i)),
                  pl.BlockSpec((te, d), lambda i, j: (j, 0)),
                  pl.BlockSpec((d, te), lambda i, j: (0, j)),
                  pl.BlockSpec((tt // LANES, te, LANES), lambda i, j: (i, j, 0))],
        out_specs=pl.BlockSpec((d, tt), lambda i, j: (0, i)),
        out_shape=jax.ShapeDtypeStruct((d, t), F32),
        compiler_params=_params(("parallel", "arbitrary")),
        name="peer_dense",
    )(h2t, u, vt, coef)


def _final_kernel(x1_ref, ffnt_ref, gt2_ref, gf_ref, y_ref):
    x2 = x1_ref[...] + gt2_ref[...] * ffnt_ref[...].T
    y_ref[...] = _rms(x2, gf_ref[...])


def _final(x1, ffnt, gt2, g_final, tm, tiles_per_group):
    t, d = x1.shape
    return pl.pallas_call(
        _final_kernel,
        grid=(t // tm,),
        in_specs=[pl.BlockSpec((tm, d), lambda i: (i, 0)),
                  pl.BlockSpec((d, tm), lambda i: (0, i)),
                  _mod_spec(gt2, tiles_per_group),
                  pl.BlockSpec((1, d), lambda i: (0, 0))],
        out_specs=pl.BlockSpec((tm, d), lambda i: (i, 0)),
        out_shape=jax.ShapeDtypeStruct((t, d), F32),
        compiler_params=_params(("parallel",)),
        name="final_norm",
    )(x1, ffnt, gt2, g_final)


def _trunk(x, mods, n_groups, window, att_fn, w, wmix, bmix, tm_in, tm_mix, tm_sel, tt, te):
    sh1, sc1, gt1, sh2, sc2, gt2 = mods
    rows = x.shape[0] // n_groups
    q, k, v, u, vn, *wins = _inproj(x, sh1, sc1, w["g1"], w["w_in"], w["lng"], w["lnb"], tm_in,
                                    n_groups, window)
    att = att_fn(q, k, v)
    x1, h2, h2t = _mix(att, u, vn, x, gt1, sh2, sc2, w["g2"], wmix, bmix, w["w_out"], tm_mix)
    sel = _select(h2, w["w_query"], w["sub_keys"], tm_sel)
    ffnt = _peer(h2t, w["u"], w["vt"], sel, tt, te)
    y = _final(x1, ffnt, gt2, w["g_final"], tm_mix, rows // tm_mix)
    return y, k, v, vn, wins


def kernel(x_prompt, x_sample, cache_k, cache_v, c_prompt, c_sample, w_ada, b_ada, g_norm1, w_in,
           ln_v_g, ln_v_b, w_spatial, b_spatial, w_out, g_norm2, w_query, sub_keys, expert_u,
           expert_v, g_final):
    depth = w_ada.shape[0]
    assert depth == 1, "single-layer trunk"
    batch, seq, d = x_prompt.shape
    n_dec, n_new, _ = x_sample.shape
    buf = cache_k.shape[2]
    assert seq % SPAN == 0 and buf == SPAN and n_new <= SLOTS
    win = min(BRANCHES[-1][0], seq)
    pad_s = BAND
    assert n_dec * SLOTS <= pad_s

    n_c = batch + n_dec
    c_all = jnp.pad(jnp.concatenate([c_prompt, c_sample], axis=0), ((0, -n_c % SUBLANES), (0, 0)))
    mod = _adaln(c_all, w_ada[0], b_ada[0][None, :])
    mods = [mod[:, i * d:(i + 1) * d] for i in range(6)]
    mods_p = [m[:batch, None, :] for m in mods]
    mods_s = [jnp.pad(jnp.repeat(m[batch:n_c], SLOTS, axis=0), ((0, pad_s - n_dec * SLOTS), (0, 0)))[None]
              for m in mods]

    tri = jnp.tril(jnp.ones((BAND, BAND), dtype=bool))
    ws = jnp.where(tri, w_spatial[0], 0.0)
    bs = b_spatial[0]
    wmix_p = ws.astype(BF16)
    bmix_p = jnp.broadcast_to(bs[:, :, None], (N_HEADS, BAND, HEAD_DIM))
    eye = jnp.eye(pad_s // SLOTS, dtype=F32)
    slot_pad = ((0, 0), (0, SLOTS - n_new))
    ws_new = jnp.pad(ws[:, :n_new, :n_new], ((0, 0),) + slot_pad[1:] + slot_pad[1:])
    wmix_s = jax.vmap(lambda m: jnp.kron(eye, m))(ws_new).astype(BF16)
    bmix_s = jnp.broadcast_to(jnp.tile(jnp.pad(bs[:, :n_new], slot_pad), (1, pad_s // SLOTS))[:, :, None],
                              (N_HEADS, pad_s, HEAD_DIM))

    w = dict(g1=g_norm1[0][None, :], w_in=w_in[0].astype(BF16), lng=ln_v_g[0][None, :],
             lnb=ln_v_b[0][None, :], g2=g_norm2[0][None, :], w_out=w_out[0].astype(BF16),
             w_query=w_query[0].astype(BF16), sub_keys=sub_keys[0].astype(BF16),
             u=expert_u[0].astype(BF16), vt=expert_v[0].astype(BF16).T, g_final=g_final[None, :])

    xp = x_prompt.reshape(batch * seq, d)
    y_p, _, _, _, (k_win, v_win) = _trunk(
        xp, mods_p, batch, win, _prompt_attention, w, wmix_p, bmix_p,
        tm_in=512, tm_mix=256, tm_sel=256, tt=512, te=1024)

    xs = jnp.pad(x_sample, ((0, 0), (0, SLOTS - n_new), (0, 0))).reshape(n_dec * SLOTS, d)
    xs = jnp.pad(xs, ((0, pad_s - n_dec * SLOTS), (0, 0)))

    def sample_att(q, k, v):
        o = _sample_attention(q[0], k[0], v[0], cache_k[0], cache_v[0], n_new)
        o = o.transpose(1, 0, 2, 3).reshape(N_HEADS, n_dec * SLOTS, HEAD_DIM)
        return jnp.pad(o, ((0, 0), (0, pad_s - n_dec * SLOTS), (0, 0)))[None]

    y_s, k_s, v_s, vn_s, _ = _trunk(
        xs, mods_s, 1, 0, sample_att, w, wmix_s, bmix_s,
        tm_in=pad_s, tm_mix=pad_s, tm_sel=pad_s, tt=pad_s, te=1024)

    new_rows = lambda t: t[:n_dec * SLOTS].reshape(n_dec, SLOTS, -1)[:, :n_new]
    new_heads = lambda t: t[0, :, :n_dec * SLOTS].reshape(N_HEADS, n_dec, SLOTS, HEAD_DIM)[:, :, :n_new
                                                                                         ].transpose(1, 2, 0, 3)
    return (y_p.reshape(batch, seq, d),
            new_rows(y_s),
            k_win.reshape(1, batch, win, N_HEADS, HEAD_DIM),
            v_win.reshape(1, batch, win, N_HEADS, HEAD_DIM),
            new_heads(k_s)[None],
            new_heads(v_s)[None],
            new_rows(vn_s)[None])
```

```python
import functools

import numpy as np
import jax
import jax.numpy as jnp
from jax import lax
from jax.experimental import pallas as pl
from jax.experimental.pallas import tpu as pltpu

F32 = jnp.float32
BF16 = jnp.bfloat16

EPS = 1e-6
LANES = 128
SUBLANES = 8
HEAD_DIM = 128
N_HEADS = 8
ATT_WIDTH = N_HEADS * HEAD_DIM
BRANCHES = ((128, 1), (512, 4), (2048, 16))
BAND = 128
SPAN = BAND * BRANCHES[-1][1]
N_KEYS = 128
TOPK = 16
NEG = -1e30
VMEM_LIMIT = 56 * 1024 * 1024
COEF_LANE_CHUNKS = 4
SLOTS = SUBLANES


def _params(sem, flags=None):
    return pltpu.CompilerParams(dimension_semantics=sem, vmem_limit_bytes=VMEM_LIMIT, flags=flags)


def _gelu(x):
    return 0.5 * x * (1.0 + lax.erf(x * (2.0 ** -0.5)))


def _rms(x, g):
    return x * lax.rsqrt(jnp.mean(x * x, axis=-1, keepdims=True) + EPS) * g


def _slope(h):
    return 2.0 ** (-8.0 * (h + 1) / N_HEADS)


def _adaln_kernel(c_ref, w_ref, b_ref, o_ref):
    c = c_ref[...]
    a = c / (1.0 + jnp.exp(-c))
    o_ref[...] = jnp.dot(a.astype(BF16), w_ref[...].astype(BF16),
                         preferred_element_type=F32) + b_ref[...]


def _adaln(c_all, w_ada, b_ada):
    rows, d = c_all.shape
    n = w_ada.shape[1]
    tn = 1024
    return pl.pallas_call(
        _adaln_kernel,
        grid=(n // tn,),
        in_specs=[pl.BlockSpec((rows, d), lambda j: (0, 0)),
                  pl.BlockSpec((d, tn), lambda j: (0, j)),
                  pl.BlockSpec((1, tn), lambda j: (0, j))],
        out_specs=pl.BlockSpec((rows, tn), lambda j: (0, j)),
        out_shape=jax.ShapeDtypeStruct((rows, n), F32),
        compiler_params=_params(("arbitrary",)),
        name="adaln",
    )(c_all, w_ada, b_ada)


def _inproj_kernel(x_ref, sh_ref, sc_ref, g1_ref, w_ref, lng_ref, lnb_ref,
                   q_ref, k_ref, v_ref, u_ref, vn_ref, *rest, windows):
    if windows:
        kwin_ref, vwin_ref, h_scr = rest
    else:
        (h_scr,) = rest
    j = pl.program_id(1)

    @pl.when(j == 0)
    def _():
        y = _rms(x_ref[...], g1_ref[...])
        h_scr[...] = (y * (1.0 + sc_ref[...]) + sh_ref[...]).astype(BF16)

    p = jnp.dot(h_scr[...], w_ref[...], preferred_element_type=F32)

    def heads_out(ref, val, win_ref=None):
        for h in range(N_HEADS):
            piece = val[:, h * HEAD_DIM:(h + 1) * HEAD_DIM]
            ref[h] = piece
            if win_ref is not None:
                win_ref[:, h, :] = piece

    @pl.when(j == 0)
    def _():
        heads_out(q_ref, p * (HEAD_DIM ** -0.5))

    @pl.when(j == 1)
    def _():
        heads_out(k_ref, p, kwin_ref if windows else None)

    @pl.when(j == 2)
    def _():
        heads_out(v_ref, p, vwin_ref if windows else None)

    @pl.when(j == 3)
    def _():
        u_ref[...] = _gelu(p).astype(BF16)

    @pl.when(j == 4)
    def _():
        g = _gelu(p)
        mu = jnp.mean(g, axis=-1, keepdims=True)
        gc = g - mu
        var = jnp.mean(gc * gc, axis=-1, keepdims=True)
        vn_ref[...] = gc * lax.rsqrt(var + EPS) * lng_ref[...] + lnb_ref[...]


def _mod_spec(mod, tiles_per_group):
    _, r, d = mod.shape
    return pl.BlockSpec((None, r, d), lambda i, *_: (i // tiles_per_group, 0, 0))


def _inproj(x, sh1, sc1, g1, w_in, lng, lnb, tm, n_groups, window):
    t, d = x.shape
    sec = ATT_WIDTH
    n_sec = w_in.shape[1] // sec
    rows = t // n_groups
    tpg = rows // tm
    tok = lambda i, j: (i, 0)
    const = lambda i, j: (0, 0)
    out_spec = pl.BlockSpec((tm, sec), tok)
    head_spec = pl.BlockSpec((None, N_HEADS, tm, HEAD_DIM), lambda i, j: (i // tpg, 0, i % tpg, 0))
    head_shape = jax.ShapeDtypeStruct((n_groups, N_HEADS, rows, HEAD_DIM), F32)
    out_specs = [head_spec] * 3 + [out_spec] * 2
    out_shape = [head_shape] * 3 + [jax.ShapeDtypeStruct((t, sec), BF16),
                                    jax.ShapeDtypeStruct((t, sec), F32)]
    if window:
        wt = window // tm
        win_spec = pl.BlockSpec((tm, N_HEADS, HEAD_DIM), lambda i, j: (
            (i // tpg) * wt + jnp.maximum(i % tpg - (tpg - wt), 0), 0, 0))
        win_shape = jax.ShapeDtypeStruct((n_groups * window, N_HEADS, HEAD_DIM), F32)
        out_specs += [win_spec] * 2
        out_shape += [win_shape] * 2
    return pl.pallas_call(
        functools.partial(_inproj_kernel, windows=bool(window)),
        grid=(t // tm, n_sec),
        in_specs=[pl.BlockSpec((tm, d), tok),
                  _mod_spec(sh1, tpg), _mod_spec(sc1, tpg),
                  pl.BlockSpec((1, d), const),
                  pl.BlockSpec((d, sec), lambda i, j: (0, j)),
                  pl.BlockSpec((1, sec), const), pl.BlockSpec((1, sec), const)],
        out_specs=out_specs,
        out_shape=out_shape,
        scratch_shapes=[pltpu.VMEM((tm, d), BF16)],
        compiler_params=_params(("arbitrary", "arbitrary")),
        name="inproj",
    )(x, sh1, sc1, g1, w_in, lng, lnb)


def _att_bias():
    a = np.arange(BAND)[:, None]
    kk = np.arange(2 * BAND)[None, :]
    steps = BAND + a - kk
    valid = (steps >= 0) & (steps <= BAND)
    slopes = np.array([_slope(h) for h in range(N_HEADS)])
    out = []
    for _, dil in BRANCHES:
        bias = -slopes[:, None, None] * (dil * steps).astype(np.float64)[None]
        out.append(np.where(valid[None], bias, NEG))
    return jnp.asarray(np.stack(out), dtype=F32)


def _att_prompt_kernel(q_ref, kc_ref, kp_ref, vc_ref, vp_ref, bias_ref, o_ref, m_scr, l_scr, acc_scr):
    first_span = pl.program_id(2) == 0
    span = q_ref.shape[0]
    is_prev = lax.broadcasted_iota(jnp.int32, (BAND, 2 * BAND), 1) < BAND

    def rows_at(start, dil):
        return pl.ds(start, BAND) if dil == 1 else pl.ds(start, BAND, stride=dil)

    for bi, (_, dil) in enumerate(BRANCHES):
        for r in range(dil):
            for n in range(span // (BAND * dil)):
                start = r + dil * BAND * n
                rows = rows_at(start, dil)
                if n > 0:
                    prev = rows_at(start - dil * BAND, dil)
                    kp, vp = kc_ref[prev, :], vc_ref[prev, :]
                else:
                    prev = rows_at(span - dil * BAND + r, dil)
                    kp, vp = kp_ref[prev, :], vp_ref[prev, :]
                kw = jnp.concatenate([kp, kc_ref[rows, :]], axis=0).astype(BF16)
                vw = jnp.concatenate([vp, vc_ref[rows, :]], axis=0).astype(BF16)
                s = lax.dot_general(q_ref[rows, :].astype(BF16), kw, (((1,), (1,)), ((), ())),
                                    preferred_element_type=F32) + bias_ref[bi]
                if n == 0:
                    s = jnp.where(jnp.logical_and(first_span, is_prev), NEG, s)
                m_b = jnp.max(s, axis=-1, keepdims=True)
                p = jnp.exp(s - m_b)
                l_b = jnp.sum(p, axis=-1, keepdims=True)
                pv = jnp.dot(p.astype(BF16), vw, preferred_element_type=F32)
                if bi == 0:
                    m_scr[rows, :] = jnp.broadcast_to(m_b, (BAND, LANES))
                    l_scr[rows, :] = jnp.broadcast_to(l_b, (BAND, LANES))
                    acc_scr[rows, :] = pv
                else:
                    m_o = m_scr[rows, :]
                    m_n = jnp.maximum(m_o, m_b)
                    e_o = jnp.exp(m_o - m_n)
                    e_b = jnp.exp(m_b - m_n)
                    m_scr[rows, :] = m_n
                    l_scr[rows, :] = e_o * l_scr[rows, :] + e_b * l_b
                    acc_scr[rows, :] = e_o * acc_scr[rows, :] + e_b * pv
    o_ref[...] = (acc_scr[...] / l_scr[...]).astype(o_ref.dtype)


def _prompt_attention(q, k, v):
    b, nh, seq, hd = q.shape
    cur = lambda bb, h, s: (bb, h, s, 0)
    prev = lambda bb, h, s: (bb, h, jnp.maximum(s - 1, 0), 0)
    blk = lambda m: pl.BlockSpec((None, None, SPAN, hd), m)
    return pl.pallas_call(
        _att_prompt_kernel,
        grid=(b, nh, seq // SPAN),
        in_specs=[blk(cur), blk(cur), blk(prev), blk(cur), blk(prev),
                  pl.BlockSpec((len(BRANCHES), None, BAND, 2 * BAND), lambda bb, h, s: (0, h, 0, 0))],
        out_specs=blk(cur),
        out_shape=jax.ShapeDtypeStruct(q.shape, BF16),
        scratch_shapes=[pltpu.VMEM((SPAN, LANES), F32)] * 3,
        compiler_params=_params(("parallel", "parallel", "arbitrary")),
        name="attention_prompt",
    )(q, k, k, v, v, _att_bias())


def _att_sample_kernel(q_ref, kn_ref, vn_ref, kt_ref, vt_ref, kr_ref, vr_ref, o_ref, *, n_new):
    o_ref[...] = jnp.zeros_like(o_ref)
    col = lax.broadcasted_iota(jnp.int32, (BAND, 1), 0).astype(F32)
    slot = lax.broadcasted_iota(jnp.int32, (SLOTS, 1), 0)
    tail = kt_ref.shape[0] // N_HEADS
    for h in range(N_HEADS):
        slope = _slope(h)
        knh, vnh = kn_ref[h], vn_ref[h]
        for t in range(n_new):
            q_row = q_ref[h, t:t + 1, :]
            parts = []
            for _, dil in BRANCHES:
                a = t if dil == 1 else 0
                if dil == 1:
                    rows = pl.ds((tail - BAND) * N_HEADS + h, BAND, stride=N_HEADS)
                    kp, vp = kt_ref[rows, :], vt_ref[rows, :]
                elif dil * BAND == tail:
                    rows = pl.ds(t * N_HEADS + h, BAND, stride=dil * N_HEADS)
                    kp, vp = kt_ref[rows, :], vt_ref[rows, :]
                else:
                    kp, vp = kr_ref[:, t * N_HEADS + h, :], vr_ref[:, t * N_HEADS + h, :]
                steps = float(BAND + a) - col
                s_p = jnp.sum(kp * q_row, axis=1, keepdims=True) - (slope * dil) * steps
                if a > 0:
                    s_p = jnp.where(steps <= float(BAND), s_p, NEG)
                s_n = jnp.sum(knh * q_row, axis=1, keepdims=True)
                if dil == 1:
                    s_n = jnp.where(slot <= t, s_n - slope * (t - slot).astype(F32), NEG)
                else:
                    s_n = jnp.where(slot == t, s_n, NEG)
                m = jnp.maximum(jnp.max(s_p, axis=0, keepdims=True), jnp.max(s_n, axis=0, keepdims=True))
                p_p = jnp.exp(s_p - m)
                p_n = jnp.exp(s_n - m)
                l = jnp.sum(p_p, axis=0, keepdims=True) + jnp.sum(p_n, axis=0, keepdims=True)
                acc = (jnp.sum(p_p * vp, axis=0, keepdims=True)
                       + jnp.sum(p_n * vnh, axis=0, keepdims=True))
                parts.append((m, l, acc))
            m_all = functools.reduce(jnp.maximum, [m for m, _, _ in parts])
            l_all = sum(l * jnp.exp(m - m_all) for m, l, _ in parts)
            acc_all = sum(acc * jnp.exp(m - m_all) for m, _, acc in parts)
            o_ref[h, t:t + 1, :] = acc_all / l_all


def _sample_attention(q, k, v, cache_k, cache_v, n_new):
    n_batch, buf, nh, hd = cache_k.shape
    res16 = BRANCHES[-1][1]
    tail = 4 * BAND
    assert buf == res16 * BAND and n_new * nh <= 32
    rows2d = lambda c: c.reshape(n_batch, buf * nh, hd)
    resid = lambda c: c.reshape(n_batch, BAND, res16 * nh, hd)
    new_spec = pl.BlockSpec((nh, SLOTS, hd), lambda b: (0, b, 0))
    tail_spec = pl.BlockSpec((None, tail * nh, hd), lambda b: (b, buf // tail - 1, 0))
    res_spec = pl.BlockSpec((None, BAND, 32, hd), lambda b: (b, 0, 0, 0))
    return pl.pallas_call(
        functools.partial(_att_sample_kernel, n_new=n_new),
        grid=(n_batch,),
        in_specs=[new_spec, new_spec, new_spec, tail_spec, tail_spec, res_spec, res_spec],
        out_specs=pl.BlockSpec((None, nh, SLOTS, hd), lambda b: (b, 0, 0, 0)),
        out_shape=jax.ShapeDtypeStruct((n_batch, nh, SLOTS, hd), F32),
        compiler_params=_params(("parallel",)),
        name="attention_sample",
    )(q, k, v, rows2d(cache_k), rows2d(cache_v), resid(cache_k), resid(cache_v))


def _mix_kernel(att_ref, u_ref, vn_ref, x_ref, gt1_ref, sh2_ref, sc2_ref, g2_ref, wmix_ref, bmix_ref,
                wout_ref, x1_ref, h2_ref, h2t_ref, mix_scr):
    tm = x_ref.shape[0]
    for h in range(N_HEADS):
        mix_scr[:, h * HEAD_DIM:(h + 1) * HEAD_DIM] = att_ref[h].astype(BF16)
    for c in range(tm // BAND):
        rows = slice(c * BAND, (c + 1) * BAND)
        for g in range(N_HEADS):
            cs = slice(g * HEAD_DIM, (g + 1) * HEAD_DIM)
            m = jnp.dot(wmix_ref[g], vn_ref[rows, cs].astype(BF16),
                        preferred_element_type=F32) + bmix_ref[g]
            mix_scr[rows, ATT_WIDTH + g * HEAD_DIM:ATT_WIDTH + (g + 1) * HEAD_DIM] = (
                u_ref[rows, cs].astype(F32) * m).astype(BF16)
    y = jnp.dot(mix_scr[...], wout_ref[...], preferred_element_type=F32)
    x1 = x_ref[...] + gt1_ref[...] * y
    x1_ref[...] = x1
    h2 = _rms(x1, g2_ref[...]) * (1.0 + sc2_ref[...]) + sh2_ref[...]
    h2_ref[...] = h2.astype(BF16)
    h2t_ref[...] = h2.T.astype(BF16)


def _mix(att, u, vn, x, gt1, sh2, sc2, g2, wmix, bmix, w_out, tm):
    t, d = x.shape
    tpg = att.shape[2] // tm
    tok = lambda i: (i, 0)
    c2 = lambda i: (0, 0)
    c3 = lambda i: (0, 0, 0)
    return pl.pallas_call(
        _mix_kernel,
        grid=(t // tm,),
        in_specs=[pl.BlockSpec((None, N_HEADS, tm, HEAD_DIM), lambda i: (i // tpg, 0, i % tpg, 0)),
                  pl.BlockSpec((tm, ATT_WIDTH), tok), pl.BlockSpec((tm, ATT_WIDTH), tok),
                  pl.BlockSpec((tm, d), tok),
                  _mod_spec(gt1, tpg), _mod_spec(sh2, tpg), _mod_spec(sc2, tpg),
                  pl.BlockSpec((1, d), c2),
                  pl.BlockSpec(wmix.shape, c3), pl.BlockSpec(bmix.shape, c3),
                  pl.BlockSpec(w_out.shape, c2)],
        out_specs=[pl.BlockSpec((tm, d), tok), pl.BlockSpec((tm, d), tok),
                   pl.BlockSpec((d, tm), lambda i: (0, i))],
        out_shape=[jax.ShapeDtypeStruct((t, d), F32),
                   jax.ShapeDtypeStruct((t, d), BF16),
                   jax.ShapeDtypeStruct((d, t), BF16)],
        scratch_shapes=[pltpu.VMEM((tm, d), BF16)],
        compiler_params=_params(("parallel",)),
        name="mix_outproj",
    )(att, u, vn, x, gt1, sh2, sc2, g2, wmix, bmix, w_out)


def _cand_layout():
    rows = []
    rows += [(0, j) for j in range(16)]
    rows += [(1, j) for j in range(8)]
    rows += [(8 + r, 0) for r in range(8)]
    rows += [(2, j) if j < 5 else None for j in range(8)]
    rows += [(3, j) if j < 4 else None for j in range(8)]
    rows += [(4, j) if j < 3 else None for j in range(8)]
    rows += [(r, 0) if 5 <= r < 8 else None for r in range(8)]
    rows += [(r, 1) if 5 <= r < 8 else None for r in range(8)]
    return rows


N_CAND_ROWS = 72
BIG_FLAT = 1e9
NO_EXPERT = 1e30


def _cand_flat():
    flat = np.full((N_CAND_ROWS, LANES), BIG_FLAT, np.float32)
    for r, cell in enumerate(_cand_layout()):
        if cell is not None:
            flat[r, :] = 16 * cell[0] + cell[1]
    return jnp.asarray(flat)


def _top16_ranked(s, row_id, top_ref):
    rank = jnp.full(s.shape, float(TOPK), F32)
    for it in range(TOPK):
        m = jnp.max(s, axis=0, keepdims=True)
        idx = jnp.min(jnp.where(s == m, row_id, float(N_KEYS)), axis=0, keepdims=True)
        sel = row_id == idx
        rank = jnp.where(sel, float(it), rank)
        s = jnp.where(sel, -jnp.inf, s)
        top_ref[it:it + 1, :] = m
    return rank


def _sort16_network():
    def merge(lo, hi, r):
        step = r * 2
        if step < hi - lo:
            yield from merge(lo, hi, step)
            yield from merge(lo + r, hi, step)
            yield from [(i, i + r) for i in range(lo + r, hi - r, step)]
        else:
            yield (lo, lo + r)

    def sort(lo, hi):
        if hi > lo:
            mid = lo + (hi - lo) // 2
            yield from sort(lo, mid)
            yield from sort(mid + 1, hi)
            yield from merge(lo, hi, 1)

    return list(sort(0, TOPK - 1))


def _top16_sorted(s, top_ref):
    n = N_KEYS // SUBLANES
    cols = [s[SUBLANES * v:SUBLANES * (v + 1), :] for v in range(n)]
    for i, j in _sort16_network():
        cols[i], cols[j] = jnp.maximum(cols[i], cols[j]), jnp.minimum(cols[i], cols[j])
    ties = jnp.zeros((1, s.shape[1]), F32)
    prev = None
    for it in range(TOPK + 1):
        m = jnp.max(cols[0], axis=0, keepdims=True)
        if prev is not None:
            ties = ties + jnp.where(m == prev, 1.0, 0.0)
        prev = m
        if it == TOPK:
            break
        sel = cols[0] == m
        ties = ties + jnp.sum(jnp.where(sel, 1.0, 0.0), axis=0, keepdims=True) - 1.0
        top_ref[it:it + 1, :] = m
        for v in range(TOPK - it):
            nxt = cols[v + 1] if v + 1 < n else jnp.full_like(cols[v], -jnp.inf)
            cols[v] = jnp.where(sel, nxt, cols[v])
    return ties


def _select_block(h, lc, toks, exact, keys_ref, qp_scr, top1_scr, top2_scr, flat, row_id, row8,
                  thr_ref, g1_ref, s2_ref, e2_ref):
    valid = flat < BIG_FLAT
    sides = []
    for p, top_scr in ((0, top1_scr), (1, top2_scr)):
        col = pl.multiple_of((2 * h + p) * N_KEYS, N_KEYS)
        s = lax.dot_general(keys_ref[p], qp_scr[toks, pl.ds(col, N_KEYS)],
                            (((1,), (1,)), ((), ())), preferred_element_type=F32)
        sides.append((s, _top16_ranked(s, row_id, top_scr) if exact else _top16_sorted(s, top_scr)))
    (s1, info1), (s2, info2) = sides
    a1 = top1_scr[...]
    a2 = top2_scr[...]
    v = jnp.concatenate([
        a1[0:1] + a2[0:16],
        a1[1:2] + a2[0:8],
        a1[8:16] + a2[0:1],
        a1[2:3] + a2[0:8],
        a1[3:4] + a2[0:8],
        a1[4:5] + a2[0:8],
        a1[0:8] + a2[0:1],
        a1[0:8] + a2[1:2]], axis=0)
    vw = jnp.where(valid, v, -jnp.inf)
    selm = jnp.zeros(v.shape, F32)
    for _ in range(TOPK):
        m = jnp.max(vw, axis=0, keepdims=True)
        sel = vw == m
        if exact:
            f = jnp.min(jnp.where(sel, flat, BIG_FLAT), axis=0, keepdims=True)
            sel = flat == f
        selm = jnp.where(sel, 1.0, selm)
        vw = jnp.where(sel, -jnp.inf, vw)
    vmax = a1[0:1] + a2[0:1]
    z = jnp.sum(jnp.where(selm > 0.0, jnp.exp(v - vmax), 0.0), axis=0, keepdims=True)
    cnt_lo = selm[56:64] + selm[64:72]
    for i, (lo, hi) in enumerate(((0, 16), (16, 24), (32, 40), (40, 48), (48, 56))):
        ci = jnp.sum(selm[lo:hi], axis=0, keepdims=True)
        cnt_lo = jnp.where(row8 == i, ci, cnt_lo)
    cnt = jnp.concatenate([cnt_lo, selm[24:32]], axis=0)
    if exact:
        thr = jnp.where(cnt > 0.0, 0.5 - cnt, NO_EXPERT)
        key1, key2 = info1, -info2
        match = [float(i) for i in range(TOPK)]
    else:
        thr = jnp.full(cnt.shape, NO_EXPERT, F32)
        for j in range(TOPK):
            thr = jnp.where(cnt == float(j + 1), a2[j:j + 1], thr)
        key1, key2 = s1, s2
        match = [a1[i:i + 1] for i in range(TOPK)]
    thr_a = jnp.full((N_KEYS, LANES), NO_EXPERT, F32)
    for i in range(TOPK):
        thr_a = jnp.where(key1 == match[i], thr[i:i + 1], thr_a)
    thr_ref[h, lc] = thr_a
    g1_ref[h, lc] = jnp.exp(s1 - a1[0:1]) / z
    s2_ref[h, lc] = key2
    e2_ref[h, lc] = jnp.exp(s2 - a2[0:1])
    if exact:
        return None
    n3 = jnp.sum(selm, axis=0, keepdims=True) - float(TOPK)
    return jnp.max(info1 + info2 + n3)


def _select_kernel(h2_ref, wq_ref, keys_ref, flat_ref, thr_ref, g1_ref, s2_ref, e2_ref,
                   qp_scr, top1_scr, top2_scr):
    tm = h2_ref.shape[0]
    qp_scr[...] = jnp.dot(h2_ref[...], wq_ref[...], preferred_element_type=F32).astype(BF16)
    row_id = lax.broadcasted_iota(jnp.int32, (N_KEYS, LANES), 0).astype(F32)
    row8 = lax.broadcasted_iota(jnp.int32, (8, LANES), 0)
    flat = flat_ref[...]
    for lc in range(tm // LANES):
        toks = slice(lc * LANES, (lc + 1) * LANES)
        block = functools.partial(
            _select_block, lc=lc, toks=toks, keys_ref=keys_ref, qp_scr=qp_scr, top1_scr=top1_scr,
            top2_scr=top2_scr, flat=flat, row_id=row_id, row8=row8, thr_ref=thr_ref, g1_ref=g1_ref,
            s2_ref=s2_ref, e2_ref=e2_ref)

        def head_body(h, carry, block=block):
            ties = block(h, exact=False)

            @pl.when(ties > 0.5)
            def _():
                block(h, exact=True)

            return carry

        lax.fori_loop(0, N_HEADS, head_body, 0)


def _select(h2, w_query, sub_keys, tm):
    t, d = h2.shape
    out_spec = pl.BlockSpec((N_HEADS, tm // LANES, N_KEYS, LANES), lambda i: (0, i, 0, 0))
    slab = lambda dt: jax.ShapeDtypeStruct((N_HEADS, t // LANES, N_KEYS, LANES), dt)
    return pl.pallas_call(
        _select_kernel,
        grid=(t // tm,),
        in_specs=[pl.BlockSpec((tm, d), lambda i: (i, 0)),
                  pl.BlockSpec(w_query.shape, lambda i: (0, 0)),
                  pl.BlockSpec(sub_keys.shape, lambda i: (0, 0, 0)),
                  pl.BlockSpec((N_CAND_ROWS, LANES), lambda i: (0, 0))],
        out_specs=[out_spec] * 4,
        out_shape=[slab(F32)] * 4,
        scratch_shapes=[pltpu.VMEM((tm, w_query.shape[1]), BF16),
                        pltpu.VMEM((TOPK, LANES), F32), pltpu.VMEM((TOPK, LANES), F32)],
        compiler_params=_params(("parallel",)),
        name="peer_select",
    )(h2, w_query, sub_keys, _cand_flat())


def _coef_kernel(thr_ref, g1_ref, s2_ref, e2_ref, c_ref):
    n_l, n_a = thr_ref.shape[1], thr_ref.shape[2]
    for lc in range(n_l):
        for ai in range(n_a):
            rows = slice(ai * N_KEYS, (ai + 1) * N_KEYS)
            coef = jnp.zeros((N_KEYS, LANES), F32)
            for h in range(N_HEADS):
                thr = thr_ref[h, lc, ai:ai + 1, :]
                g1 = g1_ref[h, lc, ai:ai + 1, :]
                coef = coef + jnp.where(s2_ref[h, lc] >= thr, e2_ref[h, lc], 0.0) * g1
            c_ref[lc, rows, :] = coef.astype(BF16)


def _coef(sel, te, n_exp):
    n_lc = sel[0].shape[1]
    n_l = min(n_lc, COEF_LANE_CHUNKS)
    a_spec = pl.BlockSpec((N_HEADS, n_l, te // N_KEYS, LANES), lambda i, j: (0, i, j, 0))
    b_spec = pl.BlockSpec((N_HEADS, n_l, N_KEYS, LANES), lambda i, j: (0, i, 0, 0))
    return pl.pallas_call(
        _coef_kernel,
        grid=(n_lc // n_l, n_exp // te),
        in_specs=[a_spec, a_spec, b_spec, b_spec],
        out_specs=pl.BlockSpec((n_l, te, LANES), lambda i, j: (i, j, 0)),
        out_shape=jax.ShapeDtypeStruct((n_lc, n_exp, LANES), BF16),
        compiler_params=_params(("parallel", "arbitrary")),
        name="peer_coef",
    )(*sel)


def _peer_kernel(h2t_ref, u_ref, vt_ref, c_ref, out_ref):
    j = pl.program_id(1)
    tt = h2t_ref.shape[1]

    @pl.when(j == 0)
    def _():
        out_ref[...] = jnp.zeros_like(out_ref)

    act = jnp.dot(u_ref[...], h2t_ref[...], preferred_element_type=F32)
    p = jnp.concatenate(
        [_gelu(act[:, lc * LANES:(lc + 1) * LANES]).astype(BF16) * c_ref[lc]
         for lc in range(tt // LANES)], axis=1)
    out_ref[...] += jnp.dot(vt_ref[...], p, preferred_element_type=F32)


def _peer(h2t, u, vt, sel, tt, te):
    d, t = h2t.shape
    n_exp = u.shape[0]
    coef = _coef(sel, te, n_exp)
    return pl.pallas_call(
        _peer_kernel,
        grid=(t // tt, n_exp // te),
        in_specs=[pl.BlockSpec((d, tt), lambda i, j: (0, i)),
                  pl.BlockSpec((te, d), lambda i, j: (j, 0)),
                  pl.BlockSpec((d, te), lambda i, j: (0, j)),
                  pl.BlockSpec((tt // LANES, te, LANES), lambda i, j: (i, j, 0))],
        out_specs=pl.BlockSpec((d, tt), lambda i, j: (0, i)),
        out_shape=jax.ShapeDtypeStruct((d, t), F32),
        compiler_params=_params(("parallel", "arbitrary")),
        name="peer_dense",
    )(h2t, u, vt, coef)


def _final_kernel(x1_ref, ffnt_ref, gt2_ref, gf_ref, y_ref):
    x2 = x1_ref[...] + gt2_ref[...] * ffnt_ref[...].T
    y_ref[...] = _rms(x2, gf_ref[...])


def _final(x1, ffnt, gt2, g_final, tm, tiles_per_group):
    t, d = x1.shape
    return pl.pallas_call(
        _final_kernel,
        grid=(t // tm,),
        in_specs=[pl.BlockSpec((tm, d), lambda i: (i, 0)),
                  pl.BlockSpec((d, tm), lambda i: (0, i)),
                  _mod_spec(gt2, tiles_per_group),
                  pl.BlockSpec((1, d), lambda i: (0, 0))],
        out_specs=pl.BlockSpec((tm, d), lambda i: (i, 0)),
        out_shape=jax.ShapeDtypeStruct((t, d), F32),
        compiler_params=_params(("parallel",)),
        name="final_norm",
    )(x1, ffnt, gt2, g_final)


def _trunk(x, mods, n_groups, window, att_fn, w, wmix, bmix, tm_in, tm_mix, tm_sel, tt, te):
    sh1, sc1, gt1, sh2, sc2, gt2 = mods
    rows = x.shape[0] // n_groups
    q, k, v, u, vn, *wins = _inproj(x, sh1, sc1, w["g1"], w["w_in"], w["lng"], w["lnb"], tm_in,
                                    n_groups, window)
    att = att_fn(q, k, v)
    x1, h2, h2t = _mix(att, u, vn, x, gt1, sh2, sc2, w["g2"], wmix, bmix, w["w_out"], tm_mix)
    sel = _select(h2, w["w_query"], w["sub_keys"], tm_sel)
    ffnt = _peer(h2t, w["u"], w["vt"], sel, tt, te)
    y = _final(x1, ffnt, gt2, w["g_final"], tm_mix, rows // tm_mix)
    return y, k, v, vn, wins


def kernel(x_prompt, x_sample, cache_k, cache_v, c_prompt, c_sample, w_ada, b_ada, g_norm1, w_in,
           ln_v_g, ln_v_b, w_spatial, b_spatial, w_out, g_norm2, w_query, sub_keys, expert_u,
           expert_v, g_final):
    depth = w_ada.shape[0]
    assert depth == 1, "single-layer trunk"
    batch, seq, d = x_prompt.shape
    n_dec, n_new, _ = x_sample.shape
    buf = cache_k.shape[2]
    assert seq % SPAN == 0 and buf == SPAN and n_new <= SLOTS
    win = min(BRANCHES[-1][0], seq)
    pad_s = BAND
    assert n_dec * SLOTS <= pad_s

    n_c = batch + n_dec
    c_all = jnp.pad(jnp.concatenate([c_prompt, c_sample], axis=0), ((0, -n_c % SUBLANES), (0, 0)))
    mod = _adaln(c_all, w_ada[0], b_ada[0][None, :])
    mods = [mod[:, i * d:(i + 1) * d] for i in range(6)]
    mods_p = [m[:batch, None, :] for m in mods]
    mods_s = [jnp.pad(jnp.repeat(m[batch:n_c], SLOTS, axis=0), ((0, pad_s - n_dec * SLOTS), (0, 0)))[None]
              for m in mods]

    tri = jnp.tril(jnp.ones((BAND, BAND), dtype=bool))
    ws = jnp.where(tri, w_spatial[0], 0.0)
    bs = b_spatial[0]
    wmix_p = ws.astype(BF16)
    bmix_p = jnp.broadcast_to(bs[:, :, None], (N_HEADS, BAND, HEAD_DIM))
    eye = jnp.eye(pad_s // SLOTS, dtype=F32)
    slot_pad = ((0, 0), (0, SLOTS - n_new))
    ws_new = jnp.pad(ws[:, :n_new, :n_new], ((0, 0),) + slot_pad[1:] + slot_pad[1:])
    wmix_s = jax.vmap(lambda m: jnp.kron(eye, m))(ws_new).astype(BF16)
    bmix_s = jnp.broadcast_to(jnp.tile(jnp.pad(bs[:, :n_new], slot_pad), (1, pad_s // SLOTS))[:, :, None],
                              (N_HEADS, pad_s, HEAD_DIM))

    w = dict(g1=g_norm1[0][None, :], w_in=w_in[0].astype(BF16), lng=ln_v_g[0][None, :],
             lnb=ln_v_b[0][None, :], g2=g_norm2[0][None, :], w_out=w_out[0].astype(BF16),
             w_query=w_query[0].astype(BF16), sub_keys=sub_keys[0].astype(BF16),
             u=expert_u[0].astype(BF16), vt=expert_v[0].astype(BF16).T, g_final=g_final[None, :])

    xp = x_prompt.reshape(batch * seq, d)
    y_p, _, _, _, (k_win, v_win) = _trunk(
        xp, mods_p, batch, win, _prompt_attention, w, wmix_p, bmix_p,
        tm_in=512, tm_mix=256, tm_sel=256, tt=512, te=1024)

    xs = jnp.pad(x_sample, ((0, 0), (0, SLOTS - n_new), (0, 0))).reshape(n_dec * SLOTS, d)
    xs = jnp.pad(xs, ((0, pad_s - n_dec * SLOTS), (0, 0)))

    def sample_att(q, k, v):
        o = _sample_attention(q[0], k[0], v[0], cache_k[0], cache_v[0], n_new)
        o = o.transpose(1, 0, 2, 3).reshape(N_HEADS, n_dec * SLOTS, HEAD_DIM)
        return jnp.pad(o, ((0, 0), (0, pad_s - n_dec * SLOTS), (0, 0)))[None]

    y_s, k_s, v_s, vn_s, _ = _trunk(
        xs, mods_s, 1, 0, sample_att, w, wmix_s, bmix_s,
        tm_in=pad_s, tm_mix=pad_s, tm_sel=pad_s, tt=pad_s, te=1024)

    new_rows = lambda t: t[:n_dec * SLOTS].reshape(n_dec, SLOTS, -1)[:, :n_new]
    new_heads = lambda t: t[0, :, :n_dec * SLOTS].reshape(N_HEADS, n_dec, SLOTS, HEAD_DIM)[:, :, :n_new
                                                                                         ].transpose(1, 2, 0, 3)
    return (y_p.reshape(batch, seq, d),
            new_rows(y_s),
            k_win.reshape(1, batch, win, N_HEADS, HEAD_DIM),
            v_win.reshape(1, batch, win, N_HEADS, HEAD_DIM),
            new_heads(k_s)[None],
            new_heads(v_s)[None],
            new_rows(vn_s)[None])
```

```python
import functools

import numpy as np
import jax
import jax.numpy as jnp
from jax import lax
from jax.experimental import pallas as pl
from jax.experimental.pallas import tpu as pltpu

F32 = jnp.float32
BF16 = jnp.bfloat16

EPS = 1e-6
LANES = 128
SUBLANES = 8
HEAD_DIM = 128
N_HEADS = 8
ATT_WIDTH = N_HEADS * HEAD_DIM
BRANCHES = ((128, 1), (512, 4), (2048, 16))
BAND = 128
SPAN = BAND * BRANCHES[-1][1]
N_KEYS = 128
TOPK = 16
NEG = -1e30
VMEM_LIMIT = 56 * 1024 * 1024
COEF_LANE_CHUNKS = 4
SLOTS = SUBLANES


def _params(sem, flags=None):
    return pltpu.CompilerParams(dimension_semantics=sem, vmem_limit_bytes=VMEM_LIMIT, flags=flags)


def _gelu(x):
    return 0.5 * x * (1.0 + lax.erf(x * (2.0 ** -0.5)))


def _rms(x, g):
    return x * lax.rsqrt(jnp.mean(x * x, axis=-1, keepdims=True) + EPS) * g


def _slope(h):
    return 2.0 ** (-8.0 * (h + 1) / N_HEADS)


def _adaln_kernel(c_ref, w_ref, b_ref, o_ref):
    c = c_ref[...]
    a = c / (1.0 + jnp.exp(-c))
    o_ref[...] = jnp.dot(a.astype(BF16), w_ref[...].astype(BF16),
                         preferred_element_type=F32) + b_ref[...]


def _adaln(c_all, w_ada, b_ada):
    rows, d = c_all.shape
    n = w_ada.shape[1]
    tn = 1024
    return pl.pallas_call(
        _adaln_kernel,
        grid=(n // tn,),
        in_specs=[pl.BlockSpec((rows, d), lambda j: (0, 0)),
                  pl.BlockSpec((d, tn), lambda j: (0, j)),
                  pl.BlockSpec((1, tn), lambda j: (0, j))],
        out_specs=pl.BlockSpec((rows, tn), lambda j: (0, j)),
        out_shape=jax.ShapeDtypeStruct((rows, n), F32),
        compiler_params=_params(("arbitrary",)),
        name="adaln",
    )(c_all, w_ada, b_ada)


def _inproj_kernel(x_ref, sh_ref, sc_ref, g1_ref, w_ref, lng_ref, lnb_ref,
                   q_ref, k_ref, v_ref, u_ref, vn_ref, *rest, windows):
    if windows:
        kwin_ref, vwin_ref, h_scr = rest
    else:
        (h_scr,) = rest
    j = pl.program_id(1)

    @pl.when(j == 0)
    def _():
        y = _rms(x_ref[...], g1_ref[...])
        h_scr[...] = (y * (1.0 + sc_ref[...]) + sh_ref[...]).astype(BF16)

    p = jnp.dot(h_scr[...], w_ref[...], preferred_element_type=F32)

    def heads_out(ref, val, win_ref=None):
        for h in range(N_HEADS):
            piece = val[:, h * HEAD_DIM:(h + 1) * HEAD_DIM]
            ref[h] = piece
            if win_ref is not None:
                win_ref[:, h, :] = piece

    @pl.when(j == 0)
    def _():
        heads_out(q_ref, p * (HEAD_DIM ** -0.5))

    @pl.when(j == 1)
    def _():
        heads_out(k_ref, p, kwin_ref if windows else None)

    @pl.when(j == 2)
    def _():
        heads_out(v_ref, p, vwin_ref if windows else None)

    @pl.when(j == 3)
    def _():
        u_ref[...] = _gelu(p).astype(BF16)

    @pl.when(j == 4)
    def _():
        g = _gelu(p)
        mu = jnp.mean(g, axis=-1, keepdims=True)
        gc = g - mu
        var = jnp.mean(gc * gc, axis=-1, keepdims=True)
        vn_ref[...] = gc * lax.rsqrt(var + EPS) * lng_ref[...] + lnb_ref[...]


def _mod_spec(mod, tiles_per_group):
    _, r, d = mod.shape
    return pl.BlockSpec((None, r, d), lambda i, *_: (i // tiles_per_group, 0, 0))


def _inproj(x, sh1, sc1, g1, w_in, lng, lnb, tm, n_groups, window):
    t, d = x.shape
    sec = ATT_WIDTH
    n_sec = w_in.shape[1] // sec
    rows = t // n_groups
    tpg = rows // tm
    tok = lambda i, j: (i, 0)
    const = lambda i, j: (0, 0)
    out_spec = pl.BlockSpec((tm, sec), tok)
    head_spec = pl.BlockSpec((None, N_HEADS, tm, HEAD_DIM), lambda i, j: (i // tpg, 0, i % tpg, 0))
    head_shape = jax.ShapeDtypeStruct((n_groups, N_HEADS, rows, HEAD_DIM), F32)
    out_specs = [head_spec] * 3 + [out_spec] * 2
    out_shape = [head_shape] * 3 + [jax.ShapeDtypeStruct((t, sec), BF16),
                                    jax.ShapeDtypeStruct((t, sec), F32)]
    if window:
        wt = window // tm
        win_spec = pl.BlockSpec((tm, N_HEADS, HEAD_DIM), lambda i, j: (
            (i // tpg) * wt + jnp.maximum(i % tpg - (tpg - wt), 0), 0, 0))
        win_shape = jax.ShapeDtypeStruct((n_groups * window, N_HEADS, HEAD_DIM), F32)
        out_specs += [win_spec] * 2
        out_shape += [win_shape] * 2
    return pl.pallas_call(
        functools.partial(_inproj_kernel, windows=bool(window)),
        grid=(t // tm, n_sec),
        in_specs=[pl.BlockSpec((tm, d), tok),
                  _mod_spec(sh1, tpg), _mod_spec(sc1, tpg),
                  pl.BlockSpec((1, d), const),
                  pl.BlockSpec((d, sec), lambda i, j: (0, j)),
                  pl.BlockSpec((1, sec), const), pl.BlockSpec((1, sec), const)],
        out_specs=out_specs,
        out_shape=out_shape,
        scratch_shapes=[pltpu.VMEM((tm, d), BF16)],
        compiler_params=_params(("arbitrary", "arbitrary")),
        name="inproj",
    )(x, sh1, sc1, g1, w_in, lng, lnb)


def _att_bias():
    a = np.arange(BAND)[:, None]
    kk = np.arange(2 * BAND)[None, :]
    steps = BAND + a - kk
    valid = (steps >= 0) & (steps <= BAND)
    slopes = np.array([_slope(h) for h in range(N_HEADS)])
    out = []
    for _, dil in BRANCHES:
        bias = -slopes[:, None, None] * (dil * steps).astype(np.float64)[None]
        out.append(np.where(valid[None], bias, NEG))
    return jnp.asarray(np.stack(out), dtype=F32)


def _att_prompt_kernel(q_ref, kc_ref, kp_ref, vc_ref, vp_ref, bias_ref, o_ref, m_scr, l_scr, acc_scr):
    first_span = pl.program_id(2) == 0
    span = q_ref.shape[0]
    is_prev = lax.broadcasted_iota(jnp.int32, (BAND, 2 * BAND), 1) < BAND

    def rows_at(start, dil):
        return pl.ds(start, BAND) if dil == 1 else pl.ds(start, BAND, stride=dil)

    for bi, (_, dil) in enumerate(BRANCHES):
        for r in range(dil):
            for n in range(span // (BAND * dil)):
                start = r + dil * BAND * n
                rows = rows_at(start, dil)
                if n > 0:
                    prev = rows_at(start - dil * BAND, dil)
                    kp, vp = kc_ref[prev, :], vc_ref[prev, :]
                else:
                    prev = rows_at(span - dil * BAND + r, dil)
                    kp, vp = kp_ref[prev, :], vp_ref[prev, :]
                kw = jnp.concatenate([kp, kc_ref[rows, :]], axis=0).astype(BF16)
                vw = jnp.concatenate([vp, vc_ref[rows, :]], axis=0).astype(BF16)
                s = lax.dot_general(q_ref[rows, :].astype(BF16), kw, (((1,), (1,)), ((), ())),
                                    preferred_element_type=F32) + bias_ref[bi]
                if n == 0:
                    s = jnp.where(jnp.logical_and(first_span, is_prev), NEG, s)
                m_b = jnp.max(s, axis=-1, keepdims=True)
                p = jnp.exp(s - m_b)
                l_b = jnp.sum(p, axis=-1, keepdims=True)
                pv = jnp.dot(p.astype(BF16), vw, preferred_element_type=F32)
                if bi == 0:
                    m_scr[rows, :] = jnp.broadcast_to(m_b, (BAND, LANES))
                    l_scr[rows, :] = jnp.broadcast_to(l_b, (BAND, LANES))
                    acc_scr[rows, :] = pv
                else:
                    m_o = m_scr[rows, :]
                    m_n = jnp.maximum(m_o, m_b)
                    e_o = jnp.exp(m_o - m_n)
                    e_b = jnp.exp(m_b - m_n)
                    m_scr[rows, :] = m_n
                    l_scr[rows, :] = e_o * l_scr[rows, :] + e_b * l_b
                    acc_scr[rows, :] = e_o * acc_scr[rows, :] + e_b * pv
    o_ref[...] = (acc_scr[...] / l_scr[...]).astype(o_ref.dtype)


def _prompt_attention(q, k, v):
    b, nh, seq, hd = q.shape
    cur = lambda bb, h, s: (bb, h, s, 0)
    prev = lambda bb, h, s: (bb, h, jnp.maximum(s - 1, 0), 0)
    blk = lambda m: pl.BlockSpec((None, None, SPAN, hd), m)
    return pl.pallas_call(
        _att_prompt_kernel,
        grid=(b, nh, seq // SPAN),
        in_specs=[blk(cur), blk(cur), blk(prev), blk(cur), blk(prev),
                  pl.BlockSpec((len(BRANCHES), None, BAND, 2 * BAND), lambda bb, h, s: (0, h, 0, 0))],
        out_specs=blk(cur),
        out_shape=jax.ShapeDtypeStruct(q.shape, BF16),
        scratch_shapes=[pltpu.VMEM((SPAN, LANES), F32)] * 3,
        compiler_params=_params(("parallel", "parallel", "arbitrary")),
        name="attention_prompt",
    )(q, k, k, v, v, _att_bias())


def _att_sample_kernel(q_ref, kn_ref, vn_ref, kt_ref, vt_ref, kr_ref, vr_ref, o_ref, *, n_new):
    o_ref[...] = jnp.zeros_like(o_ref)
    col = lax.broadcasted_iota(jnp.int32, (BAND, 1), 0).astype(F32)
    slot = lax.broadcasted_iota(jnp.int32, (SLOTS, 1), 0)
    tail = kt_ref.shape[0] // N_HEADS
    for h in range(N_HEADS):
        slope = _slope(h)
        knh, vnh = kn_ref[h], vn_ref[h]
        for t in range(n_new):
            q_row = q_ref[h, t:t + 1, :]
            parts = []
            for _, dil in BRANCHES:
                a = t if dil == 1 else 0
                if dil == 1:
                    rows = pl.ds((tail - BAND) * N_HEADS + h, BAND, stride=N_HEADS)
                    kp, vp = kt_ref[rows, :], vt_ref[rows, :]
                elif dil * BAND == tail:
                    rows = pl.ds(t * N_HEADS + h, BAND, stride=dil * N_HEADS)
                    kp, vp = kt_ref[rows, :], vt_ref[rows, :]
                else:
                    kp, vp = kr_ref[:, t * N_HEADS + h, :], vr_ref[:, t * N_HEADS + h, :]
                steps = float(BAND + a) - col
                s_p = jnp.sum(kp * q_row, axis=1, keepdims=True) - (slope * dil) * steps
                if a > 0:
                    s_p = jnp.where(steps <= float(BAND), s_p, NEG)
                s_n = jnp.sum(knh * q_row, axis=1, keepdims=True)
                if dil == 1:
                    s_n = jnp.where(slot <= t, s_n - slope * (t - slot).astype(F32), NEG)
                else:
                    s_n = jnp.where(slot == t, s_n, NEG)
                m = jnp.maximum(jnp.max(s_p, axis=0, keepdims=True), jnp.max(s_n, axis=0, keepdims=True))
                p_p = jnp.exp(s_p - m)
                p_n = jnp.exp(s_n - m)
                l = jnp.sum(p_p, axis=0, keepdims=True) + jnp.sum(p_n, axis=0, keepdims=True)
                acc = (jnp.sum(p_p * vp, axis=0, keepdims=True)
                       + jnp.sum(p_n * vnh, axis=0, keepdims=True))
                parts.append((m, l, acc))
            m_all = functools.reduce(jnp.maximum, [m for m, _, _ in parts])
            l_all = sum(l * jnp.exp(m - m_all) for m, l, _ in parts)
            acc_all = sum(acc * jnp.exp(m - m_all) for m, _, acc in parts)
            o_ref[h, t:t + 1, :] = acc_all / l_all


def _sample_attention(q, k, v, cache_k, cache_v, n_new):
    n_batch, buf, nh, hd = cache_k.shape
    res16 = BRANCHES[-1][1]
    tail = 4 * BAND
    assert buf == res16 * BAND and n_new * nh <= 32
    rows2d = lambda c: c.reshape(n_batch, buf * nh, hd)
    resid = lambda c: c.reshape(n_batch, BAND, res16 * nh, hd)
    new_spec = pl.BlockSpec((nh, SLOTS, hd), lambda b: (0, b, 0))
    tail_spec = pl.BlockSpec((None, tail * nh, hd), lambda b: (b, buf // tail - 1, 0))
    res_spec = pl.BlockSpec((None, BAND, 32, hd), lambda b: (b, 0, 0, 0))
    return pl.pallas_call(
        functools.partial(_att_sample_kernel, n_new=n_new),
        grid=(n_batch,),
        in_specs=[new_spec, new_spec, new_spec, tail_spec, tail_spec, res_spec, res_spec],
        out_specs=pl.BlockSpec((None, nh, SLOTS, hd), lambda b: (b, 0, 0, 0)),
        out_shape=jax.ShapeDtypeStruct((n_batch, nh, SLOTS, hd), F32),
        compiler_params=_params(("parallel",)),
        name="attention_sample",
    )(q, k, v, rows2d(cache_k), rows2d(cache_v), resid(cache_k), resid(cache_v))


def _mix_kernel(att_ref, u_ref, vn_ref, x_ref, gt1_ref, sh2_ref, sc2_ref, g2_ref, wmix_ref, bmix_ref,
                wout_ref, x1_ref, h2_ref, h2t_ref, mix_scr):
    tm = x_ref.shape[0]
    for h in range(N_HEADS):
        mix_scr[:, h * HEAD_DIM:(h + 1) * HEAD_DIM] = att_ref[h].astype(BF16)
    for c in range(tm // BAND):
        rows = slice(c * BAND, (c + 1) * BAND)
        for g in range(N_HEADS):
            cs = slice(g * HEAD_DIM, (g + 1) * HEAD_DIM)
            m = jnp.dot(wmix_ref[g], vn_ref[rows, cs].astype(BF16),
                        preferred_element_type=F32) + bmix_ref[g]
            mix_scr[rows, ATT_WIDTH + g * HEAD_DIM:ATT_WIDTH + (g + 1) * HEAD_DIM] = (
                u_ref[rows, cs].astype(F32) * m).astype(BF16)
    y = jnp.dot(mix_scr[...], wout_ref[...], preferred_element_type=F32)
    x1 = x_ref[...] + gt1_ref[...] * y
    x1_ref[...] = x1
    h2 = _rms(x1, g2_ref[...]) * (1.0 + sc2_ref[...]) + sh2_ref[...]
    h2_ref[...] = h2.astype(BF16)
    h2t_ref[...] = h2.T.astype(BF16)


def _mix(att, u, vn, x, gt1, sh2, sc2, g2, wmix, bmix, w_out, tm):
    t, d = x.shape
    tpg = att.shape[2] // tm
    tok = lambda i: (i, 0)
    c2 = lambda i: (0, 0)
    c3 = lambda i: (0, 0, 0)
    return pl.pallas_call(
        _mix_kernel,
        grid=(t // tm,),
        in_specs=[pl.BlockSpec((None, N_HEADS, tm, HEAD_DIM), lambda i: (i // tpg, 0, i % tpg, 0)),
                  pl.BlockSpec((tm, ATT_WIDTH), tok), pl.BlockSpec((tm, ATT_WIDTH), tok),
                  pl.BlockSpec((tm, d), tok),
                  _mod_spec(gt1, tpg), _mod_spec(sh2, tpg), _mod_spec(sc2, tpg),
                  pl.BlockSpec((1, d), c2),
                  pl.BlockSpec(wmix.shape, c3), pl.BlockSpec(bmix.shape, c3),
                  pl.BlockSpec(w_out.shape, c2)],
        out_specs=[pl.BlockSpec((tm, d), tok), pl.BlockSpec((tm, d), tok),
                   pl.BlockSpec((d, tm), lambda i: (0, i))],
        out_shape=[jax.ShapeDtypeStruct((t, d), F32),
                   jax.ShapeDtypeStruct((t, d), BF16),
                   jax.ShapeDtypeStruct((d, t), BF16)],
        scratch_shapes=[pltpu.VMEM((tm, d), BF16)],
        compiler_params=_params(("parallel",)),
        name="mix_outproj",
    )(att, u, vn, x, gt1, sh2, sc2, g2, wmix, bmix, w_out)


def _cand_layout():
    rows = []
    rows += [(0, j) for j in range(16)]
    rows += [(1, j) for j in range(8)]
    rows += [(8 + r, 0) for r in range(8)]
    rows += [(2, j) if j < 5 else None for j in range(8)]
    rows += [(3, j) if j < 4 else None for j in range(8)]
    rows += [(4, j) if j < 3 else None for j in range(8)]
    rows += [(r, 0) if 5 <= r < 8 else None for r in range(8)]
    rows += [(r, 1) if 5 <= r < 8 else None for r in range(8)]
    return rows


N_CAND_ROWS = 72
BIG_FLAT = 1e9
NO_EXPERT = 1e30


def _cand_flat():
    flat = np.full((N_CAND_ROWS, LANES), BIG_FLAT, np.float32)
    for r, cell in enumerate(_cand_layout()):
        if cell is not None:
            flat[r, :] = 16 * cell[0] + cell[1]
    return jnp.asarray(flat)


def _top16_ranked(s, row_id, top_ref):
    rank = jnp.full(s.shape, float(TOPK), F32)
    for it in range(TOPK):
        m = jnp.max(s, axis=0, keepdims=True)
        idx = jnp.min(jnp.where(s == m, row_id, float(N_KEYS)), axis=0, keepdims=True)
        sel = row_id == idx
        rank = jnp.where(sel, float(it), rank)
        s = jnp.where(sel, -jnp.inf, s)
        top_ref[it:it + 1, :] = m
    return rank


def _sort16_network():
    def merge(lo, hi, r):
        step = r * 2
        if step < hi - lo:
            yield from merge(lo, hi, step)
            yield from merge(lo + r, hi, step)
            yield from [(i, i + r) for i in range(lo + r, hi - r, step)]
        else:
            yield (lo, lo + r)

    def sort(lo, hi):
        if hi > lo:
            mid = lo + (hi - lo) // 2
            yield from sort(lo, mid)
            yield from sort(mid + 1, hi)
            yield from merge(lo, hi, 1)

    return list(sort(0, TOPK - 1))


def _top16_sorted(s, top_ref):
    n = N_KEYS // SUBLANES
    cols = [s[SUBLANES * v:SUBLANES * (v + 1), :] for v in range(n)]
    for i, j in _sort16_network():
        cols[i], cols[j] = jnp.maximum(cols[i], cols[j]), jnp.minimum(cols[i], cols[j])
    ties = jnp.zeros((1, s.shape[1]), F32)
    prev = None
    for it in range(TOPK + 1):
        m = jnp.max(cols[0], axis=0, keepdims=True)
        if prev is not None:
            ties = ties + jnp.where(m == prev, 1.0, 0.0)
        prev = m
        if it == TOPK:
            break
        sel = cols[0] == m
        ties = ties + jnp.sum(jnp.where(sel, 1.0, 0.0), axis=0, keepdims=True) - 1.0
        top_ref[it:it + 1, :] = m
        for v in range(TOPK - it):
            nxt = cols[v + 1] if v + 1 < n else jnp.full_like(cols[v], -jnp.inf)
            cols[v] = jnp.where(sel, nxt, cols[v])
    return ties


def _select_block(h, lc, toks, exact, keys_ref, qp_scr, top1_scr, top2_scr, flat, row_id, row8,
                  thr_ref, g1_ref, s2_ref, e2_ref):
    valid = flat < BIG_FLAT
    sides = []
    for p, top_scr in ((0, top1_scr), (1, top2_scr)):
        col = pl.multiple_of((2 * h + p) * N_KEYS, N_KEYS)
        s = lax.dot_general(keys_ref[p], qp_scr[toks, pl.ds(col, N_KEYS)],
                            (((1,), (1,)), ((), ())), preferred_element_type=F32)
        sides.append((s, _top16_ranked(s, row_id, top_scr) if exact else _top16_sorted(s, top_scr)))
    (s1, info1), (s2, info2) = sides
    a1 = top1_scr[...]
    a2 = top2_scr[...]
    v = jnp.concatenate([
        a1[0:1] + a2[0:16],
        a1[1:2] + a2[0:8],
        a1[8:16] + a2[0:1],
        a1[2:3] + a2[0:8],
        a1[3:4] + a2[0:8],
        a1[4:5] + a2[0:8],
        a1[0:8] + a2[0:1],
        a1[0:8] + a2[1:2]], axis=0)
    vw = jnp.where(valid, v, -jnp.inf)
    selm = jnp.zeros(v.shape, F32)
    for _ in range(TOPK):
        m = jnp.max(vw, axis=0, keepdims=True)
        sel = vw == m
        if exact:
            f = jnp.min(jnp.where(sel, flat, BIG_FLAT), axis=0, keepdims=True)
            sel = flat == f
        selm = jnp.where(sel, 1.0, selm)
        vw = jnp.where(sel, -jnp.inf, vw)
    vmax = a1[0:1] + a2[0:1]
    z = jnp.sum(jnp.where(selm > 0.0, jnp.exp(v - vmax), 0.0), axis=0, keepdims=True)
    cnt_lo = selm[56:64] + selm[64:72]
    for i, (lo, hi) in enumerate(((0, 16), (16, 24), (32, 40), (40, 48), (48, 56))):
        ci = jnp.sum(selm[lo:hi], axis=0, keepdims=True)
        cnt_lo = jnp.where(row8 == i, ci, cnt_lo)
    cnt = jnp.concatenate([cnt_lo, selm[24:32]], axis=0)
    if exact:
        thr = jnp.where(cnt > 0.0, 0.5 - cnt, NO_EXPERT)
        key1, key2 = info1, -info2
        match = [float(i) for i in range(TOPK)]
    else:
        thr = jnp.full(cnt.shape, NO_EXPERT, F32)
        for j in range(TOPK):
            thr = jnp.where(cnt == float(j + 1), a2[j:j + 1], thr)
        key1, key2 = s1, s2
        match = [a1[i:i + 1] for i in range(TOPK)]
    thr_a = jnp.full((N_KEYS, LANES), NO_EXPERT, F32)
    for i in range(TOPK):
        thr_a = jnp.where(key1 == match[i], thr[i:i + 1], thr_a)
    thr_ref[h, lc] = thr_a
    g1_ref[h, lc] = jnp.exp(s1 - a1[0:1]) / z
    s2_ref[h, lc] = key2
    e2_ref[h, lc] = jnp.exp(s2 - a2[0:1])
    if exact:
        return None
    n3 = jnp.sum(selm, axis=0, keepdims=True) - float(TOPK)
    return jnp.max(info1 + info2 + n3)


def _select_kernel(h2_ref, wq_ref, keys_ref, flat_ref, thr_ref, g1_ref, s2_ref, e2_ref,
                   qp_scr, top1_scr, top2_scr):
    tm = h2_ref.shape[0]
    qp_scr[...] = jnp.dot(h2_ref[...], wq_ref[...], preferred_element_type=F32).astype(BF16)
    row_id = lax.broadcasted_iota(jnp.int32, (N_KEYS, LANES), 0).astype(F32)
    row8 = lax.broadcasted_iota(jnp.int32, (8, LANES), 0)
    flat = flat_ref[...]
    for lc in range(tm // LANES):
        toks = slice(lc * LANES, (lc + 1) * LANES)
        block = functools.partial(
            _select_block, lc=lc, toks=toks, keys_ref=keys_ref, qp_scr=qp_scr, top1_scr=top1_scr,
            top2_scr=top2_scr, flat=flat, row_id=row_id, row8=row8, thr_ref=thr_ref, g1_ref=g1_ref,
            s2_ref=s2_ref, e2_ref=e2_ref)

        def head_body(h, carry, block=block):
            ties = block(h, exact=False)

            @pl.when(ties > 0.5)
            def _():
                block(h, exact=True)

            return carry

        lax.fori_loop(0, N_HEADS, head_body, 0)


def _select(h2, w_query, sub_keys, tm):
    t, d = h2.shape
    out_spec = pl.BlockSpec((N_HEADS, tm // LANES, N_KEYS, LANES), lambda i: (0, i, 0, 0))
    slab = lambda dt: jax.ShapeDtypeStruct((N_HEADS, t // LANES, N_KEYS, LANES), dt)
    return pl.pallas_call(
        _select_kernel,
        grid=(t // tm,),
        in_specs=[pl.BlockSpec((tm, d), lambda i: (i, 0)),
                  pl.BlockSpec(w_query.shape, lambda i: (0, 0)),
                  pl.BlockSpec(sub_keys.shape, lambda i: (0, 0, 0)),
                  pl.BlockSpec((N_CAND_ROWS, LANES), lambda i: (0, 0))],
        out_specs=[out_spec] * 4,
        out_shape=[slab(F32)] * 4,
        scratch_shapes=[pltpu.VMEM((tm, w_query.shape[1]), BF16),
                        pltpu.VMEM((TOPK, LANES), F32), pltpu.VMEM((TOPK, LANES), F32)],
        compiler_params=_params(("parallel",)),
        name="peer_select",
    )(h2, w_query, sub_keys, _cand_flat())


def _coef_kernel(thr_ref, g1_ref, s2_ref, e2_ref, *rest, cast):
    if cast:
        u_ref, v_ref, c_ref, u16_ref, vt16_ref = rest
        u16_ref[...] = u_ref[...].astype(BF16)
        step = pl.program_id(0) * pl.num_programs(1) + pl.program_id(1)

        @pl.when(step % 2 == 0)
        def _():
            vt16_ref[...] = v_ref[...].T.astype(BF16)
    else:
        (c_ref,) = rest
    n_l, n_a = thr_ref.shape[1], thr_ref.shape[2]
    for lc in range(n_l):
        for ai in range(n_a):
            rows = slice(ai * N_KEYS, (ai + 1) * N_KEYS)
            coef = jnp.zeros((N_KEYS, LANES), F32)
            for h in range(N_HEADS):
                thr = thr_ref[h, lc, ai:ai + 1, :]
                g1 = g1_ref[h, lc, ai:ai + 1, :]
                coef = coef + jnp.where(s2_ref[h, lc] >= thr, e2_ref[h, lc], 0.0) * g1
            c_ref[lc, rows, :] = coef.astype(BF16)


def _coef(sel, te, n_exp, tables=None):
    n_lc = sel[0].shape[1]
    n_l = min(n_lc, COEF_LANE_CHUNKS)
    grid = (n_lc // n_l, n_exp // te)
    a_spec = pl.BlockSpec((N_HEADS, n_l, te // N_KEYS, LANES), lambda i, j: (0, i, j, 0))
    b_spec = pl.BlockSpec((N_HEADS, n_l, N_KEYS, LANES), lambda i, j: (0, i, 0, 0))
    in_specs = [a_spec, a_spec, b_spec, b_spec]
    out_specs = [pl.BlockSpec((n_l, te, LANES), lambda i, j: (i, j, 0))]
    out_shape = [jax.ShapeDtypeStruct((n_lc, n_exp, LANES), BF16)]
    operands = list(sel)
    if tables is not None:
        d = tables[0].shape[1]
        steps = grid[0] * grid[1]
        u_rows = n_exp // steps
        assert u_rows * steps == n_exp and 2 * u_rows == LANES
        step = lambda i, j: i * grid[1] + j
        in_specs += [pl.BlockSpec((u_rows, d), lambda i, j: (step(i, j), 0)),
                     pl.BlockSpec((LANES, d), lambda i, j: (step(i, j) // 2, 0))]
        out_specs += [pl.BlockSpec((u_rows, d), lambda i, j: (step(i, j), 0)),
                      pl.BlockSpec((d, LANES), lambda i, j: (0, step(i, j) // 2))]
        out_shape += [jax.ShapeDtypeStruct((n_exp, d), BF16), jax.ShapeDtypeStruct((d, n_exp), BF16)]
        operands += list(tables)
    return pl.pallas_call(
        functools.partial(_coef_kernel, cast=tables is not None),
        grid=grid,
        in_specs=in_specs,
        out_specs=out_specs,
        out_shape=out_shape,
        compiler_params=_params(("arbitrary", "arbitrary")),
        name="peer_coef",
    )(*operands)


def _peer_kernel(h2t_ref, u_ref, vt_ref, c_ref, out_ref):
    j = pl.program_id(1)
    tt = h2t_ref.shape[1]

    @pl.when(j == 0)
    def _():
        out_ref[...] = jnp.zeros_like(out_ref)

    act = jnp.dot(u_ref[...], h2t_ref[...], preferred_element_type=F32)
    p = jnp.concatenate(
        [_gelu(act[:, lc * LANES:(lc + 1) * LANES]).astype(BF16) * c_ref[lc]
         for lc in range(tt // LANES)], axis=1)
    out_ref[...] += jnp.dot(vt_ref[...], p, preferred_element_type=F32)


def _peer(h2t, experts, sel, tt, te):
    d, t = h2t.shape
    kind, u, vt = experts
    n_exp = u.shape[0]
    if kind == "f32":
        coef, u, vt = _coef(sel, te, n_exp, tables=(u, vt))
    else:
        (coef,) = _coef(sel, te, n_exp)
    ffnt = _peer_dense(h2t, u, vt, coef, tt, te)
    return ffnt, ("bf16", u, vt)


def _peer_dense(h2t, u, vt, coef, tt, te):
    d, t = h2t.shape
    n_exp = u.shape[0]
    return pl.pallas_call(
        _peer_kernel,
        grid=(t // tt, n_exp // te),
        in_specs=[pl.BlockSpec((d, tt), lambda i, j: (0, i)),
                  pl.BlockSpec((te, d), lambda i, j: (j, 0)),
                  pl.BlockSpec((d, te), lambda i, j: (0, j)),
                  pl.BlockSpec((tt // LANES, te, LANES), lambda i, j: (i, j, 0))],
        out_specs=pl.BlockSpec((d, tt), lambda i, j: (0, i)),
        out_shape=jax.ShapeDtypeStruct((d, t), F32),
        compiler_params=_params(("parallel", "arbitrary")),
        name="peer_dense",
    )(h2t, u, vt, coef)


def _final_kernel(x1_ref, ffnt_ref, gt2_ref, gf_ref, y_ref):
    x2 = x1_ref[...] + gt2_ref[...] * ffnt_ref[...].T
    y_ref[...] = _rms(x2, gf_ref[...])


def _final(x1, ffnt, gt2, g_final, tm, tiles_per_group):
    t, d = x1.shape
    return pl.pallas_call(
        _final_kernel,
        grid=(t // tm,),
        in_specs=[pl.BlockSpec((tm, d), lambda i: (i, 0)),
                  pl.BlockSpec((d, tm), lambda i: (0, i)),
                  _mod_spec(gt2, tiles_per_group),
                  pl.BlockSpec((1, d), lambda i: (0, 0))],
        out_specs=pl.BlockSpec((tm, d), lambda i: (i, 0)),
        out_shape=jax.ShapeDtypeStruct((t, d), F32),
        compiler_params=_params(("parallel",)),
        name="final_norm",
    )(x1, ffnt, gt2, g_final)


def _trunk(x, mods, n_groups, window, att_fn, w, experts, wmix, bmix, tm_in, tm_mix, tm_sel, tt, te):
    sh1, sc1, gt1, sh2, sc2, gt2 = mods
    rows = x.shape[0] // n_groups
    q, k, v, u, vn, *wins = _inproj(x, sh1, sc1, w["g1"], w["w_in"], w["lng"], w["lnb"], tm_in,
                                    n_groups, window)
    att = att_fn(q, k, v)
    x1, h2, h2t = _mix(att, u, vn, x, gt1, sh2, sc2, w["g2"], wmix, bmix, w["w_out"], tm_mix)
    sel = _select(h2, w["w_query"], w["sub_keys"], tm_sel)
    ffnt, experts = _peer(h2t, experts, sel, tt, te)
    y = _final(x1, ffnt, gt2, w["g_final"], tm_mix, rows // tm_mix)
    return y, k, v, vn, wins, experts


def kernel(x_prompt, x_sample, cache_k, cache_v, c_prompt, c_sample, w_ada, b_ada, g_norm1, w_in,
           ln_v_g, ln_v_b, w_spatial, b_spatial, w_out, g_norm2, w_query, sub_keys, expert_u,
           expert_v, g_final):
    depth = w_ada.shape[0]
    assert depth == 1, "single-layer trunk"
    batch, seq, d = x_prompt.shape
    n_dec, n_new, _ = x_sample.shape
    buf = cache_k.shape[2]
    assert seq % SPAN == 0 and buf == SPAN and n_new <= SLOTS
    win = min(BRANCHES[-1][0], seq)
    pad_s = BAND
    assert n_dec * SLOTS <= pad_s

    n_c = batch + n_dec
    c_all = jnp.pad(jnp.concatenate([c_prompt, c_sample], axis=0), ((0, -n_c % SUBLANES), (0, 0)))
    mod = _adaln(c_all, w_ada[0], b_ada[0][None, :])
    mods = [mod[:, i * d:(i + 1) * d] for i in range(6)]
    mods_p = [m[:batch, None, :] for m in mods]
    mods_s = [jnp.pad(jnp.repeat(m[batch:n_c], SLOTS, axis=0), ((0, pad_s - n_dec * SLOTS), (0, 0)))[None]
              for m in mods]

    tri = jnp.tril(jnp.ones((BAND, BAND), dtype=bool))
    ws = jnp.where(tri, w_spatial[0], 0.0)
    bs = b_spatial[0]
    wmix_p = ws.astype(BF16)
    bmix_p = jnp.broadcast_to(bs[:, :, None], (N_HEADS, BAND, HEAD_DIM))
    eye = jnp.eye(pad_s // SLOTS, dtype=F32)
    slot_pad = ((0, 0), (0, SLOTS - n_new))
    ws_new = jnp.pad(ws[:, :n_new, :n_new], ((0, 0),) + slot_pad[1:] + slot_pad[1:])
    wmix_s = jax.vmap(lambda m: jnp.kron(eye, m))(ws_new).astype(BF16)
    bmix_s = jnp.broadcast_to(jnp.tile(jnp.pad(bs[:, :n_new], slot_pad), (1, pad_s // SLOTS))[:, :, None],
                              (N_HEADS, pad_s, HEAD_DIM))

    w = dict(g1=g_norm1[0][None, :], w_in=w_in[0].astype(BF16), lng=ln_v_g[0][None, :],
             lnb=ln_v_b[0][None, :], g2=g_norm2[0][None, :], w_out=w_out[0].astype(BF16),
             w_query=w_query[0].astype(BF16), sub_keys=sub_keys[0].astype(BF16),
             g_final=g_final[None, :])

    xp = x_prompt.reshape(batch * seq, d)
    y_p, _, _, _, (k_win, v_win), experts = _trunk(
        xp, mods_p, batch, win, _prompt_attention, w, ("f32", expert_u[0], expert_v[0]), wmix_p, bmix_p,
        tm_in=512, tm_mix=256, tm_sel=256, tt=512, te=1024)

    xs = jnp.pad(x_sample, ((0, 0), (0, SLOTS - n_new), (0, 0))).reshape(n_dec * SLOTS, d)
    xs = jnp.pad(xs, ((0, pad_s - n_dec * SLOTS), (0, 0)))

    def sample_att(q, k, v):
        o = _sample_attention(q[0], k[0], v[0], cache_k[0], cache_v[0], n_new)
        o = o.transpose(1, 0, 2, 3).reshape(N_HEADS, n_dec * SLOTS, HEAD_DIM)
        return jnp.pad(o, ((0, 0), (0, pad_s - n_dec * SLOTS), (0, 0)))[None]

    y_s, k_s, v_s, vn_s, _, _ = _trunk(
        xs, mods_s, 1, 0, sample_att, w, experts, wmix_s, bmix_s,
        tm_in=pad_s, tm_mix=pad_s, tm_sel=pad_s, tt=pad_s, te=1024)

    new_rows = lambda t: t[:n_dec * SLOTS].reshape(n_dec, SLOTS, -1)[:, :n_new]
    new_heads = lambda t: t[0, :, :n_dec * SLOTS].reshape(N_HEADS, n_dec, SLOTS, HEAD_DIM)[:, :, :n_new
                                                                                         ].transpose(1, 2, 0, 3)
    return (y_p.reshape(batch, seq, d),
            new_rows(y_s),
            k_win.reshape(1, batch, win, N_HEADS, HEAD_DIM),
            v_win.reshape(1, batch, win, N_HEADS, HEAD_DIM),
            new_heads(k_s)[None],
            new_heads(v_s)[None],
            new_rows(vn_s)[None])
```

```python
import functools

import numpy as np
import jax
import jax.numpy as jnp
from jax import lax
from jax.experimental import pallas as pl
from jax.experimental.pallas import tpu as pltpu

F32 = jnp.float32
BF16 = jnp.bfloat16

EPS = 1e-6
LANES = 128
SUBLANES = 8
HEAD_DIM = 128
N_HEADS = 8
ATT_WIDTH = N_HEADS * HEAD_DIM
BRANCHES = ((128, 1), (512, 4), (2048, 16))
BAND = 128
SPAN = BAND * BRANCHES[-1][1]
N_KEYS = 128
TOPK = 16
NEG = -1e30
VMEM_LIMIT = 56 * 1024 * 1024
COEF_LANE_CHUNKS = 4
SLOTS = SUBLANES


def _params(sem, flags=None):
    return pltpu.CompilerParams(dimension_semantics=sem, vmem_limit_bytes=VMEM_LIMIT, flags=flags)


def _gelu(x):
    return 0.5 * x * (1.0 + lax.erf(x * (2.0 ** -0.5)))


def _rms(x, g):
    return x * lax.rsqrt(jnp.mean(x * x, axis=-1, keepdims=True) + EPS) * g


def _slope(h):
    return 2.0 ** (-8.0 * (h + 1) / N_HEADS)


def _adaln_kernel(c_ref, w_ref, b_ref, o_ref):
    c = c_ref[...]
    a = c / (1.0 + jnp.exp(-c))
    o_ref[...] = jnp.dot(a.astype(BF16), w_ref[...].astype(BF16),
                         preferred_element_type=F32) + b_ref[...]


def _adaln(c_all, w_ada, b_ada):
    rows, d = c_all.shape
    n = w_ada.shape[1]
    tn = 1024
    return pl.pallas_call(
        _adaln_kernel,
        grid=(n // tn,),
        in_specs=[pl.BlockSpec((rows, d), lambda j: (0, 0)),
                  pl.BlockSpec((d, tn), lambda j: (0, j)),
                  pl.BlockSpec((1, tn), lambda j: (0, j))],
        out_specs=pl.BlockSpec((rows, tn), lambda j: (0, j)),
        out_shape=jax.ShapeDtypeStruct((rows, n), F32),
        compiler_params=_params(("arbitrary",)),
        name="adaln",
    )(c_all, w_ada, b_ada)


def _inproj_kernel(x_ref, sh_ref, sc_ref, g1_ref, w_ref, lng_ref, lnb_ref,
                   q_ref, k_ref, v_ref, u_ref, vn_ref, *rest, windows):
    if windows:
        kwin_ref, vwin_ref, h_scr = rest
    else:
        (h_scr,) = rest
    j = pl.program_id(1)

    @pl.when(j == 0)
    def _():
        y = _rms(x_ref[...], g1_ref[...])
        h_scr[...] = (y * (1.0 + sc_ref[...]) + sh_ref[...]).astype(BF16)

    p = jnp.dot(h_scr[...], w_ref[...], preferred_element_type=F32)

    def heads_out(ref, val, win_ref=None):
        for h in range(N_HEADS):
            piece = val[:, h * HEAD_DIM:(h + 1) * HEAD_DIM]
            ref[h] = piece
            if win_ref is not None:
                win_ref[:, h, :] = piece

    @pl.when(j == 0)
    def _():
        heads_out(q_ref, p * (HEAD_DIM ** -0.5))

    @pl.when(j == 1)
    def _():
        heads_out(k_ref, p, kwin_ref if windows else None)

    @pl.when(j == 2)
    def _():
        heads_out(v_ref, p, vwin_ref if windows else None)

    @pl.when(j == 3)
    def _():
        u_ref[...] = _gelu(p).astype(BF16)

    @pl.when(j == 4)
    def _():
        g = _gelu(p)
        mu = jnp.mean(g, axis=-1, keepdims=True)
        gc = g - mu
        var = jnp.mean(gc * gc, axis=-1, keepdims=True)
        vn_ref[...] = gc * lax.rsqrt(var + EPS) * lng_ref[...] + lnb_ref[...]


def _mod_spec(mod, tiles_per_group):
    _, r, d = mod.shape
    return pl.BlockSpec((None, r, d), lambda i, *_: (i // tiles_per_group, 0, 0))


def _inproj(x, sh1, sc1, g1, w_in, lng, lnb, tm, n_groups, window):
    t, d = x.shape
    sec = ATT_WIDTH
    n_sec = w_in.shape[1] // sec
    rows = t // n_groups
    tpg = rows // tm
    tok = lambda i, j: (i, 0)
    const = lambda i, j: (0, 0)
    out_spec = pl.BlockSpec((tm, sec), tok)
    head_spec = pl.BlockSpec((None, N_HEADS, tm, HEAD_DIM), lambda i, j: (i // tpg, 0, i % tpg, 0))
    head_shape = jax.ShapeDtypeStruct((n_groups, N_HEADS, rows, HEAD_DIM), F32)
    out_specs = [head_spec] * 3 + [out_spec] * 2
    out_shape = [head_shape] * 3 + [jax.ShapeDtypeStruct((t, sec), BF16),
                                    jax.ShapeDtypeStruct((t, sec), F32)]
    if window:
        wt = window // tm
        win_spec = pl.BlockSpec((tm, N_HEADS, HEAD_DIM), lambda i, j: (
            (i // tpg) * wt + jnp.maximum(i % tpg - (tpg - wt), 0), 0, 0))
        win_shape = jax.ShapeDtypeStruct((n_groups * window, N_HEADS, HEAD_DIM), F32)
        out_specs += [win_spec] * 2
        out_shape += [win_shape] * 2
    return pl.pallas_call(
        functools.partial(_inproj_kernel, windows=bool(window)),
        grid=(t // tm, n_sec),
        in_specs=[pl.BlockSpec((tm, d), tok),
                  _mod_spec(sh1, tpg), _mod_spec(sc1, tpg),
                  pl.BlockSpec((1, d), const),
                  pl.BlockSpec((d, sec), lambda i, j: (0, j)),
                  pl.BlockSpec((1, sec), const), pl.BlockSpec((1, sec), const)],
        out_specs=out_specs,
        out_shape=out_shape,
        scratch_shapes=[pltpu.VMEM((tm, d), BF16)],
        compiler_params=_params(("arbitrary", "arbitrary")),
        name="inproj",
    )(x, sh1, sc1, g1, w_in, lng, lnb)


def _att_bias():
    a = np.arange(BAND)[:, None]
    kk = np.arange(2 * BAND)[None, :]
    steps = BAND + a - kk
    valid = (steps >= 0) & (steps <= BAND)
    slopes = np.array([_slope(h) for h in range(N_HEADS)])
    out = []
    for _, dil in BRANCHES:
        bias = -slopes[:, None, None] * (dil * steps).astype(np.float64)[None]
        out.append(np.where(valid[None], bias, NEG))
    return jnp.asarray(np.stack(out), dtype=F32)


def _att_prompt_kernel(q_ref, kc_ref, kp_ref, vc_ref, vp_ref, bias_ref, o_ref, m_scr, l_scr, acc_scr):
    first_span = pl.program_id(2) == 0
    span = q_ref.shape[0]
    is_prev = lax.broadcasted_iota(jnp.int32, (BAND, 2 * BAND), 1) < BAND

    def rows_at(start, dil):
        return pl.ds(start, BAND) if dil == 1 else pl.ds(start, BAND, stride=dil)

    for bi, (_, dil) in enumerate(BRANCHES):
        for r in range(dil):
            for n in range(span // (BAND * dil)):
                start = r + dil * BAND * n
                rows = rows_at(start, dil)
                if n > 0:
                    prev = rows_at(start - dil * BAND, dil)
                    kp, vp = kc_ref[prev, :], vc_ref[prev, :]
                else:
                    prev = rows_at(span - dil * BAND + r, dil)
                    kp, vp = kp_ref[prev, :], vp_ref[prev, :]
                kw = jnp.concatenate([kp, kc_ref[rows, :]], axis=0).astype(BF16)
                vw = jnp.concatenate([vp, vc_ref[rows, :]], axis=0).astype(BF16)
                s = lax.dot_general(q_ref[rows, :].astype(BF16), kw, (((1,), (1,)), ((), ())),
                                    preferred_element_type=F32) + bias_ref[bi]
                if n == 0:
                    s = jnp.where(jnp.logical_and(first_span, is_prev), NEG, s)
                m_b = jnp.max(s, axis=-1, keepdims=True)
                p = jnp.exp(s - m_b)
                l_b = jnp.sum(p, axis=-1, keepdims=True)
                pv = jnp.dot(p.astype(BF16), vw, preferred_element_type=F32)
                if bi == 0:
                    m_scr[rows, :] = jnp.broadcast_to(m_b, (BAND, LANES))
                    l_scr[rows, :] = jnp.broadcast_to(l_b, (BAND, LANES))
                    acc_scr[rows, :] = pv
                else:
                    m_o = m_scr[rows, :]
                    m_n = jnp.maximum(m_o, m_b)
                    e_o = jnp.exp(m_o - m_n)
                    e_b = jnp.exp(m_b - m_n)
                    m_scr[rows, :] = m_n
                    l_scr[rows, :] = e_o * l_scr[rows, :] + e_b * l_b
                    acc_scr[rows, :] = e_o * acc_scr[rows, :] + e_b * pv
    o_ref[...] = (acc_scr[...] / l_scr[...]).astype(o_ref.dtype)


def _prompt_attention(q, k, v):
    b, nh, seq, hd = q.shape
    cur = lambda bb, h, s: (bb, h, s, 0)
    prev = lambda bb, h, s: (bb, h, jnp.maximum(s - 1, 0), 0)
    blk = lambda m: pl.BlockSpec((None, None, SPAN, hd), m)
    return pl.pallas_call(
        _att_prompt_kernel,
        grid=(b, nh, seq // SPAN),
        in_specs=[blk(cur), blk(cur), blk(prev), blk(cur), blk(prev),
                  pl.BlockSpec((len(BRANCHES), None, BAND, 2 * BAND), lambda bb, h, s: (0, h, 0, 0))],
        out_specs=blk(cur),
        out_shape=jax.ShapeDtypeStruct(q.shape, BF16),
        scratch_shapes=[pltpu.VMEM((SPAN, LANES), F32)] * 3,
        compiler_params=_params(("parallel", "parallel", "arbitrary")),
        name="attention_prompt",
    )(q, k, k, v, v, _att_bias())


def _att_sample_kernel(q_ref, kn_ref, vn_ref, kt_ref, vt_ref, kr_ref, vr_ref, o_ref, *, n_new):
    o_ref[...] = jnp.zeros_like(o_ref)
    col = lax.broadcasted_iota(jnp.int32, (BAND, 1), 0).astype(F32)
    slot = lax.broadcasted_iota(jnp.int32, (SLOTS, 1), 0)
    tail = kt_ref.shape[0] // N_HEADS
    for h in range(N_HEADS):
        slope = _slope(h)
        knh, vnh = kn_ref[h], vn_ref[h]
        for t in range(n_new):
            q_row = q_ref[h, t:t + 1, :]
            parts = []
            for _, dil in BRANCHES:
                a = t if dil == 1 else 0
                if dil == 1:
                    rows = pl.ds((tail - BAND) * N_HEADS + h, BAND, stride=N_HEADS)
                    kp, vp = kt_ref[rows, :], vt_ref[rows, :]
                elif dil * BAND == tail:
                    rows = pl.ds(t * N_HEADS + h, BAND, stride=dil * N_HEADS)
                    kp, vp = kt_ref[rows, :], vt_ref[rows, :]
                else:
                    kp, vp = kr_ref[:, t * N_HEADS + h, :], vr_ref[:, t * N_HEADS + h, :]
                steps = float(BAND + a) - col
                s_p = jnp.sum(kp * q_row, axis=1, keepdims=True) - (slope * dil) * steps
                if a > 0:
                    s_p = jnp.where(steps <= float(BAND), s_p, NEG)
                s_n = jnp.sum(knh * q_row, axis=1, keepdims=True)
                if dil == 1:
                    s_n = jnp.where(slot <= t, s_n - slope * (t - slot).astype(F32), NEG)
                else:
                    s_n = jnp.where(slot == t, s_n, NEG)
                m = jnp.maximum(jnp.max(s_p, axis=0, keepdims=True), jnp.max(s_n, axis=0, keepdims=True))
                p_p = jnp.exp(s_p - m)
                p_n = jnp.exp(s_n - m)
                l = jnp.sum(p_p, axis=0, keepdims=True) + jnp.sum(p_n, axis=0, keepdims=True)
                acc = (jnp.sum(p_p * vp, axis=0, keepdims=True)
                       + jnp.sum(p_n * vnh, axis=0, keepdims=True))
                parts.append((m, l, acc))
            m_all = functools.reduce(jnp.maximum, [m for m, _, _ in parts])
            l_all = sum(l * jnp.exp(m - m_all) for m, l, _ in parts)
            acc_all = sum(acc * jnp.exp(m - m_all) for m, _, acc in parts)
            o_ref[h, t:t + 1, :] = acc_all / l_all


def _sample_attention(q, k, v, cache_k, cache_v, n_new):
    n_batch, buf, nh, hd = cache_k.shape
    res16 = BRANCHES[-1][1]
    tail = 4 * BAND
    assert buf == res16 * BAND and n_new * nh <= 32
    rows2d = lambda c: c.reshape(n_batch, buf * nh, hd)
    resid = lambda c: c.reshape(n_batch, BAND, res16 * nh, hd)
    new_spec = pl.BlockSpec((nh, SLOTS, hd), lambda b: (0, b, 0))
    tail_spec = pl.BlockSpec((None, tail * nh, hd), lambda b: (b, buf // tail - 1, 0))
    res_spec = pl.BlockSpec((None, BAND, 32, hd), lambda b: (b, 0, 0, 0))
    return pl.pallas_call(
        functools.partial(_att_sample_kernel, n_new=n_new),
        grid=(n_batch,),
        in_specs=[new_spec, new_spec, new_spec, tail_spec, tail_spec, res_spec, res_spec],
        out_specs=pl.BlockSpec((None, nh, SLOTS, hd), lambda b: (b, 0, 0, 0)),
        out_shape=jax.ShapeDtypeStruct((n_batch, nh, SLOTS, hd), F32),
        compiler_params=_params(("parallel",)),
        name="attention_sample",
    )(q, k, v, rows2d(cache_k), rows2d(cache_v), resid(cache_k), resid(cache_v))


def _mix_kernel(att_ref, u_ref, vn_ref, x_ref, gt1_ref, sh2_ref, sc2_ref, g2_ref, wmix_ref, bmix_ref,
                wout_ref, x1_ref, h2_ref, h2t_ref, mix_scr):
    tm = x_ref.shape[0]
    for h in range(N_HEADS):
        mix_scr[:, h * HEAD_DIM:(h + 1) * HEAD_DIM] = att_ref[h].astype(BF16)
    for c in range(tm // BAND):
        rows = slice(c * BAND, (c + 1) * BAND)
        for g in range(N_HEADS):
            cs = slice(g * HEAD_DIM, (g + 1) * HEAD_DIM)
            m = jnp.dot(wmix_ref[g], vn_ref[rows, cs].astype(BF16),
                        preferred_element_type=F32) + bmix_ref[g]
            mix_scr[rows, ATT_WIDTH + g * HEAD_DIM:ATT_WIDTH + (g + 1) * HEAD_DIM] = (
                u_ref[rows, cs].astype(F32) * m).astype(BF16)
    y = jnp.dot(mix_scr[...], wout_ref[...], preferred_element_type=F32)
    x1 = x_ref[...] + gt1_ref[...] * y
    x1_ref[...] = x1
    h2 = _rms(x1, g2_ref[...]) * (1.0 + sc2_ref[...]) + sh2_ref[...]
    h2_ref[...] = h2.astype(BF16)
    h2t_ref[...] = h2.T.astype(BF16)


def _mix(att, u, vn, x, gt1, sh2, sc2, g2, wmix, bmix, w_out, tm):
    t, d = x.shape
    tpg = att.shape[2] // tm
    tok = lambda i: (i, 0)
    c2 = lambda i: (0, 0)
    c3 = lambda i: (0, 0, 0)
    return pl.pallas_call(
        _mix_kernel,
        grid=(t // tm,),
        in_specs=[pl.BlockSpec((None, N_HEADS, tm, HEAD_DIM), lambda i: (i // tpg, 0, i % tpg, 0)),
                  pl.BlockSpec((tm, ATT_WIDTH), tok), pl.BlockSpec((tm, ATT_WIDTH), tok),
                  pl.BlockSpec((tm, d), tok),
                  _mod_spec(gt1, tpg), _mod_spec(sh2, tpg), _mod_spec(sc2, tpg),
                  pl.BlockSpec((1, d), c2),
                  pl.BlockSpec(wmix.shape, c3), pl.BlockSpec(bmix.shape, c3),
                  pl.BlockSpec(w_out.shape, c2)],
        out_specs=[pl.BlockSpec((tm, d), tok), pl.BlockSpec((tm, d), tok),
                   pl.BlockSpec((d, tm), lambda i: (0, i))],
        out_shape=[jax.ShapeDtypeStruct((t, d), F32),
                   jax.ShapeDtypeStruct((t, d), BF16),
                   jax.ShapeDtypeStruct((d, t), BF16)],
        scratch_shapes=[pltpu.VMEM((tm, d), BF16)],
        compiler_params=_params(("parallel",)),
        name="mix_outproj",
    )(att, u, vn, x, gt1, sh2, sc2, g2, wmix, bmix, w_out)


def _cand_layout():
    rows = []
    rows += [(0, j) for j in range(16)]
    rows += [(1, j) for j in range(8)]
    rows += [(8 + r, 0) for r in range(8)]
    rows += [(2, j) if j < 5 else None for j in range(8)]
    rows += [(3, j) if j < 4 else None for j in range(8)]
    rows += [(4, j) if j < 3 else None for j in range(8)]
    rows += [(r, 0) if 5 <= r < 8 else None for r in range(8)]
    rows += [(r, 1) if 5 <= r < 8 else None for r in range(8)]
    return rows


N_CAND_ROWS = 72
BIG_FLAT = 1e9
NO_EXPERT = 1e30


def _cand_flat():
    flat = np.full((N_CAND_ROWS, LANES), BIG_FLAT, np.float32)
    for r, cell in enumerate(_cand_layout()):
        if cell is not None:
            flat[r, :] = 16 * cell[0] + cell[1]
    return jnp.asarray(flat)


def _top16_ranked(s, row_id, top_ref):
    rank = jnp.full(s.shape, float(TOPK), F32)
    for it in range(TOPK):
        m = jnp.max(s, axis=0, keepdims=True)
        idx = jnp.min(jnp.where(s == m, row_id, float(N_KEYS)), axis=0, keepdims=True)
        sel = row_id == idx
        rank = jnp.where(sel, float(it), rank)
        s = jnp.where(sel, -jnp.inf, s)
        top_ref[it:it + 1, :] = m
    return rank


def _sort16_network():
    def merge(lo, hi, r):
        step = r * 2
        if step < hi - lo:
            yield from merge(lo, hi, step)
            yield from merge(lo + r, hi, step)
            yield from [(i, i + r) for i in range(lo + r, hi - r, step)]
        else:
            yield (lo, lo + r)

    def sort(lo, hi):
        if hi > lo:
            mid = lo + (hi - lo) // 2
            yield from sort(lo, mid)
            yield from sort(mid + 1, hi)
            yield from merge(lo, hi, 1)

    return list(sort(0, TOPK - 1))


def _top16_sorted(s, top_ref):
    n = N_KEYS // SUBLANES
    cols = [s[SUBLANES * v:SUBLANES * (v + 1), :] for v in range(n)]
    for i, j in _sort16_network():
        cols[i], cols[j] = jnp.maximum(cols[i], cols[j]), jnp.minimum(cols[i], cols[j])
    ties = jnp.zeros((1, s.shape[1]), F32)
    prev = None
    for it in range(TOPK + 1):
        m = jnp.max(cols[0], axis=0, keepdims=True)
        if prev is not None:
            ties = ties + jnp.where(m == prev, 1.0, 0.0)
        prev = m
        if it == TOPK:
            break
        sel = cols[0] == m
        ties = ties + jnp.sum(jnp.where(sel, 1.0, 0.0), axis=0, keepdims=True) - 1.0
        top_ref[it:it + 1, :] = m
        for v in range(TOPK - it):
            nxt = cols[v + 1] if v + 1 < n else jnp.full_like(cols[v], -jnp.inf)
            cols[v] = jnp.where(sel, nxt, cols[v])
    return ties


def _select_block(h, lc, toks, exact, keys_ref, qp_scr, top1_scr, top2_scr, flat, row_id, row8,
                  thr_ref, g1_ref, s2_ref, e2_ref):
    valid = flat < BIG_FLAT
    sides = []
    for p, top_scr in ((0, top1_scr), (1, top2_scr)):
        col = pl.multiple_of((2 * h + p) * N_KEYS, N_KEYS)
        s = lax.dot_general(keys_ref[p], qp_scr[toks, pl.ds(col, N_KEYS)],
                            (((1,), (1,)), ((), ())), preferred_element_type=F32)
        sides.append((s, _top16_ranked(s, row_id, top_scr) if exact else _top16_sorted(s, top_scr)))
    (s1, info1), (s2, info2) = sides
    a1 = top1_scr[...]
    a2 = top2_scr[...]
    v = jnp.concatenate([
        a1[0:1] + a2[0:16],
        a1[1:2] + a2[0:8],
        a1[8:16] + a2[0:1],
        a1[2:3] + a2[0:8],
        a1[3:4] + a2[0:8],
        a1[4:5] + a2[0:8],
        a1[0:8] + a2[0:1],
        a1[0:8] + a2[1:2]], axis=0)
    vw = jnp.where(valid, v, -jnp.inf)
    selm = jnp.zeros(v.shape, F32)
    for _ in range(TOPK):
        m = jnp.max(vw, axis=0, keepdims=True)
        sel = vw == m
        if exact:
            f = jnp.min(jnp.where(sel, flat, BIG_FLAT), axis=0, keepdims=True)
            sel = flat == f
        selm = jnp.where(sel, 1.0, selm)
        vw = jnp.where(sel, -jnp.inf, vw)
    vmax = a1[0:1] + a2[0:1]
    z = jnp.sum(jnp.where(selm > 0.0, jnp.exp(v - vmax), 0.0), axis=0, keepdims=True)
    cnt_lo = selm[56:64] + selm[64:72]
    for i, (lo, hi) in enumerate(((0, 16), (16, 24), (32, 40), (40, 48), (48, 56))):
        ci = jnp.sum(selm[lo:hi], axis=0, keepdims=True)
        cnt_lo = jnp.where(row8 == i, ci, cnt_lo)
    cnt = jnp.concatenate([cnt_lo, selm[24:32]], axis=0)
    if exact:
        thr = jnp.where(cnt > 0.0, 0.5 - cnt, NO_EXPERT)
        key1, key2 = info1, -info2
        match = [float(i) for i in range(TOPK)]
    else:
        thr = jnp.full(cnt.shape, NO_EXPERT, F32)
        for j in range(TOPK):
            thr = jnp.where(cnt == float(j + 1), a2[j:j + 1], thr)
        key1, key2 = s1, s2
        match = [a1[i:i + 1] for i in range(TOPK)]
    thr_a = jnp.full((N_KEYS, LANES), NO_EXPERT, F32)
    for i in range(TOPK):
        thr_a = jnp.where(key1 == match[i], thr[i:i + 1], thr_a)
    thr_ref[h, lc] = thr_a
    g1_ref[h, lc] = jnp.exp(s1 - a1[0:1]) / z
    s2_ref[h, lc] = key2
    e2_ref[h, lc] = jnp.exp(s2 - a2[0:1])
    if exact:
        return None
    n3 = jnp.sum(selm, axis=0, keepdims=True) - float(TOPK)
    return jnp.max(info1 + info2 + n3)


def _select_kernel(h2_ref, wq_ref, keys_ref, flat_ref, thr_ref, g1_ref, s2_ref, e2_ref,
                   qp_scr, top1_scr, top2_scr):
    tm = h2_ref.shape[0]
    qp_scr[...] = jnp.dot(h2_ref[...], wq_ref[...], preferred_element_type=F32).astype(BF16)
    row_id = lax.broadcasted_iota(jnp.int32, (N_KEYS, LANES), 0).astype(F32)
    row8 = lax.broadcasted_iota(jnp.int32, (8, LANES), 0)
    flat = flat_ref[...]
    for lc in range(tm // LANES):
        toks = slice(lc * LANES, (lc + 1) * LANES)
        block = functools.partial(
            _select_block, lc=lc, toks=toks, keys_ref=keys_ref, qp_scr=qp_scr, top1_scr=top1_scr,
            top2_scr=top2_scr, flat=flat, row_id=row_id, row8=row8, thr_ref=thr_ref, g1_ref=g1_ref,
            s2_ref=s2_ref, e2_ref=e2_ref)

        def head_body(h, carry, block=block):
            ties = block(h, exact=False)

            @pl.when(ties > 0.5)
            def _():
                block(h, exact=True)

            return carry

        lax.fori_loop(0, N_HEADS, head_body, 0)


def _select(h2, w_query, sub_keys, tm):
    t, d = h2.shape
    out_spec = pl.BlockSpec((N_HEADS, tm // LANES, N_KEYS, LANES), lambda i: (0, i, 0, 0))
    slab = lambda dt: jax.ShapeDtypeStruct((N_HEADS, t // LANES, N_KEYS, LANES), dt)
    return pl.pallas_call(
        _select_kernel,
        grid=(t // tm,),
        in_specs=[pl.BlockSpec((tm, d), lambda i: (i, 0)),
                  pl.BlockSpec(w_query.shape, lambda i: (0, 0)),
                  pl.BlockSpec(sub_keys.shape, lambda i: (0, 0, 0)),
                  pl.BlockSpec((N_CAND_ROWS, LANES), lambda i: (0, 0))],
        out_specs=[out_spec] * 4,
        out_shape=[slab(F32)] * 4,
        scratch_shapes=[pltpu.VMEM((tm, w_query.shape[1]), BF16),
                        pltpu.VMEM((TOPK, LANES), F32), pltpu.VMEM((TOPK, LANES), F32)],
        compiler_params=_params(("parallel",)),
        name="peer_select",
    )(h2, w_query, sub_keys, _cand_flat())


def _coef_kernel(thr_ref, g1_ref, s2_ref, e2_ref, *rest, cast):
    if cast:
        u_ref, v_ref, c_ref, u16_ref, vt16_ref = rest
        u16_ref[...] = u_ref[...].astype(BF16)
        step = pl.program_id(0) * pl.num_programs(1) + pl.program_id(1)

        @pl.when(step % 2 == 0)
        def _():
            vt16_ref[...] = v_ref[...].T.astype(BF16)
    else:
        (c_ref,) = rest
    n_l, n_a = thr_ref.shape[1], thr_ref.shape[2]
    for lc in range(n_l):
        for ai in range(n_a):
            rows = slice(ai * N_KEYS, (ai + 1) * N_KEYS)
            coef = jnp.zeros((N_KEYS, LANES), F32)
            for h in range(N_HEADS):
                thr = thr_ref[h, lc, ai:ai + 1, :]
                g1 = g1_ref[h, lc, ai:ai + 1, :]
                coef = coef + jnp.where(s2_ref[h, lc] >= thr, e2_ref[h, lc], 0.0) * g1
            c_ref[lc, rows, :] = coef.astype(BF16)


def _coef(sel, te, n_exp, tables=None):
    n_lc = sel[0].shape[1]
    n_l = min(n_lc, COEF_LANE_CHUNKS)
    grid = (n_lc // n_l, n_exp // te)
    a_spec = pl.BlockSpec((N_HEADS, n_l, te // N_KEYS, LANES), lambda i, j: (0, i, j, 0))
    b_spec = pl.BlockSpec((N_HEADS, n_l, N_KEYS, LANES), lambda i, j: (0, i, 0, 0))
    in_specs = [a_spec, a_spec, b_spec, b_spec]
    out_specs = [pl.BlockSpec((n_l, te, LANES), lambda i, j: (i, j, 0))]
    out_shape = [jax.ShapeDtypeStruct((n_lc, n_exp, LANES), BF16)]
    operands = list(sel)
    if tables is not None:
        d = tables[0].shape[1]
        steps = grid[0] * grid[1]
        u_rows = n_exp // steps
        assert u_rows * steps == n_exp and 2 * u_rows == LANES
        step = lambda i, j: i * grid[1] + j
        in_specs += [pl.BlockSpec((u_rows, d), lambda i, j: (step(i, j), 0)),
                     pl.BlockSpec((LANES, d), lambda i, j: (step(i, j) // 2, 0))]
        out_specs += [pl.BlockSpec((u_rows, d), lambda i, j: (step(i, j), 0)),
                      pl.BlockSpec((d, LANES), lambda i, j: (0, step(i, j) // 2))]
        out_shape += [jax.ShapeDtypeStruct((n_exp, d), BF16), jax.ShapeDtypeStruct((d, n_exp), BF16)]
        operands += list(tables)
    return pl.pallas_call(
        functools.partial(_coef_kernel, cast=tables is not None),
        grid=grid,
        in_specs=in_specs,
        out_specs=out_specs,
        out_shape=out_shape,
        compiler_params=_params(("arbitrary", "arbitrary")),
        name="peer_coef",
    )(*operands)


def _peer_kernel(h2t_ref, u_ref, vt_ref, c_ref, x1_ref, gt2_ref, gf_ref, y_ref, acc_scr):
    j = pl.program_id(1)
    tt = h2t_ref.shape[1]

    @pl.when(j == 0)
    def _():
        acc_scr[...] = jnp.zeros_like(acc_scr)

    act = jnp.dot(u_ref[...], h2t_ref[...], preferred_element_type=F32)
    p = jnp.concatenate(
        [_gelu(act[:, lc * LANES:(lc + 1) * LANES]).astype(BF16) * c_ref[lc]
         for lc in range(tt // LANES)], axis=1)
    acc_scr[...] += jnp.dot(vt_ref[...], p, preferred_element_type=F32)

    @pl.when(j == pl.num_programs(1) - 1)
    def _():
        x2 = x1_ref[...] + gt2_ref[...] * acc_scr[...].T
        y_ref[...] = _rms(x2, gf_ref[...])


def _peer(h2t, experts, sel, x1, gt2, g_final, rows_per_group, tt, te):
    kind, u, vt = experts
    n_exp = u.shape[0]
    if kind == "f32":
        coef, u, vt = _coef(sel, te, n_exp, tables=(u, vt))
    else:
        (coef,) = _coef(sel, te, n_exp)
    d, t = h2t.shape
    y = pl.pallas_call(
        _peer_kernel,
        grid=(t // tt, n_exp // te),
        in_specs=[pl.BlockSpec((d, tt), lambda i, j: (0, i)),
                  pl.BlockSpec((te, d), lambda i, j: (j, 0)),
                  pl.BlockSpec((d, te), lambda i, j: (0, j)),
                  pl.BlockSpec((tt // LANES, te, LANES), lambda i, j: (i, j, 0)),
                  pl.BlockSpec((tt, d), lambda i, j: (i, 0)),
                  _mod_spec(gt2, rows_per_group // tt),
                  pl.BlockSpec((1, d), lambda i, j: (0, 0))],
        out_specs=pl.BlockSpec((tt, d), lambda i, j: (i, 0)),
        out_shape=jax.ShapeDtypeStruct((t, d), F32),
        scratch_shapes=[pltpu.VMEM((d, tt), F32)],
        compiler_params=_params(("parallel", "arbitrary")),
        name="peer_dense",
    )(h2t, u, vt, coef, x1, gt2, g_final)
    return y, ("bf16", u, vt)


def _trunk(x, mods, n_groups, window, att_fn, w, experts, wmix, bmix, tm_in, tm_mix, tm_sel, tt, te):
    sh1, sc1, gt1, sh2, sc2, gt2 = mods
    rows = x.shape[0] // n_groups
    q, k, v, u, vn, *wins = _inproj(x, sh1, sc1, w["g1"], w["w_in"], w["lng"], w["lnb"], tm_in,
                                    n_groups, window)
    att = att_fn(q, k, v)
    x1, h2, h2t = _mix(att, u, vn, x, gt1, sh2, sc2, w["g2"], wmix, bmix, w["w_out"], tm_mix)
    sel = _select(h2, w["w_query"], w["sub_keys"], tm_sel)
    y, experts = _peer(h2t, experts, sel, x1, gt2, w["g_final"], rows, tt, te)
    return y, k, v, vn, wins, experts


def kernel(x_prompt, x_sample, cache_k, cache_v, c_prompt, c_sample, w_ada, b_ada, g_norm1, w_in,
           ln_v_g, ln_v_b, w_spatial, b_spatial, w_out, g_norm2, w_query, sub_keys, expert_u,
           expert_v, g_final):
    depth = w_ada.shape[0]
    assert depth == 1, "single-layer trunk"
    batch, seq, d = x_prompt.shape
    n_dec, n_new, _ = x_sample.shape
    buf = cache_k.shape[2]
    assert seq % SPAN == 0 and buf == SPAN and n_new <= SLOTS
    win = min(BRANCHES[-1][0], seq)
    pad_s = BAND
    assert n_dec * SLOTS <= pad_s

    n_c = batch + n_dec
    c_all = jnp.pad(jnp.concatenate([c_prompt, c_sample], axis=0), ((0, -n_c % SUBLANES), (0, 0)))
    mod = _adaln(c_all, w_ada[0], b_ada[0][None, :])
    mods = [mod[:, i * d:(i + 1) * d] for i in range(6)]
    mods_p = [m[:batch, None, :] for m in mods]
    mods_s = [jnp.pad(jnp.repeat(m[batch:n_c], SLOTS, axis=0), ((0, pad_s - n_dec * SLOTS), (0, 0)))[None]
              for m in mods]

    tri = jnp.tril(jnp.ones((BAND, BAND), dtype=bool))
    ws = jnp.where(tri, w_spatial[0], 0.0)
    bs = b_spatial[0]
    wmix_p = ws.astype(BF16)
    bmix_p = jnp.broadcast_to(bs[:, :, None], (N_HEADS, BAND, HEAD_DIM))
    eye = jnp.eye(pad_s // SLOTS, dtype=F32)
    slot_pad = ((0, 0), (0, SLOTS - n_new))
    ws_new = jnp.pad(ws[:, :n_new, :n_new], ((0, 0),) + slot_pad[1:] + slot_pad[1:])
    wmix_s = jax.vmap(lambda m: jnp.kron(eye, m))(ws_new).astype(BF16)
    bmix_s = jnp.broadcast_to(jnp.tile(jnp.pad(bs[:, :n_new], slot_pad), (1, pad_s // SLOTS))[:, :, None],
                              (N_HEADS, pad_s, HEAD_DIM))

    w = dict(g1=g_norm1[0][None, :], w_in=w_in[0].astype(BF16), lng=ln_v_g[0][None, :],
             lnb=ln_v_b[0][None, :], g2=g_norm2[0][None, :], w_out=w_out[0].astype(BF16),
             w_query=w_query[0].astype(BF16), sub_keys=sub_keys[0].astype(BF16),
             g_final=g_final[None, :])

    xp = x_prompt.reshape(batch * seq, d)
    y_p, _, _, _, (k_win, v_win), experts = _trunk(
        xp, mods_p, batch, win, _prompt_attention, w, ("f32", expert_u[0], expert_v[0]), wmix_p, bmix_p,
        tm_in=512, tm_mix=256, tm_sel=256, tt=512, te=1024)

    xs = jnp.pad(x_sample, ((0, 0), (0, SLOTS - n_new), (0, 0))).reshape(n_dec * SLOTS, d)
    xs = jnp.pad(xs, ((0, pad_s - n_dec * SLOTS), (0, 0)))

    def sample_att(q, k, v):
        o = _sample_attention(q[0], k[0], v[0], cache_k[0], cache_v[0], n_new)
        o = o.transpose(1, 0, 2, 3).reshape(N_HEADS, n_dec * SLOTS, HEAD_DIM)
        return jnp.pad(o, ((0, 0), (0, pad_s - n_dec * SLOTS), (0, 0)))[None]

    y_s, k_s, v_s, vn_s, _, _ = _trunk(
        xs, mods_s, 1, 0, sample_att, w, experts, wmix_s, bmix_s,
        tm_in=pad_s, tm_mix=pad_s, tm_sel=pad_s, tt=pad_s, te=1024)

    new_rows = lambda t: t[:n_dec * SLOTS].reshape(n_dec, SLOTS, -1)[:, :n_new]
    new_heads = lambda t: t[0, :, :n_dec * SLOTS].reshape(N_HEADS, n_dec, SLOTS, HEAD_DIM)[:, :, :n_new
                                                                                         ].transpose(1, 2, 0, 3)
    return (y_p.reshape(batch, seq, d),
            new_rows(y_s),
            k_win.reshape(1, batch, win, N_HEADS, HEAD_DIM),
            v_win.reshape(1, batch, win, N_HEADS, HEAD_DIM),
            new_heads(k_s)[None],
            new_heads(v_s)[None],
            new_rows(vn_s)[None])
```

```python
import functools

import numpy as np
import jax
import jax.numpy as jnp
from jax import lax
from jax.experimental import pallas as pl
from jax.experimental.pallas import tpu as pltpu

F32 = jnp.float32
BF16 = jnp.bfloat16

EPS = 1e-6
LANES = 128
SUBLANES = 8
HEAD_DIM = 128
N_HEADS = 8
ATT_WIDTH = N_HEADS * HEAD_DIM
BRANCHES = ((128, 1), (512, 4), (2048, 16))
BAND = 128
SPAN = BAND * BRANCHES[-1][1]
N_KEYS = 128
TOPK = 16
NEG = -1e30
VMEM_LIMIT = 56 * 1024 * 1024
COEF_LANE_CHUNKS = 4
SLOTS = SUBLANES


def _params(sem, flags=None):
    return pltpu.CompilerParams(dimension_semantics=sem, vmem_limit_bytes=VMEM_LIMIT, flags=flags)


def _gelu(x):
    return 0.5 * x * (1.0 + lax.erf(x * (2.0 ** -0.5)))


def _rms(x, g):
    return x * lax.rsqrt(jnp.mean(x * x, axis=-1, keepdims=True) + EPS) * g


def _slope(h):
    return 2.0 ** (-8.0 * (h + 1) / N_HEADS)


def _adaln_kernel(c_ref, w_ref, b_ref, o_ref):
    c = c_ref[...]
    a = c / (1.0 + jnp.exp(-c))
    o_ref[...] = jnp.dot(a.astype(BF16), w_ref[...].astype(BF16),
                         preferred_element_type=F32) + b_ref[...]


def _adaln(c_all, w_ada, b_ada):
    rows, d = c_all.shape
    n = w_ada.shape[1]
    tn = 1024
    return pl.pallas_call(
        _adaln_kernel,
        grid=(n // tn,),
        in_specs=[pl.BlockSpec((rows, d), lambda j: (0, 0)),
                  pl.BlockSpec((d, tn), lambda j: (0, j)),
                  pl.BlockSpec((1, tn), lambda j: (0, j))],
        out_specs=pl.BlockSpec((rows, tn), lambda j: (0, j)),
        out_shape=jax.ShapeDtypeStruct((rows, n), F32),
        compiler_params=_params(("arbitrary",)),
        name="adaln",
    )(c_all, w_ada, b_ada)


def _inproj_kernel(x_ref, sh_ref, sc_ref, g1_ref, w_ref, lng_ref, lnb_ref,
                   q_ref, k_ref, v_ref, u_ref, vn_ref, *rest, windows):
    if windows:
        kwin_ref, vwin_ref, h_scr = rest
    else:
        (h_scr,) = rest
    j = pl.program_id(1)

    @pl.when(j == 0)
    def _():
        y = _rms(x_ref[...], g1_ref[...])
        h_scr[...] = (y * (1.0 + sc_ref[...]) + sh_ref[...]).astype(BF16)

    p = jnp.dot(h_scr[...], w_ref[...], preferred_element_type=F32)

    def heads_out(ref, val, win_ref=None):
        for h in range(N_HEADS):
            piece = val[:, h * HEAD_DIM:(h + 1) * HEAD_DIM]
            ref[h] = piece
            if win_ref is not None:
                win_ref[:, h, :] = piece

    @pl.when(j == 0)
    def _():
        heads_out(q_ref, p * (HEAD_DIM ** -0.5))

    @pl.when(j == 1)
    def _():
        heads_out(k_ref, p, kwin_ref if windows else None)

    @pl.when(j == 2)
    def _():
        heads_out(v_ref, p, vwin_ref if windows else None)

    @pl.when(j == 3)
    def _():
        u_ref[...] = _gelu(p).astype(BF16)

    @pl.when(j == 4)
    def _():
        g = _gelu(p)
        mu = jnp.mean(g, axis=-1, keepdims=True)
        gc = g - mu
        var = jnp.mean(gc * gc, axis=-1, keepdims=True)
        vn_ref[...] = gc * lax.rsqrt(var + EPS) * lng_ref[...] + lnb_ref[...]


def _mod_spec(mod, tiles_per_group):
    _, r, d = mod.shape
    return pl.BlockSpec((None, r, d), lambda i, *_: (i // tiles_per_group, 0, 0))


def _inproj(x, sh1, sc1, g1, w_in, lng, lnb, tm, n_groups, window):
    t, d = x.shape
    sec = ATT_WIDTH
    n_sec = w_in.shape[1] // sec
    rows = t // n_groups
    tpg = rows // tm
    tok = lambda i, j: (i, 0)
    const = lambda i, j: (0, 0)
    out_spec = pl.BlockSpec((tm, sec), tok)
    head_spec = pl.BlockSpec((None, N_HEADS, tm, HEAD_DIM), lambda i, j: (i // tpg, 0, i % tpg, 0))
    head_shape = jax.ShapeDtypeStruct((n_groups, N_HEADS, rows, HEAD_DIM), F32)
    out_specs = [head_spec] * 3 + [out_spec] * 2
    out_shape = [head_shape] * 3 + [jax.ShapeDtypeStruct((t, sec), BF16),
                                    jax.ShapeDtypeStruct((t, sec), F32)]
    if window:
        wt = window // tm
        win_spec = pl.BlockSpec((tm, N_HEADS, HEAD_DIM), lambda i, j: (
            (i // tpg) * wt + jnp.maximum(i % tpg - (tpg - wt), 0), 0, 0))
        win_shape = jax.ShapeDtypeStruct((n_groups * window, N_HEADS, HEAD_DIM), F32)
        out_specs += [win_spec] * 2
        out_shape += [win_shape] * 2
    return pl.pallas_call(
        functools.partial(_inproj_kernel, windows=bool(window)),
        grid=(t // tm, n_sec),
        in_specs=[pl.BlockSpec((tm, d), tok),
                  _mod_spec(sh1, tpg), _mod_spec(sc1, tpg),
                  pl.BlockSpec((1, d), const),
                  pl.BlockSpec((d, sec), lambda i, j: (0, j)),
                  pl.BlockSpec((1, sec), const), pl.BlockSpec((1, sec), const)],
        out_specs=out_specs,
        out_shape=out_shape,
        scratch_shapes=[pltpu.VMEM((tm, d), BF16)],
        compiler_params=_params(("arbitrary", "arbitrary")),
        name="inproj",
    )(x, sh1, sc1, g1, w_in, lng, lnb)


def _att_bias():
    a = np.arange(BAND)[:, None]
    kk = np.arange(2 * BAND)[None, :]
    steps = BAND + a - kk
    valid = (steps >= 0) & (steps <= BAND)
    slopes = np.array([_slope(h) for h in range(N_HEADS)])
    out = []
    for _, dil in BRANCHES:
        bias = -slopes[:, None, None] * (dil * steps).astype(np.float64)[None]
        out.append(np.where(valid[None], bias, NEG))
    return jnp.asarray(np.stack(out), dtype=F32)


def _att_prompt_kernel(q_ref, kc_ref, kp_ref, vc_ref, vp_ref, bias_ref, o_ref, m_scr, l_scr, acc_scr):
    first_span = pl.program_id(2) == 0
    span = q_ref.shape[0]
    is_prev = lax.broadcasted_iota(jnp.int32, (BAND, 2 * BAND), 1) < BAND

    def rows_at(start, dil):
        return pl.ds(start, BAND) if dil == 1 else pl.ds(start, BAND, stride=dil)

    for bi, (_, dil) in enumerate(BRANCHES):
        for r in range(dil):
            for n in range(span // (BAND * dil)):
                start = r + dil * BAND * n
                rows = rows_at(start, dil)
                if n > 0:
                    prev = rows_at(start - dil * BAND, dil)
                    kp, vp = kc_ref[prev, :], vc_ref[prev, :]
                else:
                    prev = rows_at(span - dil * BAND + r, dil)
                    kp, vp = kp_ref[prev, :], vp_ref[prev, :]
                kw = jnp.concatenate([kp, kc_ref[rows, :]], axis=0).astype(BF16)
                vw = jnp.concatenate([vp, vc_ref[rows, :]], axis=0).astype(BF16)
                s = lax.dot_general(q_ref[rows, :].astype(BF16), kw, (((1,), (1,)), ((), ())),
                                    preferred_element_type=F32) + bias_ref[bi]
                if n == 0:
                    s = jnp.where(jnp.logical_and(first_span, is_prev), NEG, s)
                m_b = jnp.max(s, axis=-1, keepdims=True)
                p = jnp.exp(s - m_b)
                l_b = jnp.sum(p, axis=-1, keepdims=True)
                pv = jnp.dot(p.astype(BF16), vw, preferred_element_type=F32)
                if bi == 0:
                    m_scr[rows, :] = jnp.broadcast_to(m_b, (BAND, LANES))
                    l_scr[rows, :] = jnp.broadcast_to(l_b, (BAND, LANES))
                    acc_scr[rows, :] = pv
                else:
                    m_o = m_scr[rows, :]
                    m_n = jnp.maximum(m_o, m_b)
                    e_o = jnp.exp(m_o - m_n)
                    e_b = jnp.exp(m_b - m_n)
                    m_scr[rows, :] = m_n
                    l_scr[rows, :] = e_o * l_scr[rows, :] + e_b * l_b
                    acc_scr[rows, :] = e_o * acc_scr[rows, :] + e_b * pv
    o_ref[...] = (acc_scr[...] / l_scr[...]).astype(o_ref.dtype)


def _prompt_attention(q, k, v):
    b, nh, seq, hd = q.shape
    cur = lambda bb, h, s: (bb, h, s, 0)
    prev = lambda bb, h, s: (bb, h, jnp.maximum(s - 1, 0), 0)
    blk = lambda m: pl.BlockSpec((None, None, SPAN, hd), m)
    return pl.pallas_call(
        _att_prompt_kernel,
        grid=(b, nh, seq // SPAN),
        in_specs=[blk(cur), blk(cur), blk(prev), blk(cur), blk(prev),
                  pl.BlockSpec((len(BRANCHES), None, BAND, 2 * BAND), lambda bb, h, s: (0, h, 0, 0))],
        out_specs=blk(cur),
        out_shape=jax.ShapeDtypeStruct(q.shape, BF16),
        scratch_shapes=[pltpu.VMEM((SPAN, LANES), F32)] * 3,
        compiler_params=_params(("parallel", "parallel", "arbitrary")),
        name="attention_prompt",
    )(q, k, k, v, v, _att_bias())


def _att_sample_kernel(q_ref, kn_ref, vn_ref, kt_ref, vt_ref, kr_ref, vr_ref, o_ref, *, n_new):
    o_ref[...] = jnp.zeros_like(o_ref)
    col = lax.broadcasted_iota(jnp.int32, (BAND, 1), 0).astype(F32)
    slot = lax.broadcasted_iota(jnp.int32, (SLOTS, 1), 0)
    tail = kt_ref.shape[0] // N_HEADS
    for h in range(N_HEADS):
        slope = _slope(h)
        knh, vnh = kn_ref[h], vn_ref[h]
        for t in range(n_new):
            q_row = q_ref[h, t:t + 1, :]
            parts = []
            for _, dil in BRANCHES:
                a = t if dil == 1 else 0
                if dil == 1:
                    rows = pl.ds((tail - BAND) * N_HEADS + h, BAND, stride=N_HEADS)
                    kp, vp = kt_ref[rows, :], vt_ref[rows, :]
                elif dil * BAND == tail:
                    rows = pl.ds(t * N_HEADS + h, BAND, stride=dil * N_HEADS)
                    kp, vp = kt_ref[rows, :], vt_ref[rows, :]
                else:
                    kp, vp = kr_ref[:, t * N_HEADS + h, :], vr_ref[:, t * N_HEADS + h, :]
                steps = float(BAND + a) - col
                s_p = jnp.sum(kp * q_row, axis=1, keepdims=True) - (slope * dil) * steps
                if a > 0:
                    s_p = jnp.where(steps <= float(BAND), s_p, NEG)
                s_n = jnp.sum(knh * q_row, axis=1, keepdims=True)
                if dil == 1:
                    s_n = jnp.where(slot <= t, s_n - slope * (t - slot).astype(F32), NEG)
                else:
                    s_n = jnp.where(slot == t, s_n, NEG)
                m = jnp.maximum(jnp.max(s_p, axis=0, keepdims=True), jnp.max(s_n, axis=0, keepdims=True))
                p_p = jnp.exp(s_p - m)
                p_n = jnp.exp(s_n - m)
                l = jnp.sum(p_p, axis=0, keepdims=True) + jnp.sum(p_n, axis=0, keepdims=True)
                acc = (jnp.sum(p_p * vp, axis=0, keepdims=True)
                       + jnp.sum(p_n * vnh, axis=0, keepdims=True))
                parts.append((m, l, acc))
            m_all = functools.reduce(jnp.maximum, [m for m, _, _ in parts])
            l_all = sum(l * jnp.exp(m - m_all) for m, l, _ in parts)
            acc_all = sum(acc * jnp.exp(m - m_all) for m, _, acc in parts)
            o_ref[h, t:t + 1, :] = acc_all / l_all


def _sample_attention(q, k, v, cache_k, cache_v, n_new):
    n_batch, buf, nh, hd = cache_k.shape
    res16 = BRANCHES[-1][1]
    tail = 4 * BAND
    assert buf == res16 * BAND and n_new * nh <= 32
    rows2d = lambda c: c.reshape(n_batch, buf * nh, hd)
    resid = lambda c: c.reshape(n_batch, BAND, res16 * nh, hd)
    new_spec = pl.BlockSpec((nh, SLOTS, hd), lambda b: (0, b, 0))
    tail_spec = pl.BlockSpec((None, tail * nh, hd), lambda b: (b, buf // tail - 1, 0))
    res_spec = pl.BlockSpec((None, BAND, 32, hd), lambda b: (b, 0, 0, 0))
    return pl.pallas_call(
        functools.partial(_att_sample_kernel, n_new=n_new),
        grid=(n_batch,),
        in_specs=[new_spec, new_spec, new_spec, tail_spec, tail_spec, res_spec, res_spec],
        out_specs=pl.BlockSpec((None, nh, SLOTS, hd), lambda b: (b, 0, 0, 0)),
        out_shape=jax.ShapeDtypeStruct((n_batch, nh, SLOTS, hd), F32),
        compiler_params=_params(("parallel",)),
        name="attention_sample",
    )(q, k, v, rows2d(cache_k), rows2d(cache_v), resid(cache_k), resid(cache_v))


def _mix_kernel(att_ref, u_ref, vn_ref, x_ref, gt1_ref, sh2_ref, sc2_ref, g2_ref, wmix_ref, bmix_ref,
                wout_ref, x1_ref, h2_ref, h2t_ref, mix_scr):
    tm = x_ref.shape[0]
    for h in range(N_HEADS):
        mix_scr[:, h * HEAD_DIM:(h + 1) * HEAD_DIM] = att_ref[h].astype(BF16)
    for c in range(tm // BAND):
        rows = slice(c * BAND, (c + 1) * BAND)
        for g in range(N_HEADS):
            cs = slice(g * HEAD_DIM, (g + 1) * HEAD_DIM)
            m = jnp.dot(wmix_ref[g], vn_ref[rows, cs].astype(BF16),
                        preferred_element_type=F32) + bmix_ref[g]
            mix_scr[rows, ATT_WIDTH + g * HEAD_DIM:ATT_WIDTH + (g + 1) * HEAD_DIM] = (
                u_ref[rows, cs].astype(F32) * m).astype(BF16)
    y = jnp.dot(mix_scr[...], wout_ref[...], preferred_element_type=F32)
    x1 = x_ref[...] + gt1_ref[...] * y
    x1_ref[...] = x1
    h2 = _rms(x1, g2_ref[...]) * (1.0 + sc2_ref[...]) + sh2_ref[...]
    h2_ref[...] = h2.astype(BF16)
    h2t_ref[...] = h2.T.astype(BF16)


def _mix(att, u, vn, x, gt1, sh2, sc2, g2, wmix, bmix, w_out, tm):
    t, d = x.shape
    tpg = att.shape[2] // tm
    tok = lambda i: (i, 0)
    c2 = lambda i: (0, 0)
    c3 = lambda i: (0, 0, 0)
    return pl.pallas_call(
        _mix_kernel,
        grid=(t // tm,),
        in_specs=[pl.BlockSpec((None, N_HEADS, tm, HEAD_DIM), lambda i: (i // tpg, 0, i % tpg, 0)),
                  pl.BlockSpec((tm, ATT_WIDTH), tok), pl.BlockSpec((tm, ATT_WIDTH), tok),
                  pl.BlockSpec((tm, d), tok),
                  _mod_spec(gt1, tpg), _mod_spec(sh2, tpg), _mod_spec(sc2, tpg),
                  pl.BlockSpec((1, d), c2),
                  pl.BlockSpec(wmix.shape, c3), pl.BlockSpec(bmix.shape, c3),
                  pl.BlockSpec(w_out.shape, c2)],
        out_specs=[pl.BlockSpec((tm, d), tok), pl.BlockSpec((tm, d), tok),
                   pl.BlockSpec((d, tm), lambda i: (0, i))],
        out_shape=[jax.ShapeDtypeStruct((t, d), F32),
                   jax.ShapeDtypeStruct((t, d), BF16),
                   jax.ShapeDtypeStruct((d, t), BF16)],
        scratch_shapes=[pltpu.VMEM((tm, d), BF16)],
        compiler_params=_params(("parallel",)),
        name="mix_outproj",
    )(att, u, vn, x, gt1, sh2, sc2, g2, wmix, bmix, w_out)


def _cand_layout():
    rows = []
    rows += [(0, j) for j in range(16)]
    rows += [(1, j) for j in range(8)]
    rows += [(8 + r, 0) for r in range(8)]
    rows += [(2, j) if j < 5 else None for j in range(8)]
    rows += [(3, j) if j < 4 else None for j in range(8)]
    rows += [(4, j) if j < 3 else None for j in range(8)]
    rows += [(r, 0) if 5 <= r < 8 else None for r in range(8)]
    rows += [(r, 1) if 5 <= r < 8 else None for r in range(8)]
    return rows


N_CAND_ROWS = 72
BIG_FLAT = 1e9
NO_EXPERT = 1e30


def _cand_flat():
    flat = np.full((N_CAND_ROWS, LANES), BIG_FLAT, np.float32)
    for r, cell in enumerate(_cand_layout()):
        if cell is not None:
            flat[r, :] = 16 * cell[0] + cell[1]
    return jnp.asarray(flat)


def _top16_ranked(s, row_id, top_ref):
    rank = jnp.full(s.shape, float(TOPK), F32)
    for it in range(TOPK):
        m = jnp.max(s, axis=0, keepdims=True)
        idx = jnp.min(jnp.where(s == m, row_id, float(N_KEYS)), axis=0, keepdims=True)
        sel = row_id == idx
        rank = jnp.where(sel, float(it), rank)
        s = jnp.where(sel, -jnp.inf, s)
        top_ref[it:it + 1, :] = m
    return rank


def _sort16_network():
    def merge(lo, hi, r):
        step = r * 2
        if step < hi - lo:
            yield from merge(lo, hi, step)
            yield from merge(lo + r, hi, step)
            yield from [(i, i + r) for i in range(lo + r, hi - r, step)]
        else:
            yield (lo, lo + r)

    def sort(lo, hi):
        if hi > lo:
            mid = lo + (hi - lo) // 2
            yield from sort(lo, mid)
            yield from sort(mid + 1, hi)
            yield from merge(lo, hi, 1)

    return list(sort(0, TOPK - 1))


def _top16_sorted(s, top_ref):
    n = N_KEYS // SUBLANES
    cols = [s[SUBLANES * v:SUBLANES * (v + 1), :] for v in range(n)]
    for i, j in _sort16_network():
        cols[i], cols[j] = jnp.maximum(cols[i], cols[j]), jnp.minimum(cols[i], cols[j])
    ties = jnp.zeros((1, s.shape[1]), F32)
    prev = None
    for it in range(TOPK + 1):
        m = jnp.max(cols[0], axis=0, keepdims=True)
        if prev is not None:
            ties = ties + jnp.where(m == prev, 1.0, 0.0)
        prev = m
        if it == TOPK:
            break
        sel = cols[0] == m
        ties = ties + jnp.sum(jnp.where(sel, 1.0, 0.0), axis=0, keepdims=True) - 1.0
        top_ref[it:it + 1, :] = m
        for v in range(TOPK - it):
            nxt = cols[v + 1] if v + 1 < n else jnp.full_like(cols[v], -jnp.inf)
            cols[v] = jnp.where(sel, nxt, cols[v])
    return ties


def _select_block(h, lc, toks, exact, keys_ref, qp_scr, top1_scr, top2_scr, flat, row_id, row8,
                  thr_ref, g1_ref, s2_ref, e2_ref):
    valid = flat < BIG_FLAT
    sides = []
    for p, top_scr in ((0, top1_scr), (1, top2_scr)):
        col = pl.multiple_of((2 * h + p) * N_KEYS, N_KEYS)
        s = lax.dot_general(keys_ref[p], qp_scr[toks, pl.ds(col, N_KEYS)],
                            (((1,), (1,)), ((), ())), preferred_element_type=F32)
        sides.append((s, _top16_ranked(s, row_id, top_scr) if exact else _top16_sorted(s, top_scr)))
    (s1, info1), (s2, info2) = sides
    a1 = top1_scr[...]
    a2 = top2_scr[...]
    v = jnp.concatenate([
        a1[0:1] + a2[0:16],
        a1[1:2] + a2[0:8],
        a1[8:16] + a2[0:1],
        a1[2:3] + a2[0:8],
        a1[3:4] + a2[0:8],
        a1[4:5] + a2[0:8],
        a1[0:8] + a2[0:1],
        a1[0:8] + a2[1:2]], axis=0)
    vw = jnp.where(valid, v, -jnp.inf)
    selm = jnp.zeros(v.shape, F32)
    for _ in range(TOPK):
        m = jnp.max(vw, axis=0, keepdims=True)
        sel = vw == m
        if exact:
            f = jnp.min(jnp.where(sel, flat, BIG_FLAT), axis=0, keepdims=True)
            sel = flat == f
        selm = jnp.where(sel, 1.0, selm)
        vw = jnp.where(sel, -jnp.inf, vw)
    vmax = a1[0:1] + a2[0:1]
    z = jnp.sum(jnp.where(selm > 0.0, jnp.exp(v - vmax), 0.0), axis=0, keepdims=True)
    cnt_lo = selm[56:64] + selm[64:72]
    for i, (lo, hi) in enumerate(((0, 16), (16, 24), (32, 40), (40, 48), (48, 56))):
        ci = jnp.sum(selm[lo:hi], axis=0, keepdims=True)
        cnt_lo = jnp.where(row8 == i, ci, cnt_lo)
    cnt = jnp.concatenate([cnt_lo, selm[24:32]], axis=0)
    if exact:
        thr = jnp.where(cnt > 0.0, 0.5 - cnt, NO_EXPERT)
        key1, key2 = info1, -info2
        match = [float(i) for i in range(TOPK)]
    else:
        thr = jnp.full(cnt.shape, NO_EXPERT, F32)
        for j in range(TOPK):
            thr = jnp.where(cnt == float(j + 1), a2[j:j + 1], thr)
        key1, key2 = s1, s2
        match = [a1[i:i + 1] for i in range(TOPK)]
    thr_a = jnp.full((N_KEYS, LANES), NO_EXPERT, F32)
    for i in range(TOPK):
        thr_a = jnp.where(key1 == match[i], thr[i:i + 1], thr_a)
    thr_ref[h, lc] = thr_a
    g1_ref[h, lc] = jnp.exp(s1 - a1[0:1]) / z
    s2_ref[h, lc] = key2
    e2_ref[h, lc] = jnp.exp(s2 - a2[0:1])
    if exact:
        return None
    n3 = jnp.sum(selm, axis=0, keepdims=True) - float(TOPK)
    return jnp.max(info1 + info2 + n3)


def _select_kernel(h2_ref, wq_ref, keys_ref, flat_ref, thr_ref, g1_ref, s2_ref, e2_ref,
                   qp_scr, top1_scr, top2_scr):
    tm = h2_ref.shape[0]
    qp_scr[...] = jnp.dot(h2_ref[...], wq_ref[...], preferred_element_type=F32).astype(BF16)
    row_id = lax.broadcasted_iota(jnp.int32, (N_KEYS, LANES), 0).astype(F32)
    row8 = lax.broadcasted_iota(jnp.int32, (8, LANES), 0)
    flat = flat_ref[...]
    blocks = []
    for lc in range(tm // LANES):
        toks = slice(lc * LANES, (lc + 1) * LANES)
        blocks.append(functools.partial(
            _select_block, lc=lc, toks=toks, keys_ref=keys_ref, qp_scr=qp_scr, top1_scr=top1_scr.at[lc],
            top2_scr=top2_scr.at[lc], flat=flat, row_id=row_id, row8=row8, thr_ref=thr_ref,
            g1_ref=g1_ref, s2_ref=s2_ref, e2_ref=e2_ref))

    def head_body(h, carry):
        ties = [block(h, exact=False) for block in blocks]
        for block, tie in zip(blocks, ties):
            @pl.when(tie > 0.5)
            def _(block=block):
                block(h, exact=True)

        return carry

    lax.fori_loop(0, N_HEADS, head_body, 0)


def _select(h2, w_query, sub_keys, tm):
    t, d = h2.shape
    out_spec = pl.BlockSpec((N_HEADS, tm // LANES, N_KEYS, LANES), lambda i: (0, i, 0, 0))
    slab = lambda dt: jax.ShapeDtypeStruct((N_HEADS, t // LANES, N_KEYS, LANES), dt)
    return pl.pallas_call(
        _select_kernel,
        grid=(t // tm,),
        in_specs=[pl.BlockSpec((tm, d), lambda i: (i, 0)),
                  pl.BlockSpec(w_query.shape, lambda i: (0, 0)),
                  pl.BlockSpec(sub_keys.shape, lambda i: (0, 0, 0)),
                  pl.BlockSpec((N_CAND_ROWS, LANES), lambda i: (0, 0))],
        out_specs=[out_spec] * 4,
        out_shape=[slab(F32)] * 4,
        scratch_shapes=[pltpu.VMEM((tm, w_query.shape[1]), BF16),
                        pltpu.VMEM((tm // LANES, TOPK, LANES), F32),
                        pltpu.VMEM((tm // LANES, TOPK, LANES), F32)],
        compiler_params=_params(("parallel",)),
        name="peer_select",
    )(h2, w_query, sub_keys, _cand_flat())


def _coef_kernel(thr_ref, g1_ref, s2_ref, e2_ref, *rest, cast):
    if cast:
        u_ref, v_ref, c_ref, u16_ref, vt16_ref = rest
        u16_ref[...] = u_ref[...].astype(BF16)
        step = pl.program_id(0) * pl.num_programs(1) + pl.program_id(1)

        @pl.when(step % 2 == 0)
        def _():
            vt16_ref[...] = v_ref[...].T.astype(BF16)
    else:
        (c_ref,) = rest
    n_l, n_a = thr_ref.shape[1], thr_ref.shape[2]
    for lc in range(n_l):
        for ai in range(n_a):
            rows = slice(ai * N_KEYS, (ai + 1) * N_KEYS)
            coef = jnp.zeros((N_KEYS, LANES), F32)
            for h in range(N_HEADS):
                thr = thr_ref[h, lc, ai:ai + 1, :]
                g1 = g1_ref[h, lc, ai:ai + 1, :]
                coef = coef + jnp.where(s2_ref[h, lc] >= thr, e2_ref[h, lc], 0.0) * g1
            c_ref[lc, rows, :] = coef.astype(BF16)


def _coef(sel, te, n_exp, tables=None):
    n_lc = sel[0].shape[1]
    n_l = min(n_lc, COEF_LANE_CHUNKS)
    grid = (n_lc // n_l, n_exp // te)
    a_spec = pl.BlockSpec((N_HEADS, n_l, te // N_KEYS, LANES), lambda i, j: (0, i, j, 0))
    b_spec = pl.BlockSpec((N_HEADS, n_l, N_KEYS, LANES), lambda i, j: (0, i, 0, 0))
    in_specs = [a_spec, a_spec, b_spec, b_spec]
    out_specs = [pl.BlockSpec((n_l, te, LANES), lambda i, j: (i, j, 0))]
    out_shape = [jax.ShapeDtypeStruct((n_lc, n_exp, LANES), BF16)]
    operands = list(sel)
    if tables is not None:
        d = tables[0].shape[1]
        steps = grid[0] * grid[1]
        u_rows = n_exp // steps
        assert u_rows * steps == n_exp and 2 * u_rows == LANES
        step = lambda i, j: i * grid[1] + j
        in_specs += [pl.BlockSpec((u_rows, d), lambda i, j: (step(i, j), 0)),
                     pl.BlockSpec((LANES, d), lambda i, j: (step(i, j) // 2, 0))]
        out_specs += [pl.BlockSpec((u_rows, d), lambda i, j: (step(i, j), 0)),
                      pl.BlockSpec((d, LANES), lambda i, j: (0, step(i, j) // 2))]
        out_shape += [jax.ShapeDtypeStruct((n_exp, d), BF16), jax.ShapeDtypeStruct((d, n_exp), BF16)]
        operands += list(tables)
    return pl.pallas_call(
        functools.partial(_coef_kernel, cast=tables is not None),
        grid=grid,
        in_specs=in_specs,
        out_specs=out_specs,
        out_shape=out_shape,
        compiler_params=_params(("arbitrary", "arbitrary")),
        name="peer_coef",
    )(*operands)


def _peer_kernel(h2t_ref, u_ref, vt_ref, c_ref, x1_ref, gt2_ref, gf_ref, y_ref, acc_scr):
    j = pl.program_id(1)
    tt = h2t_ref.shape[1]

    @pl.when(j == 0)
    def _():
        acc_scr[...] = jnp.zeros_like(acc_scr)

    act = jnp.dot(u_ref[...], h2t_ref[...], preferred_element_type=F32)
    p = jnp.concatenate(
        [_gelu(act[:, lc * LANES:(lc + 1) * LANES]).astype(BF16) * c_ref[lc]
         for lc in range(tt // LANES)], axis=1)
    acc_scr[...] += jnp.dot(vt_ref[...], p, preferred_element_type=F32)

    @pl.when(j == pl.num_programs(1) - 1)
    def _():
        x2 = x1_ref[...] + gt2_ref[...] * acc_scr[...].T
        y_ref[...] = _rms(x2, gf_ref[...])


def _peer(h2t, experts, sel, x1, gt2, g_final, rows_per_group, tt, te):
    kind, u, vt = experts
    n_exp = u.shape[0]
    if kind == "f32":
        coef, u, vt = _coef(sel, te, n_exp, tables=(u, vt))
    else:
        (coef,) = _coef(sel, te, n_exp)
    d, t = h2t.shape
    y = pl.pallas_call(
        _peer_kernel,
        grid=(t // tt, n_exp // te),
        in_specs=[pl.BlockSpec((d, tt), lambda i, j: (0, i)),
                  pl.BlockSpec((te, d), lambda i, j: (j, 0)),
                  pl.BlockSpec((d, te), lambda i, j: (0, j)),
                  pl.BlockSpec((tt // LANES, te, LANES), lambda i, j: (i, j, 0)),
                  pl.BlockSpec((tt, d), lambda i, j: (i, 0)),
                  _mod_spec(gt2, rows_per_group // tt),
                  pl.BlockSpec((1, d), lambda i, j: (0, 0))],
        out_specs=pl.BlockSpec((tt, d), lambda i, j: (i, 0)),
        out_shape=jax.ShapeDtypeStruct((t, d), F32),
        scratch_shapes=[pltpu.VMEM((d, tt), F32)],
        compiler_params=_params(("parallel", "arbitrary")),
        name="peer_dense",
    )(h2t, u, vt, coef, x1, gt2, g_final)
    return y, ("bf16", u, vt)


def _trunk(x, mods, n_groups, window, att_fn, w, experts, wmix, bmix, tm_in, tm_mix, tm_sel, tt, te):
    sh1, sc1, gt1, sh2, sc2, gt2 = mods
    rows = x.shape[0] // n_groups
    q, k, v, u, vn, *wins = _inproj(x, sh1, sc1, w["g1"], w["w_in"], w["lng"], w["lnb"], tm_in,
                                    n_groups, window)
    att = att_fn(q, k, v)
    x1, h2, h2t = _mix(att, u, vn, x, gt1, sh2, sc2, w["g2"], wmix, bmix, w["w_out"], tm_mix)
    sel = _select(h2, w["w_query"], w["sub_keys"], tm_sel)
    y, experts = _peer(h2t, experts, sel, x1, gt2, w["g_final"], rows, tt, te)
    return y, k, v, vn, wins, experts


def kernel(x_prompt, x_sample, cache_k, cache_v, c_prompt, c_sample, w_ada, b_ada, g_norm1, w_in,
           ln_v_g, ln_v_b, w_spatial, b_spatial, w_out, g_norm2, w_query, sub_keys, expert_u,
           expert_v, g_final):
    depth = w_ada.shape[0]
    assert depth == 1, "single-layer trunk"
    batch, seq, d = x_prompt.shape
    n_dec, n_new, _ = x_sample.shape
    buf = cache_k.shape[2]
    assert seq % SPAN == 0 and buf == SPAN and n_new <= SLOTS
    win = min(BRANCHES[-1][0], seq)
    pad_s = BAND
    assert n_dec * SLOTS <= pad_s

    n_c = batch + n_dec
    c_all = jnp.pad(jnp.concatenate([c_prompt, c_sample], axis=0), ((0, -n_c % SUBLANES), (0, 0)))
    mod = _adaln(c_all, w_ada[0], b_ada[0][None, :])
    mods = [mod[:, i * d:(i + 1) * d] for i in range(6)]
    mods_p = [m[:batch, None, :] for m in mods]
    mods_s = [jnp.pad(jnp.repeat(m[batch:n_c], SLOTS, axis=0), ((0, pad_s - n_dec * SLOTS), (0, 0)))[None]
              for m in mods]

    tri = jnp.tril(jnp.ones((BAND, BAND), dtype=bool))
    ws = jnp.where(tri, w_spatial[0], 0.0)
    bs = b_spatial[0]
    wmix_p = ws.astype(BF16)
    bmix_p = jnp.broadcast_to(bs[:, :, None], (N_HEADS, BAND, HEAD_DIM))
    eye = jnp.eye(pad_s // SLOTS, dtype=F32)
    slot_pad = ((0, 0), (0, SLOTS - n_new))
    ws_new = jnp.pad(ws[:, :n_new, :n_new], ((0, 0),) + slot_pad[1:] + slot_pad[1:])
    wmix_s = jax.vmap(lambda m: jnp.kron(eye, m))(ws_new).astype(BF16)
    bmix_s = jnp.broadcast_to(jnp.tile(jnp.pad(bs[:, :n_new], slot_pad), (1, pad_s // SLOTS))[:, :, None],
                              (N_HEADS, pad_s, HEAD_DIM))

    w = dict(g1=g_norm1[0][None, :], w_in=w_in[0].astype(BF16), lng=ln_v_g[0][None, :],
             lnb=ln_v_b[0][None, :], g2=g_norm2[0][None, :], w_out=w_out[0].astype(BF16),
             w_query=w_query[0].astype(BF16), sub_keys=sub_keys[0].astype(BF16),
             g_final=g_final[None, :])

    xp = x_prompt.reshape(batch * seq, d)
    y_p, _, _, _, (k_win, v_win), experts = _trunk(
        xp, mods_p, batch, win, _prompt_attention, w, ("f32", expert_u[0], expert_v[0]), wmix_p, bmix_p,
        tm_in=512, tm_mix=256, tm_sel=512, tt=512, te=1024)

    xs = jnp.pad(x_sample, ((0, 0), (0, SLOTS - n_new), (0, 0))).reshape(n_dec * SLOTS, d)
    xs = jnp.pad(xs, ((0, pad_s - n_dec * SLOTS), (0, 0)))

    def sample_att(q, k, v):
        o = _sample_attention(q[0], k[0], v[0], cache_k[0], cache_v[0], n_new)
        o = o.transpose(1, 0, 2, 3).reshape(N_HEADS, n_dec * SLOTS, HEAD_DIM)
        return jnp.pad(o, ((0, 0), (0, pad_s - n_dec * SLOTS), (0, 0)))[None]

    y_s, k_s, v_s, vn_s, _, _ = _trunk(
        xs, mods_s, 1, 0, sample_att, w, experts, wmix_s, bmix_s,
        tm_in=pad_s, tm_mix=pad_s, tm_sel=pad_s, tt=pad_s, te=1024)

    new_rows = lambda t: t[:n_dec * SLOTS].reshape(n_dec, SLOTS, -1)[:, :n_new]
    new_heads = lambda t: t[0, :, :n_dec * SLOTS].reshape(N_HEADS, n_dec, SLOTS, HEAD_DIM)[:, :, :n_new
                                                                                         ].transpose(1, 2, 0, 3)
    return (y_p.reshape(batch, seq, d),
            new_rows(y_s),
            k_win.reshape(1, batch, win, N_HEADS, HEAD_DIM),
            v_win.reshape(1, batch, win, N_HEADS, HEAD_DIM),
            new_heads(k_s)[None],
            new_heads(v_s)[None],
            new_rows(vn_s)[None])
```

```python
import functools

import numpy as np
import jax
import jax.numpy as jnp
from jax import lax
from jax.experimental import pallas as pl
from jax.experimental.pallas import tpu as pltpu

F32 = jnp.float32
BF16 = jnp.bfloat16

EPS = 1e-6
LANES = 128
SUBLANES = 8
HEAD_DIM = 128
N_HEADS = 8
ATT_WIDTH = N_HEADS * HEAD_DIM
BRANCHES = ((128, 1), (512, 4), (2048, 16))
BAND = 128
SPAN = BAND * BRANCHES[-1][1]
N_KEYS = 128
TOPK = 16
NEG = -1e30
VMEM_LIMIT = 56 * 1024 * 1024
COEF_LANE_CHUNKS = 4
SLOTS = SUBLANES


def _params(sem, flags=None):
    return pltpu.CompilerParams(dimension_semantics=sem, vmem_limit_bytes=VMEM_LIMIT, flags=flags)


def _gelu(x):
    return 0.5 * x * (1.0 + lax.erf(x * (2.0 ** -0.5)))


def _rms(x, g):
    return x * lax.rsqrt(jnp.mean(x * x, axis=-1, keepdims=True) + EPS) * g


def _slope(h):
    return 2.0 ** (-8.0 * (h + 1) / N_HEADS)


def _adaln_kernel(c_ref, w_ref, b_ref, o_ref):
    c = c_ref[...]
    a = c / (1.0 + jnp.exp(-c))
    o_ref[...] = jnp.dot(a.astype(BF16), w_ref[...].astype(BF16),
                         preferred_element_type=F32) + b_ref[...]


def _adaln(c_all, w_ada, b_ada):
    rows, d = c_all.shape
    n = w_ada.shape[1]
    tn = 1024
    return pl.pallas_call(
        _adaln_kernel,
        grid=(n // tn,),
        in_specs=[pl.BlockSpec((rows, d), lambda j: (0, 0)),
                  pl.BlockSpec((d, tn), lambda j: (0, j)),
                  pl.BlockSpec((1, tn), lambda j: (0, j))],
        out_specs=pl.BlockSpec((rows, tn), lambda j: (0, j)),
        out_shape=jax.ShapeDtypeStruct((rows, n), F32),
        compiler_params=_params(("arbitrary",)),
        name="adaln",
    )(c_all, w_ada, b_ada)


def _inproj_kernel(x_ref, sh_ref, sc_ref, g1_ref, w_ref, lng_ref, lnb_ref,
                   q_ref, k_ref, v_ref, u_ref, vn_ref, *rest, win_tiles):
    if win_tiles:
        kwin_ref, vwin_ref, h_scr = rest
        tpg, wt = win_tiles
        in_window = pl.program_id(0) % tpg >= tpg - wt
    else:
        (h_scr,) = rest
    j = pl.program_id(1)

    @pl.when(j == 0)
    def _():
        y = _rms(x_ref[...], g1_ref[...])
        h_scr[...] = (y * (1.0 + sc_ref[...]) + sh_ref[...]).astype(BF16)

    p = jnp.dot(h_scr[...], w_ref[...], preferred_element_type=F32)

    def heads_out(ref, val, win_ref=None):
        for h in range(N_HEADS):
            ref[h] = val[:, h * HEAD_DIM:(h + 1) * HEAD_DIM]
        if win_ref is not None:
            @pl.when(in_window)
            def _():
                for h in range(N_HEADS):
                    win_ref[:, h, :] = val[:, h * HEAD_DIM:(h + 1) * HEAD_DIM]

    @pl.when(j == 0)
    def _():
        heads_out(q_ref, p * (HEAD_DIM ** -0.5))

    @pl.when(j == 1)
    def _():
        heads_out(k_ref, p, kwin_ref if win_tiles else None)

    @pl.when(j == 2)
    def _():
        heads_out(v_ref, p, vwin_ref if win_tiles else None)

    @pl.when(j == 3)
    def _():
        u_ref[...] = _gelu(p).astype(BF16)

    @pl.when(j == 4)
    def _():
        g = _gelu(p)
        mu = jnp.mean(g, axis=-1, keepdims=True)
        gc = g - mu
        var = jnp.mean(gc * gc, axis=-1, keepdims=True)
        vn_ref[...] = gc * lax.rsqrt(var + EPS) * lng_ref[...] + lnb_ref[...]


def _mod_spec(mod, tiles_per_group):
    _, r, d = mod.shape
    return pl.BlockSpec((None, r, d), lambda i, *_: (i // tiles_per_group, 0, 0))


def _inproj(x, sh1, sc1, g1, w_in, lng, lnb, tm, n_groups, window):
    t, d = x.shape
    sec = ATT_WIDTH
    n_sec = w_in.shape[1] // sec
    rows = t // n_groups
    tpg = rows // tm
    tok = lambda i, j: (i, 0)
    const = lambda i, j: (0, 0)
    out_spec = pl.BlockSpec((tm, sec), tok)
    head_spec = pl.BlockSpec((None, N_HEADS, tm, HEAD_DIM), lambda i, j: (i // tpg, 0, i % tpg, 0))
    head_shape = jax.ShapeDtypeStruct((n_groups, N_HEADS, rows, HEAD_DIM), F32)
    out_specs = [head_spec] * 3 + [out_spec] * 2
    out_shape = [head_shape] * 3 + [jax.ShapeDtypeStruct((t, sec), BF16),
                                    jax.ShapeDtypeStruct((t, sec), F32)]
    if window:
        wt = window // tm
        win_spec = pl.BlockSpec((tm, N_HEADS, HEAD_DIM), lambda i, j: (
            (i // tpg) * wt + jnp.maximum(i % tpg - (tpg - wt), 0), 0, 0))
        win_shape = jax.ShapeDtypeStruct((n_groups * window, N_HEADS, HEAD_DIM), F32)
        out_specs += [win_spec] * 2
        out_shape += [win_shape] * 2
    return pl.pallas_call(
        functools.partial(_inproj_kernel, win_tiles=(tpg, window // tm) if window else None),
        grid=(t // tm, n_sec),
        in_specs=[pl.BlockSpec((tm, d), tok),
                  _mod_spec(sh1, tpg), _mod_spec(sc1, tpg),
                  pl.BlockSpec((1, d), const),
                  pl.BlockSpec((d, sec), lambda i, j: (0, j)),
                  pl.BlockSpec((1, sec), const), pl.BlockSpec((1, sec), const)],
        out_specs=out_specs,
        out_shape=out_shape,
        scratch_shapes=[pltpu.VMEM((tm, d), BF16)],
        compiler_params=_params(("arbitrary", "arbitrary")),
        name="inproj",
    )(x, sh1, sc1, g1, w_in, lng, lnb)


def _att_bias():
    a = np.arange(BAND)[:, None]
    kk = np.arange(2 * BAND)[None, :]
    steps = BAND + a - kk
    valid = (steps >= 0) & (steps <= BAND)
    slopes = np.array([_slope(h) for h in range(N_HEADS)])
    out = []
    for _, dil in BRANCHES:
        bias = -slopes[:, None, None] * (dil * steps).astype(np.float64)[None]
        out.append(np.where(valid[None], bias, NEG))
    return jnp.asarray(np.stack(out), dtype=F32)


def _att_prompt_kernel(q_ref, kc_ref, kp_ref, vc_ref, vp_ref, bias_ref, o_ref, m_scr, l_scr, acc_scr):
    first_span = pl.program_id(2) == 0
    span = q_ref.shape[0]
    is_prev = lax.broadcasted_iota(jnp.int32, (BAND, 2 * BAND), 1) < BAND

    def rows_at(start, dil):
        return pl.ds(start, BAND) if dil == 1 else pl.ds(start, BAND, stride=dil)

    for bi, (_, dil) in enumerate(BRANCHES):
        for r in range(dil):
            for n in range(span // (BAND * dil)):
                start = r + dil * BAND * n
                rows = rows_at(start, dil)
                if n > 0:
                    prev = rows_at(start - dil * BAND, dil)
                    kp, vp = kc_ref[prev, :], vc_ref[prev, :]
                else:
                    prev = rows_at(span - dil * BAND + r, dil)
                    kp, vp = kp_ref[prev, :], vp_ref[prev, :]
                kw = jnp.concatenate([kp, kc_ref[rows, :]], axis=0).astype(BF16)
                vw = jnp.concatenate([vp, vc_ref[rows, :]], axis=0).astype(BF16)
                s = lax.dot_general(q_ref[rows, :].astype(BF16), kw, (((1,), (1,)), ((), ())),
                                    preferred_element_type=F32) + bias_ref[bi]
                if n == 0:
                    s = jnp.where(jnp.logical_and(first_span, is_prev), NEG, s)
                m_b = jnp.max(s, axis=-1, keepdims=True)
                p = jnp.exp(s - m_b)
                l_b = jnp.sum(p, axis=-1, keepdims=True)
                pv = jnp.dot(p.astype(BF16), vw, preferred_element_type=F32)
                m_scr[bi, rows, :] = jnp.broadcast_to(m_b, (BAND, LANES))
                l_scr[bi, rows, :] = jnp.broadcast_to(l_b, (BAND, LANES))
                acc_scr[bi, rows, :] = pv
    ms = [m_scr[bi] for bi in range(len(BRANCHES))]
    m_all = functools.reduce(jnp.maximum, ms)
    es = [jnp.exp(m - m_all) for m in ms]
    l_all = sum(e * l_scr[bi] for bi, e in enumerate(es))
    acc_all = sum(e * acc_scr[bi] for bi, e in enumerate(es))
    o_ref[...] = (acc_all / l_all).astype(o_ref.dtype)


def _prompt_attention(q, k, v):
    b, nh, seq, hd = q.shape
    cur = lambda bb, h, s: (bb, h, s, 0)
    prev = lambda bb, h, s: (bb, h, jnp.maximum(s - 1, 0), 0)
    blk = lambda m: pl.BlockSpec((None, None, SPAN, hd), m)
    return pl.pallas_call(
        _att_prompt_kernel,
        grid=(b, nh, seq // SPAN),
        in_specs=[blk(cur), blk(cur), blk(prev), blk(cur), blk(prev),
                  pl.BlockSpec((len(BRANCHES), None, BAND, 2 * BAND), lambda bb, h, s: (0, h, 0, 0))],
        out_specs=blk(cur),
        out_shape=jax.ShapeDtypeStruct(q.shape, BF16),
        scratch_shapes=[pltpu.VMEM((len(BRANCHES), SPAN, LANES), F32)] * 3,
        compiler_params=_params(("parallel", "parallel", "arbitrary")),
        name="attention_prompt",
    )(q, k, k, v, v, _att_bias())


def _att_sample_kernel(q_ref, kn_ref, vn_ref, kt_ref, vt_ref, kr_ref, vr_ref, o_ref, *, n_new):
    o_ref[...] = jnp.zeros_like(o_ref)
    col = lax.broadcasted_iota(jnp.int32, (BAND, 1), 0).astype(F32)
    slot = lax.broadcasted_iota(jnp.int32, (SLOTS, 1), 0)
    tail = kt_ref.shape[0] // N_HEADS
    for h in range(N_HEADS):
        slope = _slope(h)
        knh, vnh = kn_ref[h], vn_ref[h]
        for t in range(n_new):
            q_row = q_ref[h, t:t + 1, :]
            parts = []
            for _, dil in BRANCHES:
                a = t if dil == 1 else 0
                if dil == 1:
                    rows = pl.ds((tail - BAND) * N_HEADS + h, BAND, stride=N_HEADS)
                    kp, vp = kt_ref[rows, :], vt_ref[rows, :]
                elif dil * BAND == tail:
                    rows = pl.ds(t * N_HEADS + h, BAND, stride=dil * N_HEADS)
                    kp, vp = kt_ref[rows, :], vt_ref[rows, :]
                else:
                    kp, vp = kr_ref[:, t * N_HEADS + h, :], vr_ref[:, t * N_HEADS + h, :]
                steps = float(BAND + a) - col
                s_p = jnp.sum(kp * q_row, axis=1, keepdims=True) - (slope * dil) * steps
                if a > 0:
                    s_p = jnp.where(steps <= float(BAND), s_p, NEG)
                s_n = jnp.sum(knh * q_row, axis=1, keepdims=True)
                if dil == 1:
                    s_n = jnp.where(slot <= t, s_n - slope * (t - slot).astype(F32), NEG)
                else:
                    s_n = jnp.where(slot == t, s_n, NEG)
                m = jnp.maximum(jnp.max(s_p, axis=0, keepdims=True), jnp.max(s_n, axis=0, keepdims=True))
                p_p = jnp.exp(s_p - m)
                p_n = jnp.exp(s_n - m)
                l = jnp.sum(p_p, axis=0, keepdims=True) + jnp.sum(p_n, axis=0, keepdims=True)
                acc = (jnp.sum(p_p * vp, axis=0, keepdims=True)
                       + jnp.sum(p_n * vnh, axis=0, keepdims=True))
                parts.append((m, l, acc))
            m_all = functools.reduce(jnp.maximum, [m for m, _, _ in parts])
            l_all = sum(l * jnp.exp(m - m_all) for m, l, _ in parts)
            acc_all = sum(acc * jnp.exp(m - m_all) for m, _, acc in parts)
            o_ref[h, t:t + 1, :] = acc_all / l_all


def _sample_attention(q, k, v, cache_k, cache_v, n_new):
    n_batch, buf, nh, hd = cache_k.shape
    res16 = BRANCHES[-1][1]
    tail = 4 * BAND
    assert buf == res16 * BAND and n_new * nh <= 32
    rows2d = lambda c: c.reshape(n_batch, buf * nh, hd)
    resid = lambda c: c.reshape(n_batch, BAND, res16 * nh, hd)
    new_spec = pl.BlockSpec((nh, SLOTS, hd), lambda b: (0, b, 0))
    tail_spec = pl.BlockSpec((None, tail * nh, hd), lambda b: (b, buf // tail - 1, 0))
    res_spec = pl.BlockSpec((None, BAND, 32, hd), lambda b: (b, 0, 0, 0))
    return pl.pallas_call(
        functools.partial(_att_sample_kernel, n_new=n_new),
        grid=(n_batch,),
        in_specs=[new_spec, new_spec, new_spec, tail_spec, tail_spec, res_spec, res_spec],
        out_specs=pl.BlockSpec((None, nh, SLOTS, hd), lambda b: (b, 0, 0, 0)),
        out_shape=jax.ShapeDtypeStruct((n_batch, nh, SLOTS, hd), F32),
        compiler_params=_params(("parallel",)),
        name="attention_sample",
    )(q, k, v, rows2d(cache_k), rows2d(cache_v), resid(cache_k), resid(cache_v))


def _mix_kernel(att_ref, u_ref, vn_ref, x_ref, gt1_ref, sh2_ref, sc2_ref, g2_ref, wmix_ref, bmix_ref,
                wout_ref, x1_ref, h2_ref, h2t_ref, mix_scr):
    tm = x_ref.shape[0]
    for h in range(N_HEADS):
        mix_scr[:, h * HEAD_DIM:(h + 1) * HEAD_DIM] = att_ref[h].astype(BF16)
    for c in range(tm // BAND):
        rows = slice(c * BAND, (c + 1) * BAND)
        for g in range(N_HEADS):
            cs = slice(g * HEAD_DIM, (g + 1) * HEAD_DIM)
            m = jnp.dot(wmix_ref[g], vn_ref[rows, cs].astype(BF16),
                        preferred_element_type=F32) + bmix_ref[g]
            mix_scr[rows, ATT_WIDTH + g * HEAD_DIM:ATT_WIDTH + (g + 1) * HEAD_DIM] = (
                u_ref[rows, cs].astype(F32) * m).astype(BF16)
    y = jnp.dot(mix_scr[...], wout_ref[...], preferred_element_type=F32)
    x1 = x_ref[...] + gt1_ref[...] * y
    x1_ref[...] = x1
    h2 = _rms(x1, g2_ref[...]) * (1.0 + sc2_ref[...]) + sh2_ref[...]
    h2_ref[...] = h2.astype(BF16)
    h2t_ref[...] = h2.T.astype(BF16)


def _mix(att, u, vn, x, gt1, sh2, sc2, g2, wmix, bmix, w_out, tm):
    t, d = x.shape
    tpg = att.shape[2] // tm
    tok = lambda i: (i, 0)
    c2 = lambda i: (0, 0)
    c3 = lambda i: (0, 0, 0)
    return pl.pallas_call(
        _mix_kernel,
        grid=(t // tm,),
        in_specs=[pl.BlockSpec((None, N_HEADS, tm, HEAD_DIM), lambda i: (i // tpg, 0, i % tpg, 0)),
                  pl.BlockSpec((tm, ATT_WIDTH), tok), pl.BlockSpec((tm, ATT_WIDTH), tok),
                  pl.BlockSpec((tm, d), tok),
                  _mod_spec(gt1, tpg), _mod_spec(sh2, tpg), _mod_spec(sc2, tpg),
                  pl.BlockSpec((1, d), c2),
                  pl.BlockSpec(wmix.shape, c3), pl.BlockSpec(bmix.shape, c3),
                  pl.BlockSpec(w_out.shape, c2)],
        out_specs=[pl.BlockSpec((tm, d), tok), pl.BlockSpec((tm, d), tok),
                   pl.BlockSpec((d, tm), lambda i: (0, i))],
        out_shape=[jax.ShapeDtypeStruct((t, d), F32),
                   jax.ShapeDtypeStruct((t, d), BF16),
                   jax.ShapeDtypeStruct((d, t), BF16)],
        scratch_shapes=[pltpu.VMEM((tm, d), BF16)],
        compiler_params=_params(("parallel",)),
        name="mix_outproj",
    )(att, u, vn, x, gt1, sh2, sc2, g2, wmix, bmix, w_out)


def _cand_layout():
    rows = []
    rows += [(0, j) for j in range(16)]
    rows += [(1, j) for j in range(8)]
    rows += [(8 + r, 0) for r in range(8)]
    rows += [(2, j) if j < 5 else None for j in range(8)]
    rows += [(3, j) if j < 4 else None for j in range(8)]
    rows += [(4, j) if j < 3 else None for j in range(8)]
    rows += [(r, 0) if 5 <= r < 8 else None for r in range(8)]
    rows += [(r, 1) if 5 <= r < 8 else None for r in range(8)]
    return rows


N_CAND_ROWS = 72
BIG_FLAT = 1e9
NO_EXPERT = 1e30


def _cand_flat():
    flat = np.full((N_CAND_ROWS, LANES), BIG_FLAT, np.float32)
    for r, cell in enumerate(_cand_layout()):
        if cell is not None:
            flat[r, :] = 16 * cell[0] + cell[1]
    return jnp.asarray(flat)


def _top16_ranked(s, row_id, top_ref):
    rank = jnp.full(s.shape, float(TOPK), F32)
    for it in range(TOPK):
        m = jnp.max(s, axis=0, keepdims=True)
        idx = jnp.min(jnp.where(s == m, row_id, float(N_KEYS)), axis=0, keepdims=True)
        sel = row_id == idx
        rank = jnp.where(sel, float(it), rank)
        s = jnp.where(sel, -jnp.inf, s)
        top_ref[it:it + 1, :] = m
    return rank


def _sort16_network():
    def merge(lo, hi, r):
        step = r * 2
        if step < hi - lo:
            yield from merge(lo, hi, step)
            yield from merge(lo + r, hi, step)
            yield from [(i, i + r) for i in range(lo + r, hi - r, step)]
        else:
            yield (lo, lo + r)

    def sort(lo, hi):
        if hi > lo:
            mid = lo + (hi - lo) // 2
            yield from sort(lo, mid)
            yield from sort(mid + 1, hi)
            yield from merge(lo, hi, 1)

    return list(sort(0, TOPK - 1))


def _top16_sorted(s, top_ref):
    n = N_KEYS // SUBLANES
    cols = [s[SUBLANES * v:SUBLANES * (v + 1), :] for v in range(n)]
    for i, j in _sort16_network():
        cols[i], cols[j] = jnp.maximum(cols[i], cols[j]), jnp.minimum(cols[i], cols[j])
    ties = jnp.zeros((1, s.shape[1]), F32)
    prev = None
    for it in range(TOPK + 1):
        m = jnp.max(cols[0], axis=0, keepdims=True)
        if prev is not None:
            ties = ties + jnp.where(m == prev, 1.0, 0.0)
        prev = m
        if it == TOPK:
            break
        sel = cols[0] == m
        ties = ties + jnp.sum(jnp.where(sel, 1.0, 0.0), axis=0, keepdims=True) - 1.0
        top_ref[it:it + 1, :] = m
        for v in range(TOPK - it):
            nxt = cols[v + 1] if v + 1 < n else jnp.full_like(cols[v], -jnp.inf)
            cols[v] = jnp.where(sel, nxt, cols[v])
    return ties


def _select_block(h, lc, toks, exact, keys_ref, qp_scr, top1_scr, top2_scr, flat, row_id, row8,
                  thr_ref, g1_ref, s2_ref, e2_ref):
    valid = flat < BIG_FLAT
    sides = []
    for p, top_scr in ((0, top1_scr), (1, top2_scr)):
        col = pl.multiple_of((2 * h + p) * N_KEYS, N_KEYS)
        s = lax.dot_general(keys_ref[p], qp_scr[toks, pl.ds(col, N_KEYS)],
                            (((1,), (1,)), ((), ())), preferred_element_type=F32)
        sides.append((s, _top16_ranked(s, row_id, top_scr) if exact else _top16_sorted(s, top_scr)))
    (s1, info1), (s2, info2) = sides
    a1 = top1_scr[...]
    a2 = top2_scr[...]
    v = jnp.concatenate([
        a1[0:1] + a2[0:16],
        a1[1:2] + a2[0:8],
        a1[8:16] + a2[0:1],
        a1[2:3] + a2[0:8],
        a1[3:4] + a2[0:8],
        a1[4:5] + a2[0:8],
        a1[0:8] + a2[0:1],
        a1[0:8] + a2[1:2]], axis=0)
    vw = jnp.where(valid, v, -jnp.inf)
    selm = jnp.zeros(v.shape, F32)
    for _ in range(TOPK):
        m = jnp.max(vw, axis=0, keepdims=True)
        sel = vw == m
        if exact:
            f = jnp.min(jnp.where(sel, flat, BIG_FLAT), axis=0, keepdims=True)
            sel = flat == f
        selm = jnp.where(sel, 1.0, selm)
        vw = jnp.where(sel, -jnp.inf, vw)
    vmax = a1[0:1] + a2[0:1]
    z = jnp.sum(jnp.where(selm > 0.0, jnp.exp(v - vmax), 0.0), axis=0, keepdims=True)
    cnt_lo = selm[56:64] + selm[64:72]
    for i, (lo, hi) in enumerate(((0, 16), (16, 24), (32, 40), (40, 48), (48, 56))):
        ci = jnp.sum(selm[lo:hi], axis=0, keepdims=True)
        cnt_lo = jnp.where(row8 == i, ci, cnt_lo)
    cnt = jnp.concatenate([cnt_lo, selm[24:32]], axis=0)
    if exact:
        thr = jnp.where(cnt > 0.0, 0.5 - cnt, NO_EXPERT)
        key1, key2 = info1, -info2
        match = [float(i) for i in range(TOPK)]
    else:
        thr = jnp.full(cnt.shape, NO_EXPERT, F32)
        for j in range(TOPK):
            thr = jnp.where(cnt == float(j + 1), a2[j:j + 1], thr)
        key1, key2 = s1, s2
        match = [a1[i:i + 1] for i in range(TOPK)]
    thr_a = jnp.full((N_KEYS, LANES), NO_EXPERT, F32)
    for i in range(TOPK):
        thr_a = jnp.where(key1 == match[i], thr[i:i + 1], thr_a)
    thr_ref[h, lc] = thr_a
    g1_ref[h, lc] = jnp.exp(s1 - a1[0:1]) / z
    s2_ref[h, lc] = key2
    e2_ref[h, lc] = jnp.exp(s2 - a2[0:1])
    if exact:
        return None
    n3 = jnp.sum(selm, axis=0, keepdims=True) - float(TOPK)
    return jnp.max(info1 + info2 + n3)


def _select_kernel(h2_ref, wq_ref, keys_ref, flat_ref, thr_ref, g1_ref, s2_ref, e2_ref,
                   qp_scr, top1_scr, top2_scr):
    tm = h2_ref.shape[0]
    qp_scr[...] = jnp.dot(h2_ref[...], wq_ref[...], preferred_element_type=F32).astype(BF16)
    row_id = lax.broadcasted_iota(jnp.int32, (N_KEYS, LANES), 0).astype(F32)
    row8 = lax.broadcasted_iota(jnp.int32, (8, LANES), 0)
    flat = flat_ref[...]
    blocks = []
    for lc in range(tm // LANES):
        toks = slice(lc * LANES, (lc + 1) * LANES)
        blocks.append(functools.partial(
            _select_block, lc=lc, toks=toks, keys_ref=keys_ref, qp_scr=qp_scr, top1_scr=top1_scr.at[lc],
            top2_scr=top2_scr.at[lc], flat=flat, row_id=row_id, row8=row8, thr_ref=thr_ref,
            g1_ref=g1_ref, s2_ref=s2_ref, e2_ref=e2_ref))

    def head_pair_body(hh, carry):
        work = [(block, h) for h in (2 * hh, 2 * hh + 1) for block in blocks]
        ties = [block(h, exact=False) for block, h in work]
        for (block, h), tie in zip(work, ties):
            @pl.when(tie > 0.5)
            def _(block=block, h=h):
                block(h, exact=True)

        return carry

    lax.fori_loop(0, N_HEADS // 2, head_pair_body, 0)


def _select(h2, w_query, sub_keys, tm):
    t, d = h2.shape
    out_spec = pl.BlockSpec((N_HEADS, tm // LANES, N_KEYS, LANES), lambda i: (0, i, 0, 0))
    slab = lambda dt: jax.ShapeDtypeStruct((N_HEADS, t // LANES, N_KEYS, LANES), dt)
    return pl.pallas_call(
        _select_kernel,
        grid=(t // tm,),
        in_specs=[pl.BlockSpec((tm, d), lambda i: (i, 0)),
                  pl.BlockSpec(w_query.shape, lambda i: (0, 0)),
                  pl.BlockSpec(sub_keys.shape, lambda i: (0, 0, 0)),
                  pl.BlockSpec((N_CAND_ROWS, LANES), lambda i: (0, 0))],
        out_specs=[out_spec] * 4,
        out_shape=[slab(F32)] * 4,
        scratch_shapes=[pltpu.VMEM((tm, w_query.shape[1]), BF16),
                        pltpu.VMEM((tm // LANES, TOPK, LANES), F32),
                        pltpu.VMEM((tm // LANES, TOPK, LANES), F32)],
        compiler_params=_params(("parallel",)),
        name="peer_select",
    )(h2, w_query, sub_keys, _cand_flat())


def _coef_kernel(thr_ref, g1_ref, s2_ref, e2_ref, *rest, cast):
    if cast:
        u_ref, v_ref, c_ref, u16_ref, vt16_ref = rest
        u16_ref[...] = u_ref[...].astype(BF16)
        step = pl.program_id(0) * pl.num_programs(1) + pl.program_id(1)

        @pl.when(step % 2 == 0)
        def _():
            vt16_ref[...] = v_ref[...].T.astype(BF16)
    else:
        (c_ref,) = rest
    n_l, n_a = thr_ref.shape[1], thr_ref.shape[2]
    for lc in range(n_l):
        for ai in range(n_a):
            rows = slice(ai * N_KEYS, (ai + 1) * N_KEYS)
            coef = jnp.zeros((N_KEYS, LANES), F32)
            for h in range(N_HEADS):
                thr = thr_ref[h, lc, ai:ai + 1, :]
                g1 = g1_ref[h, lc, ai:ai + 1, :]
                coef = coef + jnp.where(s2_ref[h, lc] >= thr, e2_ref[h, lc], 0.0) * g1
            c_ref[lc, rows, :] = coef.astype(BF16)


def _coef(sel, te, n_exp, tables=None):
    n_lc = sel[0].shape[1]
    n_l = min(n_lc, COEF_LANE_CHUNKS)
    grid = (n_lc // n_l, n_exp // te)
    a_spec = pl.BlockSpec((N_HEADS, n_l, te // N_KEYS, LANES), lambda i, j: (0, i, j, 0))
    b_spec = pl.BlockSpec((N_HEADS, n_l, N_KEYS, LANES), lambda i, j: (0, i, 0, 0))
    in_specs = [a_spec, a_spec, b_spec, b_spec]
    out_specs = [pl.BlockSpec((n_l, te, LANES), lambda i, j: (i, j, 0))]
    out_shape = [jax.ShapeDtypeStruct((n_lc, n_exp, LANES), BF16)]
    operands = list(sel)
    if tables is not None:
        d = tables[0].shape[1]
        steps = grid[0] * grid[1]
        u_rows = n_exp // steps
        assert u_rows * steps == n_exp and 2 * u_rows == LANES
        step = lambda i, j: i * grid[1] + j
        in_specs += [pl.BlockSpec((u_rows, d), lambda i, j: (step(i, j), 0)),
                     pl.BlockSpec((LANES, d), lambda i, j: (step(i, j) // 2, 0))]
        out_specs += [pl.BlockSpec((u_rows, d), lambda i, j: (step(i, j), 0)),
                      pl.BlockSpec((d, LANES), lambda i, j: (0, step(i, j) // 2))]
        out_shape += [jax.ShapeDtypeStruct((n_exp, d), BF16), jax.ShapeDtypeStruct((d, n_exp), BF16)]
        operands += list(tables)
    return pl.pallas_call(
        functools.partial(_coef_kernel, cast=tables is not None),
        grid=grid,
        in_specs=in_specs,
        out_specs=out_specs,
        out_shape=out_shape,
        compiler_params=_params(("arbitrary", "arbitrary")),
        name="peer_coef",
    )(*operands)


def _peer_kernel(h2t_ref, u_ref, vt_ref, c_ref, x1_ref, gt2_ref, gf_ref, y_ref, acc_scr):
    j = pl.program_id(1)
    tt = h2t_ref.shape[1]

    @pl.when(j == 0)
    def _():
        acc_scr[...] = jnp.zeros_like(acc_scr)

    act = jnp.dot(u_ref[...], h2t_ref[...], preferred_element_type=F32)
    p = jnp.concatenate(
        [_gelu(act[:, lc * LANES:(lc + 1) * LANES]).astype(BF16) * c_ref[lc]
         for lc in range(tt // LANES)], axis=1)
    acc_scr[...] += jnp.dot(vt_ref[...], p, preferred_element_type=F32)

    @pl.when(j == pl.num_programs(1) - 1)
    def _():
        x2 = x1_ref[...] + gt2_ref[...] * acc_scr[...].T
        y_ref[...] = _rms(x2, gf_ref[...])


def _peer(h2t, experts, sel, x1, gt2, g_final, rows_per_group, tt, te):
    kind, u, vt = experts
    n_exp = u.shape[0]
    if kind == "f32":
        coef, u, vt = _coef(sel, te, n_exp, tables=(u, vt))
    else:
        (coef,) = _coef(sel, te, n_exp)
    d, t = h2t.shape
    y = pl.pallas_call(
        _peer_kernel,
        grid=(t // tt, n_exp // te),
        in_specs=[pl.BlockSpec((d, tt), lambda i, j: (0, i)),
                  pl.BlockSpec((te, d), lambda i, j: (j, 0)),
                  pl.BlockSpec((d, te), lambda i, j: (0, j)),
                  pl.BlockSpec((tt // LANES, te, LANES), lambda i, j: (i, j, 0)),
                  pl.BlockSpec((tt, d), lambda i, j: (i, 0)),
                  _mod_spec(gt2, rows_per_group // tt),
                  pl.BlockSpec((1, d), lambda i, j: (0, 0))],
        out_specs=pl.BlockSpec((tt, d), lambda i, j: (i, 0)),
        out_shape=jax.ShapeDtypeStruct((t, d), F32),
        scratch_shapes=[pltpu.VMEM((d, tt), F32)],
        compiler_params=_params(("parallel", "arbitrary")),
        name="peer_dense",
    )(h2t, u, vt, coef, x1, gt2, g_final)
    return y, ("bf16", u, vt)


def _trunk(x, mods, n_groups, window, att_fn, w, experts, wmix, bmix, tm_in, tm_mix, tm_sel, tt, te):
    sh1, sc1, gt1, sh2, sc2, gt2 = mods
    rows = x.shape[0] // n_groups
    q, k, v, u, vn, *wins = _inproj(x, sh1, sc1, w["g1"], w["w_in"], w["lng"], w["lnb"], tm_in,
                                    n_groups, window)
    att = att_fn(q, k, v)
    x1, h2, h2t = _mix(att, u, vn, x, gt1, sh2, sc2, w["g2"], wmix, bmix, w["w_out"], tm_mix)
    sel = _select(h2, w["w_query"], w["sub_keys"], tm_sel)
    y, experts = _peer(h2t, experts, sel, x1, gt2, w["g_final"], rows, tt, te)
    return y, k, v, vn, wins, experts


def kernel(x_prompt, x_sample, cache_k, cache_v, c_prompt, c_sample, w_ada, b_ada, g_norm1, w_in,
           ln_v_g, ln_v_b, w_spatial, b_spatial, w_out, g_norm2, w_query, sub_keys, expert_u,
           expert_v, g_final):
    depth = w_ada.shape[0]
    assert depth == 1, "single-layer trunk"
    batch, seq, d = x_prompt.shape
    n_dec, n_new, _ = x_sample.shape
    buf = cache_k.shape[2]
    assert seq % SPAN == 0 and buf == SPAN and n_new <= SLOTS
    win = min(BRANCHES[-1][0], seq)
    pad_s = BAND
    assert n_dec * SLOTS <= pad_s

    n_c = batch + n_dec
    c_all = jnp.pad(jnp.concatenate([c_prompt, c_sample], axis=0), ((0, -n_c % SUBLANES), (0, 0)))
    mod = _adaln(c_all, w_ada[0], b_ada[0][None, :])
    mods = [mod[:, i * d:(i + 1) * d] for i in range(6)]
    mods_p = [m[:batch, None, :] for m in mods]
    mods_s = [jnp.pad(jnp.repeat(m[batch:n_c], SLOTS, axis=0), ((0, pad_s - n_dec * SLOTS), (0, 0)))[None]
              for m in mods]

    tri = jnp.tril(jnp.ones((BAND, BAND), dtype=bool))
    ws = jnp.where(tri, w_spatial[0], 0.0)
    bs = b_spatial[0]
    wmix_p = ws.astype(BF16)
    bmix_p = jnp.broadcast_to(bs[:, :, None], (N_HEADS, BAND, HEAD_DIM))
    eye = jnp.eye(pad_s // SLOTS, dtype=F32)
    slot_pad = ((0, 0), (0, SLOTS - n_new))
    ws_new = jnp.pad(ws[:, :n_new, :n_new], ((0, 0),) + slot_pad[1:] + slot_pad[1:])
    wmix_s = jax.vmap(lambda m: jnp.kron(eye, m))(ws_new).astype(BF16)
    bmix_s = jnp.broadcast_to(jnp.tile(jnp.pad(bs[:, :n_new], slot_pad), (1, pad_s // SLOTS))[:, :, None],
                              (N_HEADS, pad_s, HEAD_DIM))

    w = dict(g1=g_norm1[0][None, :], w_in=w_in[0].astype(BF16), lng=ln_v_g[0][None, :],
             lnb=ln_v_b[0][None, :], g2=g_norm2[0][None, :], w_out=w_out[0].astype(BF16),
             w_query=w_query[0].astype(BF16), sub_keys=sub_keys[0].astype(BF16),
             g_final=g_final[None, :])

    xp = x_prompt.reshape(batch * seq, d)
    y_p, _, _, _, (k_win, v_win), experts = _trunk(
        xp, mods_p, batch, win, _prompt_attention, w, ("f32", expert_u[0], expert_v[0]), wmix_p, bmix_p,
        tm_in=512, tm_mix=256, tm_sel=512, tt=512, te=1024)

    xs = jnp.pad(x_sample, ((0, 0), (0, SLOTS - n_new), (0, 0))).reshape(n_dec * SLOTS, d)
    xs = jnp.pad(xs, ((0, pad_s - n_dec * SLOTS), (0, 0)))

    def sample_att(q, k, v):
        o = _sample_attention(q[0], k[0], v[0], cache_k[0], cache_v[0], n_new)
        o = o.transpose(1, 0, 2, 3).reshape(N_HEADS, n_dec * SLOTS, HEAD_DIM)
        return jnp.pad(o, ((0, 0), (0, pad_s - n_dec * SLOTS), (0, 0)))[None]

    y_s, k_s, v_s, vn_s, _, _ = _trunk(
        xs, mods_s, 1, 0, sample_att, w, experts, wmix_s, bmix_s,
        tm_in=pad_s, tm_mix=pad_s, tm_sel=pad_s, tt=pad_s, te=2048)

    new_rows = lambda t: t[:n_dec * SLOTS].reshape(n_dec, SLOTS, -1)[:, :n_new]
    new_heads = lambda t: t[0, :, :n_dec * SLOTS].reshape(N_HEADS, n_dec, SLOTS, HEAD_DIM)[:, :, :n_new
                                                                                         ].transpose(1, 2, 0, 3)
    return (y_p.reshape(batch, seq, d),
            new_rows(y_s),
            k_win.reshape(1, batch, win, N_HEADS, HEAD_DIM),
            v_win.reshape(1, batch, win, N_HEADS, HEAD_DIM),
            new_heads(k_s)[None],
            new_heads(v_s)[None],
            new_rows(vn_s)[None])
```

```python
import functools

import numpy as np
import jax
import jax.numpy as jnp
from jax import lax
from jax.experimental import pallas as pl
from jax.experimental.pallas import tpu as pltpu

F32 = jnp.float32
BF16 = jnp.bfloat16

EPS = 1e-6
LANES = 128
SUBLANES = 8
HEAD_DIM = 128
N_HEADS = 8
ATT_WIDTH = N_HEADS * HEAD_DIM
BRANCHES = ((128, 1), (512, 4), (2048, 16))
BAND = 128
SPAN = BAND * BRANCHES[-1][1]
N_KEYS = 128
TOPK = 16
NEG = -1e30
VMEM_LIMIT = 56 * 1024 * 1024
COEF_LANE_CHUNKS = 4
SLOTS = SUBLANES
SAMPLE_ROWS = LANES
EXPERT_TILE = 1024
PROMPT_TILES = dict(tm_in=512, tm_mix=256, tm_sel=512, tt=512)


def _params(sem, flags=None):
    return pltpu.CompilerParams(dimension_semantics=sem, vmem_limit_bytes=VMEM_LIMIT, flags=flags)


def _gelu(x):
    return 0.5 * x * (1.0 + lax.erf(x * (2.0 ** -0.5)))


def _rms(x, g):
    return x * lax.rsqrt(jnp.mean(x * x, axis=-1, keepdims=True) + EPS) * g


def _slope(h):
    return 2.0 ** (-8.0 * (h + 1) / N_HEADS)


def _adaln_kernel(c_ref, w_ref, b_ref, o_ref):
    c = c_ref[...]
    a = c / (1.0 + jnp.exp(-c))
    o_ref[...] = jnp.dot(a.astype(BF16), w_ref[...].astype(BF16),
                         preferred_element_type=F32) + b_ref[...]


def _adaln(c_all, w_ada, b_ada):
    rows, d = c_all.shape
    n = w_ada.shape[1]
    tn = 1024
    return pl.pallas_call(
        _adaln_kernel,
        grid=(n // tn,),
        in_specs=[pl.BlockSpec((rows, d), lambda j: (0, 0)),
                  pl.BlockSpec((d, tn), lambda j: (0, j)),
                  pl.BlockSpec((1, tn), lambda j: (0, j))],
        out_specs=pl.BlockSpec((rows, tn), lambda j: (0, j)),
        out_shape=jax.ShapeDtypeStruct((rows, n), F32),
        compiler_params=_params(("arbitrary",)),
        name="adaln",
    )(c_all, w_ada, b_ada)


def _inproj_kernel(x_ref, sh_ref, sc_ref, g1_ref, w_ref, lng_ref, lnb_ref,
                   q_ref, k_ref, v_ref, u_ref, vn_ref, *rest, win_tiles):
    if win_tiles:
        kwin_ref, vwin_ref, h_scr = rest
        tpg, wt = win_tiles
        in_window = pl.program_id(0) % tpg >= tpg - wt
    else:
        (h_scr,) = rest
    j = pl.program_id(1)

    @pl.when(j == 0)
    def _():
        y = _rms(x_ref[...], g1_ref[...])
        h_scr[...] = (y * (1.0 + sc_ref[...]) + sh_ref[...]).astype(BF16)

    p = jnp.dot(h_scr[...], w_ref[...], preferred_element_type=F32)

    def heads_out(ref, val, win_ref=None):
        for h in range(N_HEADS):
            ref[h] = val[:, h * HEAD_DIM:(h + 1) * HEAD_DIM]
        if win_ref is not None:
            @pl.when(in_window)
            def _():
                for h in range(N_HEADS):
                    win_ref[:, h, :] = val[:, h * HEAD_DIM:(h + 1) * HEAD_DIM]

    @pl.when(j == 0)
    def _():
        heads_out(q_ref, p * (HEAD_DIM ** -0.5))

    @pl.when(j == 1)
    def _():
        heads_out(k_ref, p, kwin_ref if win_tiles else None)

    @pl.when(j == 2)
    def _():
        heads_out(v_ref, p, vwin_ref if win_tiles else None)

    @pl.when(j == 3)
    def _():
        u_ref[...] = _gelu(p).astype(BF16)

    @pl.when(j == 4)
    def _():
        g = _gelu(p)
        mu = jnp.mean(g, axis=-1, keepdims=True)
        gc = g - mu
        var = jnp.mean(gc * gc, axis=-1, keepdims=True)
        vn_ref[...] = gc * lax.rsqrt(var + EPS) * lng_ref[...] + lnb_ref[...]


def _mod_spec(mod, tiles_per_group):
    _, r, d = mod.shape
    return pl.BlockSpec((None, r, d), lambda i, *_: (i // tiles_per_group, 0, 0))


def _inproj(x, sh1, sc1, g1, w_in, lng, lnb, tm, n_groups, window):
    t, d = x.shape
    sec = ATT_WIDTH
    n_sec = w_in.shape[1] // sec
    rows = t // n_groups
    tpg = rows // tm
    tok = lambda i, j: (i, 0)
    const = lambda i, j: (0, 0)
    out_spec = pl.BlockSpec((tm, sec), tok)
    head_spec = pl.BlockSpec((None, N_HEADS, tm, HEAD_DIM), lambda i, j: (i // tpg, 0, i % tpg, 0))
    head_shape = jax.ShapeDtypeStruct((n_groups, N_HEADS, rows, HEAD_DIM), F32)
    out_specs = [head_spec] * 3 + [out_spec] * 2
    out_shape = [head_shape] * 3 + [jax.ShapeDtypeStruct((t, sec), BF16),
                                    jax.ShapeDtypeStruct((t, sec), F32)]
    if window:
        wt = window // tm
        win_spec = pl.BlockSpec((tm, N_HEADS, HEAD_DIM), lambda i, j: (
            (i // tpg) * wt + jnp.maximum(i % tpg - (tpg - wt), 0), 0, 0))
        win_shape = jax.ShapeDtypeStruct((n_groups * window, N_HEADS, HEAD_DIM), F32)
        out_specs += [win_spec] * 2
        out_shape += [win_shape] * 2
    return pl.pallas_call(
        functools.partial(_inproj_kernel, win_tiles=(tpg, window // tm) if window else None),
        grid=(t // tm, n_sec),
        in_specs=[pl.BlockSpec((tm, d), tok),
                  _mod_spec(sh1, tpg), _mod_spec(sc1, tpg),
                  pl.BlockSpec((1, d), const),
                  pl.BlockSpec((d, sec), lambda i, j: (0, j)),
                  pl.BlockSpec((1, sec), const), pl.BlockSpec((1, sec), const)],
        out_specs=out_specs,
        out_shape=out_shape,
        scratch_shapes=[pltpu.VMEM((tm, d), BF16)],
        compiler_params=_params(("arbitrary", "arbitrary")),
        name="inproj",
    )(x, sh1, sc1, g1, w_in, lng, lnb)


def _att_bias():
    a = np.arange(BAND)[:, None]
    kk = np.arange(2 * BAND)[None, :]
    steps = BAND + a - kk
    valid = (steps >= 0) & (steps <= BAND)
    slopes = np.array([_slope(h) for h in range(N_HEADS)])
    out = []
    for _, dil in BRANCHES:
        bias = -slopes[:, None, None] * (dil * steps).astype(np.float64)[None]
        out.append(np.where(valid[None], bias, NEG))
    return jnp.asarray(np.stack(out), dtype=F32)


def _att_prompt_kernel(q_ref, kc_ref, kp_ref, vc_ref, vp_ref, bias_ref, o_ref, m_scr, l_scr, acc_scr):
    first_span = pl.program_id(2) == 0
    span = q_ref.shape[0]
    is_prev = lax.broadcasted_iota(jnp.int32, (BAND, 2 * BAND), 1) < BAND

    def rows_at(start, dil):
        return pl.ds(start, BAND) if dil == 1 else pl.ds(start, BAND, stride=dil)

    for bi, (_, dil) in enumerate(BRANCHES):
        for r in range(dil):
            for n in range(span // (BAND * dil)):
                start = r + dil * BAND * n
                rows = rows_at(start, dil)
                if n > 0:
                    prev = rows_at(start - dil * BAND, dil)
                    kp, vp = kc_ref[prev, :], vc_ref[prev, :]
                else:
                    prev = rows_at(span - dil * BAND + r, dil)
                    kp, vp = kp_ref[prev, :], vp_ref[prev, :]
                kw = jnp.concatenate([kp, kc_ref[rows, :]], axis=0).astype(BF16)
                vw = jnp.concatenate([vp, vc_ref[rows, :]], axis=0).astype(BF16)
                s = lax.dot_general(q_ref[rows, :].astype(BF16), kw, (((1,), (1,)), ((), ())),
                                    preferred_element_type=F32) + bias_ref[bi]
                if n == 0:
                    s = jnp.where(jnp.logical_and(first_span, is_prev), NEG, s)
                m_b = jnp.max(s, axis=-1, keepdims=True)
                p = jnp.exp(s - m_b)
                l_b = jnp.sum(p, axis=-1, keepdims=True)
                pv = jnp.dot(p.astype(BF16), vw, preferred_element_type=F32)
                m_scr[bi, rows, :] = jnp.broadcast_to(m_b, (BAND, LANES))
                l_scr[bi, rows, :] = jnp.broadcast_to(l_b, (BAND, LANES))
                acc_scr[bi, rows, :] = pv
    ms = [m_scr[bi] for bi in range(len(BRANCHES))]
    m_all = functools.reduce(jnp.maximum, ms)
    es = [jnp.exp(m - m_all) for m in ms]
    l_all = sum(e * l_scr[bi] for bi, e in enumerate(es))
    acc_all = sum(e * acc_scr[bi] for bi, e in enumerate(es))
    o_ref[...] = (acc_all / l_all).astype(o_ref.dtype)


def _prompt_attention(q, k, v):
    b, nh, seq, hd = q.shape
    cur = lambda bb, h, s: (bb, h, s, 0)
    prev = lambda bb, h, s: (bb, h, jnp.maximum(s - 1, 0), 0)
    blk = lambda m: pl.BlockSpec((None, None, SPAN, hd), m)
    return pl.pallas_call(
        _att_prompt_kernel,
        grid=(b, nh, seq // SPAN),
        in_specs=[blk(cur), blk(cur), blk(prev), blk(cur), blk(prev),
                  pl.BlockSpec((len(BRANCHES), None, BAND, 2 * BAND), lambda bb, h, s: (0, h, 0, 0))],
        out_specs=blk(cur),
        out_shape=jax.ShapeDtypeStruct(q.shape, BF16),
        scratch_shapes=[pltpu.VMEM((len(BRANCHES), SPAN, LANES), F32)] * 3,
        compiler_params=_params(("parallel", "parallel", "arbitrary")),
        name="attention_prompt",
    )(q, k, k, v, v, _att_bias())


def _att_sample_kernel(q_ref, kn_ref, vn_ref, kt_ref, vt_ref, kr_ref, vr_ref, o_ref, *, n_new):
    o_ref[...] = jnp.zeros_like(o_ref)
    col = lax.broadcasted_iota(jnp.int32, (BAND, 1), 0).astype(F32)
    slot = lax.broadcasted_iota(jnp.int32, (SLOTS, 1), 0)
    tail = kt_ref.shape[0] // N_HEADS
    for h in range(N_HEADS):
        slope = _slope(h)
        knh, vnh = kn_ref[h], vn_ref[h]
        for t in range(n_new):
            q_row = q_ref[h, t:t + 1, :]
            parts = []
            for _, dil in BRANCHES:
                a = t if dil == 1 else 0
                if dil == 1:
                    rows = pl.ds((tail - BAND) * N_HEADS + h, BAND, stride=N_HEADS)
                    kp, vp = kt_ref[rows, :], vt_ref[rows, :]
                elif dil * BAND == tail:
                    rows = pl.ds(t * N_HEADS + h, BAND, stride=dil * N_HEADS)
                    kp, vp = kt_ref[rows, :], vt_ref[rows, :]
                else:
                    kp, vp = kr_ref[:, t * N_HEADS + h, :], vr_ref[:, t * N_HEADS + h, :]
                steps = float(BAND + a) - col
                s_p = jnp.sum(kp * q_row, axis=1, keepdims=True) - (slope * dil) * steps
                if a > 0:
                    s_p = jnp.where(steps <= float(BAND), s_p, NEG)
                s_n = jnp.sum(knh * q_row, axis=1, keepdims=True)
                if dil == 1:
                    s_n = jnp.where(slot <= t, s_n - slope * (t - slot).astype(F32), NEG)
                else:
                    s_n = jnp.where(slot == t, s_n, NEG)
                m = jnp.maximum(jnp.max(s_p, axis=0, keepdims=True), jnp.max(s_n, axis=0, keepdims=True))
                p_p = jnp.exp(s_p - m)
                p_n = jnp.exp(s_n - m)
                l = jnp.sum(p_p, axis=0, keepdims=True) + jnp.sum(p_n, axis=0, keepdims=True)
                acc = (jnp.sum(p_p * vp, axis=0, keepdims=True)
                       + jnp.sum(p_n * vnh, axis=0, keepdims=True))
                parts.append((m, l, acc))
            m_all = functools.reduce(jnp.maximum, [m for m, _, _ in parts])
            l_all = sum(l * jnp.exp(m - m_all) for m, l, _ in parts)
            acc_all = sum(acc * jnp.exp(m - m_all) for m, _, acc in parts)
            o_ref[h, t:t + 1, :] = acc_all / l_all


def _sample_attention(q, k, v, cache_k, cache_v, n_new):
    n_batch, buf, nh, hd = cache_k.shape
    res16 = BRANCHES[-1][1]
    tail = 4 * BAND
    assert buf == res16 * BAND and n_new * nh <= 32
    rows2d = lambda c: c.reshape(n_batch, buf * nh, hd)
    resid = lambda c: c.reshape(n_batch, BAND, res16 * nh, hd)
    new_spec = pl.BlockSpec((nh, SLOTS, hd), lambda b: (0, b, 0))
    tail_spec = pl.BlockSpec((None, tail * nh, hd), lambda b: (b, buf // tail - 1, 0))
    res_spec = pl.BlockSpec((None, BAND, 32, hd), lambda b: (b, 0, 0, 0))
    return pl.pallas_call(
        functools.partial(_att_sample_kernel, n_new=n_new),
        grid=(n_batch,),
        in_specs=[new_spec, new_spec, new_spec, tail_spec, tail_spec, res_spec, res_spec],
        out_specs=pl.BlockSpec((None, nh, SLOTS, hd), lambda b: (b, 0, 0, 0)),
        out_shape=jax.ShapeDtypeStruct((n_batch, nh, SLOTS, hd), F32),
        compiler_params=_params(("parallel",)),
        name="attention_sample",
    )(q, k, v, rows2d(cache_k), rows2d(cache_v), resid(cache_k), resid(cache_v))


def _mix_kernel(att_ref, u_ref, vn_ref, x_ref, gt1_ref, sh2_ref, sc2_ref, g2_ref, wmix_ref, bmix_ref,
                wout_ref, x1_ref, h2_ref, h2t_ref, mix_scr):
    tm = x_ref.shape[0]
    for h in range(N_HEADS):
        mix_scr[:, h * HEAD_DIM:(h + 1) * HEAD_DIM] = att_ref[h].astype(BF16)
    for c in range(tm // BAND):
        rows = slice(c * BAND, (c + 1) * BAND)
        for g in range(N_HEADS):
            cs = slice(g * HEAD_DIM, (g + 1) * HEAD_DIM)
            m = jnp.dot(wmix_ref[g], vn_ref[rows, cs].astype(BF16),
                        preferred_element_type=F32) + bmix_ref[g]
            mix_scr[rows, ATT_WIDTH + g * HEAD_DIM:ATT_WIDTH + (g + 1) * HEAD_DIM] = (
                u_ref[rows, cs].astype(F32) * m).astype(BF16)
    y = jnp.dot(mix_scr[...], wout_ref[...], preferred_element_type=F32)
    x1 = x_ref[...] + gt1_ref[...] * y
    x1_ref[...] = x1
    h2 = _rms(x1, g2_ref[...]) * (1.0 + sc2_ref[...]) + sh2_ref[...]
    h2_ref[...] = h2.astype(BF16)
    h2t_ref[...] = h2.T.astype(BF16)


def _mix(att, u, vn, x, gt1, sh2, sc2, g2, wmix, bmix, w_out, tm):
    t, d = x.shape
    tpg = att.shape[2] // tm
    tok = lambda i: (i, 0)
    c2 = lambda i: (0, 0)
    c3 = lambda i: (0, 0, 0)
    return pl.pallas_call(
        _mix_kernel,
        grid=(t // tm,),
        in_specs=[pl.BlockSpec((None, N_HEADS, tm, HEAD_DIM), lambda i: (i // tpg, 0, i % tpg, 0)),
                  pl.BlockSpec((tm, ATT_WIDTH), tok), pl.BlockSpec((tm, ATT_WIDTH), tok),
                  pl.BlockSpec((tm, d), tok),
                  _mod_spec(gt1, tpg), _mod_spec(sh2, tpg), _mod_spec(sc2, tpg),
                  pl.BlockSpec((1, d), c2),
                  pl.BlockSpec(wmix.shape, c3), pl.BlockSpec(bmix.shape, c3),
                  pl.BlockSpec(w_out.shape, c2)],
        out_specs=[pl.BlockSpec((tm, d), tok), pl.BlockSpec((tm, d), tok),
                   pl.BlockSpec((d, tm), lambda i: (0, i))],
        out_shape=[jax.ShapeDtypeStruct((t, d), F32),
                   jax.ShapeDtypeStruct((t, d), BF16),
                   jax.ShapeDtypeStruct((d, t), BF16)],
        scratch_shapes=[pltpu.VMEM((tm, d), BF16)],
        compiler_params=_params(("parallel",)),
        name="mix_outproj",
    )(att, u, vn, x, gt1, sh2, sc2, g2, wmix, bmix, w_out)


def _cand_layout():
    rows = []
    rows += [(0, j) for j in range(16)]
    rows += [(1, j) for j in range(8)]
    rows += [(8 + r, 0) for r in range(8)]
    rows += [(2, j) if j < 5 else None for j in range(8)]
    rows += [(3, j) if j < 4 else None for j in range(8)]
    rows += [(4, j) if j < 3 else None for j in range(8)]
    rows += [(r, 0) if 5 <= r < 8 else None for r in range(8)]
    rows += [(r, 1) if 5 <= r < 8 else None for r in range(8)]
    return rows


N_CAND_ROWS = 72
BIG_FLAT = 1e9
NO_EXPERT = 1e30


def _cand_flat():
    flat = np.full((N_CAND_ROWS, LANES), BIG_FLAT, np.float32)
    for r, cell in enumerate(_cand_layout()):
        if cell is not None:
            flat[r, :] = 16 * cell[0] + cell[1]
    return jnp.asarray(flat)


def _top16_ranked(s, row_id, top_ref):
    rank = jnp.full(s.shape, float(TOPK), F32)
    for it in range(TOPK):
        m = jnp.max(s, axis=0, keepdims=True)
        idx = jnp.min(jnp.where(s == m, row_id, float(N_KEYS)), axis=0, keepdims=True)
        sel = row_id == idx
        rank = jnp.where(sel, float(it), rank)
        s = jnp.where(sel, -jnp.inf, s)
        top_ref[it:it + 1, :] = m
    return rank


def _sort16_network():
    def merge(lo, hi, r):
        step = r * 2
        if step < hi - lo:
            yield from merge(lo, hi, step)
            yield from merge(lo + r, hi, step)
            yield from [(i, i + r) for i in range(lo + r, hi - r, step)]
        else:
            yield (lo, lo + r)

    def sort(lo, hi):
        if hi > lo:
            mid = lo + (hi - lo) // 2
            yield from sort(lo, mid)
            yield from sort(mid + 1, hi)
            yield from merge(lo, hi, 1)

    return list(sort(0, TOPK - 1))


def _top16_sorted(s, top_ref):
    n = N_KEYS // SUBLANES
    cols = [s[SUBLANES * v:SUBLANES * (v + 1), :] for v in range(n)]
    for i, j in _sort16_network():
        cols[i], cols[j] = jnp.maximum(cols[i], cols[j]), jnp.minimum(cols[i], cols[j])
    ties = jnp.zeros((1, s.shape[1]), F32)
    prev = None
    for it in range(TOPK + 1):
        m = jnp.max(cols[0], axis=0, keepdims=True)
        if prev is not None:
            ties = ties + jnp.where(m == prev, 1.0, 0.0)
        prev = m
        if it == TOPK:
            break
        sel = cols[0] == m
        ties = ties + jnp.sum(jnp.where(sel, 1.0, 0.0), axis=0, keepdims=True) - 1.0
        top_ref[it:it + 1, :] = m
        for v in range(TOPK - it):
            nxt = cols[v + 1] if v + 1 < n else jnp.full_like(cols[v], -jnp.inf)
            cols[v] = jnp.where(sel, nxt, cols[v])
    return ties


def _select_block(h, lc, toks, exact, keys_ref, qp_scr, top1_scr, top2_scr, flat, row_id, row8,
                  thr_ref, g1_ref, s2_ref, e2_ref):
    valid = flat < BIG_FLAT
    sides = []
    for p, top_scr in ((0, top1_scr), (1, top2_scr)):
        col = pl.multiple_of((2 * h + p) * N_KEYS, N_KEYS)
        s = lax.dot_general(keys_ref[p], qp_scr[toks, pl.ds(col, N_KEYS)],
                            (((1,), (1,)), ((), ())), preferred_element_type=F32)
        sides.append((s, _top16_ranked(s, row_id, top_scr) if exact else _top16_sorted(s, top_scr)))
    (s1, info1), (s2, info2) = sides
    a1 = top1_scr[...]
    a2 = top2_scr[...]
    v = jnp.concatenate([
        a1[0:1] + a2[0:16],
        a1[1:2] + a2[0:8],
        a1[8:16] + a2[0:1],
        a1[2:3] + a2[0:8],
        a1[3:4] + a2[0:8],
        a1[4:5] + a2[0:8],
        a1[0:8] + a2[0:1],
        a1[0:8] + a2[1:2]], axis=0)
    vw = jnp.where(valid, v, -jnp.inf)
    selm = jnp.zeros(v.shape, F32)
    for _ in range(TOPK):
        m = jnp.max(vw, axis=0, keepdims=True)
        sel = vw == m
        if exact:
            f = jnp.min(jnp.where(sel, flat, BIG_FLAT), axis=0, keepdims=True)
            sel = flat == f
        selm = jnp.where(sel, 1.0, selm)
        vw = jnp.where(sel, -jnp.inf, vw)
    vmax = a1[0:1] + a2[0:1]
    z = jnp.sum(jnp.where(selm > 0.0, jnp.exp(v - vmax), 0.0), axis=0, keepdims=True)
    cnt_lo = selm[56:64] + selm[64:72]
    for i, (lo, hi) in enumerate(((0, 16), (16, 24), (32, 40), (40, 48), (48, 56))):
        ci = jnp.sum(selm[lo:hi], axis=0, keepdims=True)
        cnt_lo = jnp.where(row8 == i, ci, cnt_lo)
    cnt = jnp.concatenate([cnt_lo, selm[24:32]], axis=0)
    if exact:
        thr = jnp.where(cnt > 0.0, 0.5 - cnt, NO_EXPERT)
        key1, key2 = info1, -info2
        match = [float(i) for i in range(TOPK)]
    else:
        thr = jnp.full(cnt.shape, NO_EXPERT, F32)
        for j in range(TOPK):
            thr = jnp.where(cnt == float(j + 1), a2[j:j + 1], thr)
        key1, key2 = s1, s2
        match = [a1[i:i + 1] for i in range(TOPK)]
    thr_a = jnp.full((N_KEYS, LANES), NO_EXPERT, F32)
    for i in range(TOPK):
        thr_a = jnp.where(key1 == match[i], thr[i:i + 1], thr_a)
    thr_ref[h, lc] = thr_a
    g1_ref[h, lc] = jnp.exp(s1 - a1[0:1]) / z
    s2_ref[h, lc] = key2
    e2_ref[h, lc] = jnp.exp(s2 - a2[0:1])
    if exact:
        return None
    n3 = jnp.sum(selm, axis=0, keepdims=True) - float(TOPK)
    return jnp.max(info1 + info2 + n3)


def _select_kernel(h2_ref, wq_ref, keys_ref, flat_ref, thr_ref, g1_ref, s2_ref, e2_ref,
                   qp_scr, top1_scr, top2_scr):
    tm = h2_ref.shape[0]
    qp_scr[...] = jnp.dot(h2_ref[...], wq_ref[...], preferred_element_type=F32).astype(BF16)
    row_id = lax.broadcasted_iota(jnp.int32, (N_KEYS, LANES), 0).astype(F32)
    row8 = lax.broadcasted_iota(jnp.int32, (8, LANES), 0)
    flat = flat_ref[...]
    blocks = []
    for lc in range(tm // LANES):
        toks = slice(lc * LANES, (lc + 1) * LANES)
        blocks.append(functools.partial(
            _select_block, lc=lc, toks=toks, keys_ref=keys_ref, qp_scr=qp_scr, top1_scr=top1_scr.at[lc],
            top2_scr=top2_scr.at[lc], flat=flat, row_id=row_id, row8=row8, thr_ref=thr_ref,
            g1_ref=g1_ref, s2_ref=s2_ref, e2_ref=e2_ref))

    def head_pair_body(hh, carry):
        work = [(block, h) for h in (2 * hh, 2 * hh + 1) for block in blocks]
        ties = [block(h, exact=False) for block, h in work]
        for (block, h), tie in zip(work, ties):
            @pl.when(tie > 0.5)
            def _(block=block, h=h):
                block(h, exact=True)

        return carry

    lax.fori_loop(0, N_HEADS // 2, head_pair_body, 0)


def _select(h2, w_query, sub_keys, tm):
    t, d = h2.shape
    out_spec = pl.BlockSpec((N_HEADS, tm // LANES, N_KEYS, LANES), lambda i: (0, i, 0, 0))
    slab = lambda dt: jax.ShapeDtypeStruct((N_HEADS, t // LANES, N_KEYS, LANES), dt)
    return pl.pallas_call(
        _select_kernel,
        grid=(t // tm,),
        in_specs=[pl.BlockSpec((tm, d), lambda i: (i, 0)),
                  pl.BlockSpec(w_query.shape, lambda i: (0, 0)),
                  pl.BlockSpec(sub_keys.shape, lambda i: (0, 0, 0)),
                  pl.BlockSpec((N_CAND_ROWS, LANES), lambda i: (0, 0))],
        out_specs=[out_spec] * 4,
        out_shape=[slab(F32)] * 4,
        scratch_shapes=[pltpu.VMEM((tm, w_query.shape[1]), BF16),
                        pltpu.VMEM((tm // LANES, TOPK, LANES), F32),
                        pltpu.VMEM((tm // LANES, TOPK, LANES), F32)],
        compiler_params=_params(("parallel",)),
        name="peer_select",
    )(h2, w_query, sub_keys, _cand_flat())


def _coef_kernel(thr_ref, g1_ref, s2_ref, e2_ref, *rest, cast):
    if cast:
        u_ref, v_ref, c_ref, u16_ref, vt16_ref = rest
        u16_ref[...] = u_ref[...].astype(BF16)
        step = pl.program_id(0) * pl.num_programs(1) + pl.program_id(1)

        @pl.when(step % 2 == 0)
        def _():
            vt16_ref[...] = v_ref[...].T.astype(BF16)
    else:
        (c_ref,) = rest
    n_l, n_a = thr_ref.shape[1], thr_ref.shape[2]
    for lc in range(n_l):
        for ai in range(n_a):
            rows = slice(ai * N_KEYS, (ai + 1) * N_KEYS)
            coef = jnp.zeros((N_KEYS, LANES), F32)
            for h in range(N_HEADS):
                thr = thr_ref[h, lc, ai:ai + 1, :]
                g1 = g1_ref[h, lc, ai:ai + 1, :]
                coef = coef + jnp.where(s2_ref[h, lc] >= thr, e2_ref[h, lc], 0.0) * g1
            c_ref[lc, rows, :] = coef.astype(BF16)


def _coef(sel, te, n_exp, tables=None):
    n_lc = sel[0].shape[1]
    n_l = min(n_lc, COEF_LANE_CHUNKS)
    grid = (n_lc // n_l, n_exp // te)
    a_spec = pl.BlockSpec((N_HEADS, n_l, te // N_KEYS, LANES), lambda i, j: (0, i, j, 0))
    b_spec = pl.BlockSpec((N_HEADS, n_l, N_KEYS, LANES), lambda i, j: (0, i, 0, 0))
    in_specs = [a_spec, a_spec, b_spec, b_spec]
    out_specs = [pl.BlockSpec((n_l, te, LANES), lambda i, j: (i, j, 0))]
    out_shape = [jax.ShapeDtypeStruct((n_lc, n_exp, LANES), BF16)]
    operands = list(sel)
    if tables is not None:
        d = tables[0].shape[1]
        steps = grid[0] * grid[1]
        u_rows = n_exp // steps
        assert u_rows * steps == n_exp and 2 * u_rows == LANES
        step = lambda i, j: i * grid[1] + j
        in_specs += [pl.BlockSpec((u_rows, d), lambda i, j: (step(i, j), 0)),
                     pl.BlockSpec((LANES, d), lambda i, j: (step(i, j) // 2, 0))]
        out_specs += [pl.BlockSpec((u_rows, d), lambda i, j: (step(i, j), 0)),
                      pl.BlockSpec((d, LANES), lambda i, j: (0, step(i, j) // 2))]
        out_shape += [jax.ShapeDtypeStruct((n_exp, d), BF16), jax.ShapeDtypeStruct((d, n_exp), BF16)]
        operands += list(tables)
    return pl.pallas_call(
        functools.partial(_coef_kernel, cast=tables is not None),
        grid=grid,
        in_specs=in_specs,
        out_specs=out_specs,
        out_shape=out_shape,
        compiler_params=_params(("arbitrary", "arbitrary")),
        name="peer_coef",
    )(*operands)


def _peer_kernel(h2t_ref, u_ref, vt_ref, c_ref, x1_ref, gt2_ref, gf_ref, y_ref, acc_scr):
    j = pl.program_id(1)
    tt = h2t_ref.shape[1]

    @pl.when(j == 0)
    def _():
        acc_scr[...] = jnp.zeros_like(acc_scr)

    act = jnp.dot(u_ref[...], h2t_ref[...], preferred_element_type=F32)
    p = jnp.concatenate(
        [_gelu(act[:, lc * LANES:(lc + 1) * LANES]).astype(BF16) * c_ref[lc]
         for lc in range(tt // LANES)], axis=1)
    acc_scr[...] += jnp.dot(vt_ref[...], p, preferred_element_type=F32)

    @pl.when(j == pl.num_programs(1) - 1)
    def _():
        x2 = x1_ref[...] + gt2_ref[...] * acc_scr[...].T
        y_ref[...] = _rms(x2, gf_ref[...])


def _peer(h2t, experts, sel, x1, gt2, g_final, rows_per_group, tt, te):
    kind, u, vt = experts
    n_exp = u.shape[0]
    if kind == "f32":
        coef, u, vt = _coef(sel, te, n_exp, tables=(u, vt))
    else:
        (coef,) = _coef(sel, te, n_exp)
    d, t = h2t.shape
    y = pl.pallas_call(
        _peer_kernel,
        grid=(t // tt, n_exp // te),
        in_specs=[pl.BlockSpec((d, tt), lambda i, j: (0, i)),
                  pl.BlockSpec((te, d), lambda i, j: (j, 0)),
                  pl.BlockSpec((d, te), lambda i, j: (0, j)),
                  pl.BlockSpec((tt // LANES, te, LANES), lambda i, j: (i, j, 0)),
                  pl.BlockSpec((tt, d), lambda i, j: (i, 0)),
                  _mod_spec(gt2, rows_per_group // tt),
                  pl.BlockSpec((1, d), lambda i, j: (0, 0))],
        out_specs=pl.BlockSpec((tt, d), lambda i, j: (i, 0)),
        out_shape=jax.ShapeDtypeStruct((t, d), F32),
        scratch_shapes=[pltpu.VMEM((d, tt), F32)],
        compiler_params=_params(("parallel", "arbitrary")),
        name="peer_dense",
    )(h2t, u, vt, coef, x1, gt2, g_final)
    return y, ("bf16", u, vt)


def _trunk(x, mods, n_groups, window, att_fn, w, experts, wmix, bmix, tm_in, tm_mix, tm_sel, tt, te):
    sh1, sc1, gt1, sh2, sc2, gt2 = mods
    rows = x.shape[0] // n_groups
    q, k, v, u, vn, *wins = _inproj(x, sh1, sc1, w["g1"], w["w_in"], w["lng"], w["lnb"], tm_in,
                                    n_groups, window)
    att = att_fn(q, k, v)
    x1, h2, h2t = _mix(att, u, vn, x, gt1, sh2, sc2, w["g2"], wmix, bmix, w["w_out"], tm_mix)
    sel = _select(h2, w["w_query"], w["sub_keys"], tm_sel)
    y, experts = _peer(h2t, experts, sel, x1, gt2, w["g_final"], rows, tt, te)
    return y, k, v, vn, wins, experts


def kernel(x_prompt, x_sample, cache_k, cache_v, c_prompt, c_sample, w_ada, b_ada, g_norm1, w_in,
           ln_v_g, ln_v_b, w_spatial, b_spatial, w_out, g_norm2, w_query, sub_keys, expert_u,
           expert_v, g_final):
    depth = w_ada.shape[0]
    assert depth == 1, "single-layer trunk"
    batch, seq, d = x_prompt.shape
    n_dec, n_new, _ = x_sample.shape
    buf = cache_k.shape[2]
    assert seq % SPAN == 0 and buf == SPAN and n_new <= SLOTS
    win = min(BRANCHES[-1][0], seq)
    pad_s = SAMPLE_ROWS
    assert n_dec * SLOTS <= pad_s

    n_c = batch + n_dec
    c_all = jnp.pad(jnp.concatenate([c_prompt, c_sample], axis=0), ((0, -n_c % SUBLANES), (0, 0)))
    mod = _adaln(c_all, w_ada[0], b_ada[0][None, :])
    mods = [mod[:, i * d:(i + 1) * d] for i in range(6)]
    mods_p = [m[:batch, None, :] for m in mods]
    mods_s = [jnp.pad(jnp.repeat(m[batch:n_c], SLOTS, axis=0), ((0, pad_s - n_dec * SLOTS), (0, 0)))[None]
              for m in mods]

    tri = jnp.tril(jnp.ones((BAND, BAND), dtype=bool))
    ws = jnp.where(tri, w_spatial[0], 0.0)
    bs = b_spatial[0]
    wmix_p = ws.astype(BF16)
    bmix_p = jnp.broadcast_to(bs[:, :, None], (N_HEADS, BAND, HEAD_DIM))
    eye = jnp.eye(pad_s // SLOTS, dtype=F32)
    slot_pad = ((0, 0), (0, SLOTS - n_new))
    ws_new = jnp.pad(ws[:, :n_new, :n_new], ((0, 0),) + slot_pad[1:] + slot_pad[1:])
    wmix_s = jax.vmap(lambda m: jnp.kron(eye, m))(ws_new).astype(BF16)
    bmix_s = jnp.broadcast_to(jnp.tile(jnp.pad(bs[:, :n_new], slot_pad), (1, pad_s // SLOTS))[:, :, None],
                              (N_HEADS, pad_s, HEAD_DIM))

    w = dict(g1=g_norm1[0][None, :], w_in=w_in[0].astype(BF16), lng=ln_v_g[0][None, :],
             lnb=ln_v_b[0][None, :], g2=g_norm2[0][None, :], w_out=w_out[0].astype(BF16),
             w_query=w_query[0].astype(BF16), sub_keys=sub_keys[0].astype(BF16),
             g_final=g_final[None, :])

    xp = x_prompt.reshape(batch * seq, d)
    y_p, _, _, _, (k_win, v_win), experts = _trunk(
        xp, mods_p, batch, win, _prompt_attention, w, ("f32", expert_u[0], expert_v[0]), wmix_p, bmix_p,
        te=EXPERT_TILE, **PROMPT_TILES)

    xs = jnp.pad(x_sample, ((0, 0), (0, SLOTS - n_new), (0, 0))).reshape(n_dec * SLOTS, d)
    xs = jnp.pad(xs, ((0, pad_s - n_dec * SLOTS), (0, 0)))

    def sample_att(q, k, v):
        o = _sample_attention(q[0], k[0], v[0], cache_k[0], cache_v[0], n_new)
        o = o.transpose(1, 0, 2, 3).reshape(N_HEADS, n_dec * SLOTS, HEAD_DIM)
        return jnp.pad(o, ((0, 0), (0, pad_s - n_dec * SLOTS), (0, 0)))[None]

    y_s, k_s, v_s, vn_s, _, _ = _trunk(
        xs, mods_s, 1, 0, sample_att, w, experts, wmix_s, bmix_s,
        tm_in=pad_s, tm_mix=pad_s, tm_sel=pad_s, tt=pad_s, te=EXPERT_TILE)

    new_rows = lambda t: t[:n_dec * SLOTS].reshape(n_dec, SLOTS, -1)[:, :n_new]
    new_heads = lambda t: t[0, :, :n_dec * SLOTS].reshape(N_HEADS, n_dec, SLOTS, HEAD_DIM)[:, :, :n_new
                                                                                         ].transpose(1, 2, 0, 3)
    return (y_p.reshape(batch, seq, d),
            new_rows(y_s),
            k_win.reshape(1, batch, win, N_HEADS, HEAD_DIM),
            v_win.reshape(1, batch, win, N_HEADS, HEAD_DIM),
            new_heads(k_s)[None],
            new_heads(v_s)[None],
            new_rows(vn_s)[None])
```

```python
import functools

import numpy as np
import jax
import jax.numpy as jnp
from jax import lax
from jax.experimental import pallas as pl
from jax.experimental.pallas import tpu as pltpu

F32 = jnp.float32
BF16 = jnp.bfloat16

EPS = 1e-6
LANES = 128
SUBLANES = 8
HEAD_DIM = 128
N_HEADS = 8
ATT_WIDTH = N_HEADS * HEAD_DIM
BRANCHES = ((128, 1), (512, 4), (2048, 16))
BAND = 128
SPAN = BAND * BRANCHES[-1][1]
N_KEYS = 128
TOPK = 16
NEG = -1e30
VMEM_LIMIT = 56 * 1024 * 1024
COEF_LANE_CHUNKS = 8
SLOTS = SUBLANES
SAMPLE_ROWS = LANES
EXPERT_TILE = SUBLANES * N_KEYS
PROMPT_TILES = dict(tm_in=512, tm_mix=256, tm_sel=512, tt=512)


def _params(sem):
    return pltpu.CompilerParams(dimension_semantics=sem, vmem_limit_bytes=VMEM_LIMIT)


def _gelu(x):
    return 0.5 * x * (1.0 + lax.erf(x * (2.0 ** -0.5)))


def _rms(x, g):
    return x * lax.rsqrt(jnp.mean(x * x, axis=-1, keepdims=True) + EPS) * g


def _slope(h):
    return 2.0 ** (-8.0 * (h + 1) / N_HEADS)


def _adaln_kernel(c_ref, w_ref, b_ref, o_ref):
    c = c_ref[...]
    a = c / (1.0 + jnp.exp(-c))
    o_ref[...] = jnp.dot(a.astype(BF16), w_ref[...].astype(BF16),
                         preferred_element_type=F32) + b_ref[...]


def _adaln(c_all, w_ada, b_ada):
    rows, d = c_all.shape
    n = w_ada.shape[1]
    tn = 1024
    return pl.pallas_call(
        _adaln_kernel,
        grid=(n // tn,),
        in_specs=[pl.BlockSpec((rows, d), lambda j: (0, 0)),
                  pl.BlockSpec((d, tn), lambda j: (0, j)),
                  pl.BlockSpec((1, tn), lambda j: (0, j))],
        out_specs=pl.BlockSpec((rows, tn), lambda j: (0, j)),
        out_shape=jax.ShapeDtypeStruct((rows, n), F32),
        compiler_params=_params(("arbitrary",)),
        name="adaln",
    )(c_all, w_ada, b_ada)


def _inproj_kernel(x_ref, sh_ref, sc_ref, g1_ref, w_ref, lng_ref, lnb_ref,
                   q_ref, k_ref, v_ref, u_ref, vn_ref, *rest, win_tiles):
    if win_tiles:
        kwin_ref, vwin_ref, h_scr = rest
        tpg, wt = win_tiles
        in_window = pl.program_id(0) % tpg >= tpg - wt
    else:
        (h_scr,) = rest
    j = pl.program_id(1)

    @pl.when(j == 0)
    def _():
        y = _rms(x_ref[...], g1_ref[...])
        h_scr[...] = (y * (1.0 + sc_ref[...]) + sh_ref[...]).astype(BF16)

    p = jnp.dot(h_scr[...], w_ref[...], preferred_element_type=F32)

    def heads_out(ref, val, win_ref=None):
        for h in range(N_HEADS):
            ref[h] = val[:, h * HEAD_DIM:(h + 1) * HEAD_DIM]
        if win_ref is not None:
            @pl.when(in_window)
            def _():
                for h in range(N_HEADS):
                    win_ref[:, h, :] = val[:, h * HEAD_DIM:(h + 1) * HEAD_DIM]

    @pl.when(j == 0)
    def _():
        heads_out(q_ref, p * (HEAD_DIM ** -0.5))

    @pl.when(j == 1)
    def _():
        heads_out(k_ref, p, kwin_ref if win_tiles else None)

    @pl.when(j == 2)
    def _():
        heads_out(v_ref, p, vwin_ref if win_tiles else None)

    @pl.when(j == 3)
    def _():
        u_ref[...] = _gelu(p).astype(BF16)

    @pl.when(j == 4)
    def _():
        g = _gelu(p)
        mu = jnp.mean(g, axis=-1, keepdims=True)
        gc = g - mu
        var = jnp.mean(gc * gc, axis=-1, keepdims=True)
        vn_ref[...] = gc * lax.rsqrt(var + EPS) * lng_ref[...] + lnb_ref[...]


def _mod_spec(mod, tiles_per_group):
    _, r, d = mod.shape
    return pl.BlockSpec((None, r, d), lambda i, *_: (i // tiles_per_group, 0, 0))


def _inproj(x, sh1, sc1, g1, w_in, lng, lnb, tm, n_groups, window):
    t, d = x.shape
    sec = ATT_WIDTH
    n_sec = w_in.shape[1] // sec
    rows = t // n_groups
    tpg = rows // tm
    tok = lambda i, j: (i, 0)
    const = lambda i, j: (0, 0)
    out_spec = pl.BlockSpec((tm, sec), tok)
    head_spec = pl.BlockSpec((None, N_HEADS, tm, HEAD_DIM), lambda i, j: (i // tpg, 0, i % tpg, 0))
    head_shape = jax.ShapeDtypeStruct((n_groups, N_HEADS, rows, HEAD_DIM), F32)
    out_specs = [head_spec] * 3 + [out_spec] * 2
    out_shape = [head_shape] * 3 + [jax.ShapeDtypeStruct((t, sec), BF16),
                                    jax.ShapeDtypeStruct((t, sec), F32)]
    if window:
        wt = window // tm
        win_spec = pl.BlockSpec((tm, N_HEADS, HEAD_DIM), lambda i, j: (
            (i // tpg) * wt + jnp.maximum(i % tpg - (tpg - wt), 0), 0, 0))
        win_shape = jax.ShapeDtypeStruct((n_groups * window, N_HEADS, HEAD_DIM), F32)
        out_specs += [win_spec] * 2
        out_shape += [win_shape] * 2
    return pl.pallas_call(
        functools.partial(_inproj_kernel, win_tiles=(tpg, window // tm) if window else None),
        grid=(t // tm, n_sec),
        in_specs=[pl.BlockSpec((tm, d), tok),
                  _mod_spec(sh1, tpg), _mod_spec(sc1, tpg),
                  pl.BlockSpec((1, d), const),
                  pl.BlockSpec((d, sec), lambda i, j: (0, j)),
                  pl.BlockSpec((1, sec), const), pl.BlockSpec((1, sec), const)],
        out_specs=out_specs,
        out_shape=out_shape,
        scratch_shapes=[pltpu.VMEM((tm, d), BF16)],
        compiler_params=_params(("arbitrary", "arbitrary")),
        name="inproj",
    )(x, sh1, sc1, g1, w_in, lng, lnb)


def _att_bias():
    a = np.arange(BAND)[:, None]
    kk = np.arange(2 * BAND)[None, :]
    steps = BAND + a - kk
    valid = (steps >= 0) & (steps <= BAND)
    slopes = np.array([_slope(h) for h in range(N_HEADS)])
    out = []
    for _, dil in BRANCHES:
        bias = -slopes[:, None, None] * (dil * steps).astype(np.float64)[None]
        out.append(np.where(valid[None], bias, NEG))
    return jnp.asarray(np.stack(out), dtype=F32)


def _att_prompt_kernel(q_ref, kc_ref, kp_ref, vc_ref, vp_ref, bias_ref, *rest):
    n_w = (len(rest) - 4) // 2
    w_refs, o_ref, w16_refs = rest[:n_w], rest[n_w], rest[n_w + 1:2 * n_w + 1]
    m_scr, l_scr, acc_scr = rest[2 * n_w + 1:]
    for w_ref, w16_ref in zip(w_refs, w16_refs):
        w16_ref[...] = w_ref[...].astype(BF16)
    first_span = pl.program_id(2) == 0
    span = q_ref.shape[0]
    is_prev = lax.broadcasted_iota(jnp.int32, (BAND, 2 * BAND), 1) < BAND

    def rows_at(start, dil):
        return pl.ds(start, BAND) if dil == 1 else pl.ds(start, BAND, stride=dil)

    for bi, (_, dil) in enumerate(BRANCHES):
        for r in range(dil):
            for n in range(span // (BAND * dil)):
                start = r + dil * BAND * n
                rows = rows_at(start, dil)
                if n > 0:
                    prev = rows_at(start - dil * BAND, dil)
                    kp, vp = kc_ref[prev, :], vc_ref[prev, :]
                else:
                    prev = rows_at(span - dil * BAND + r, dil)
                    kp, vp = kp_ref[prev, :], vp_ref[prev, :]
                kw = jnp.concatenate([kp, kc_ref[rows, :]], axis=0).astype(BF16)
                vw = jnp.concatenate([vp, vc_ref[rows, :]], axis=0).astype(BF16)
                s = lax.dot_general(q_ref[rows, :].astype(BF16), kw, (((1,), (1,)), ((), ())),
                                    preferred_element_type=F32) + bias_ref[bi]
                if n == 0:
                    s = jnp.where(jnp.logical_and(first_span, is_prev), NEG, s)
                m_b = jnp.max(s, axis=-1, keepdims=True)
                p = jnp.exp(s - m_b)
                l_b = jnp.sum(p, axis=-1, keepdims=True)
                pv = jnp.dot(p.astype(BF16), vw, preferred_element_type=F32)
                m_scr[bi, rows, :] = jnp.broadcast_to(m_b, (BAND, LANES))
                l_scr[bi, rows, :] = jnp.broadcast_to(l_b, (BAND, LANES))
                acc_scr[bi, rows, :] = pv
    ms = [m_scr[bi] for bi in range(len(BRANCHES))]
    m_all = functools.reduce(jnp.maximum, ms)
    es = [jnp.exp(m - m_all) for m in ms]
    l_all = sum(e * l_scr[bi] for bi, e in enumerate(es))
    acc_all = sum(e * acc_scr[bi] for bi, e in enumerate(es))
    o_ref[...] = (acc_all / l_all).astype(o_ref.dtype)


def _prompt_attention(q, k, v, weights=()):
    b, nh, seq, hd = q.shape
    n_span = seq // SPAN
    steps = b * nh * n_span
    cur = lambda bb, h, s: (bb, h, s, 0)
    prev = lambda bb, h, s: (bb, h, jnp.maximum(s - 1, 0), 0)
    blk = lambda m: pl.BlockSpec((None, None, SPAN, hd), m)
    step = lambda bb, h, s: ((bb * nh + h) * n_span + s, 0)
    w_specs = []
    for wm in weights:
        assert wm.shape[0] % (16 * steps) == 0
        w_specs.append(pl.BlockSpec((wm.shape[0] // steps, wm.shape[1]), step))
    return pl.pallas_call(
        _att_prompt_kernel,
        grid=(b, nh, n_span),
        in_specs=[blk(cur), blk(cur), blk(prev), blk(cur), blk(prev),
                  pl.BlockSpec((len(BRANCHES), None, BAND, 2 * BAND), lambda bb, h, s: (0, h, 0, 0))
                  ] + w_specs,
        out_specs=[blk(cur)] + w_specs,
        out_shape=[jax.ShapeDtypeStruct(q.shape, BF16)]
        + [jax.ShapeDtypeStruct(wm.shape, BF16) for wm in weights],
        scratch_shapes=[pltpu.VMEM((len(BRANCHES), SPAN, LANES), F32)] * 3,
        compiler_params=_params(("parallel", "parallel", "arbitrary")),
        name="attention_prompt",
    )(q, k, k, v, v, _att_bias(), *weights)


def _att_sample_kernel(q_ref, kn_ref, vn_ref, kt_ref, vt_ref, kr_ref, vr_ref, o_ref, *, n_new):
    o_ref[...] = jnp.zeros_like(o_ref)
    col = lax.broadcasted_iota(jnp.int32, (BAND, 1), 0).astype(F32)
    slot = lax.broadcasted_iota(jnp.int32, (SLOTS, 1), 0)
    tail = kt_ref.shape[0] // N_HEADS
    for h in range(N_HEADS):
        slope = _slope(h)
        knh, vnh = kn_ref[h], vn_ref[h]
        for t in range(n_new):
            q_row = q_ref[h, t:t + 1, :]
            parts = []
            for _, dil in BRANCHES:
                a = t if dil == 1 else 0
                if dil == 1:
                    rows = pl.ds((tail - BAND) * N_HEADS + h, BAND, stride=N_HEADS)
                    kp, vp = kt_ref[rows, :], vt_ref[rows, :]
                elif dil * BAND == tail:
                    rows = pl.ds(t * N_HEADS + h, BAND, stride=dil * N_HEADS)
                    kp, vp = kt_ref[rows, :], vt_ref[rows, :]
                else:
                    kp, vp = kr_ref[:, t * N_HEADS + h, :], vr_ref[:, t * N_HEADS + h, :]
                steps = float(BAND + a) - col
                s_p = jnp.sum(kp * q_row, axis=1, keepdims=True) - (slope * dil) * steps
                if a > 0:
                    s_p = jnp.where(steps <= float(BAND), s_p, NEG)
                s_n = jnp.sum(knh * q_row, axis=1, keepdims=True)
                if dil == 1:
                    s_n = jnp.where(slot <= t, s_n - slope * (t - slot).astype(F32), NEG)
                else:
                    s_n = jnp.where(slot == t, s_n, NEG)
                m = jnp.maximum(jnp.max(s_p, axis=0, keepdims=True), jnp.max(s_n, axis=0, keepdims=True))
                p_p = jnp.exp(s_p - m)
                p_n = jnp.exp(s_n - m)
                l = jnp.sum(p_p, axis=0, keepdims=True) + jnp.sum(p_n, axis=0, keepdims=True)
                acc = (jnp.sum(p_p * vp, axis=0, keepdims=True)
                       + jnp.sum(p_n * vnh, axis=0, keepdims=True))
                parts.append((m, l, acc))
            m_all = functools.reduce(jnp.maximum, [m for m, _, _ in parts])
            l_all = sum(l * jnp.exp(m - m_all) for m, l, _ in parts)
            acc_all = sum(acc * jnp.exp(m - m_all) for m, _, acc in parts)
            o_ref[h, t:t + 1, :] = acc_all / l_all


def _sample_attention(q, k, v, cache_k, cache_v, n_new):
    n_batch, buf, nh, hd = cache_k.shape
    res16 = BRANCHES[-1][1]
    tail = 4 * BAND
    assert buf == res16 * BAND and n_new * nh <= 32
    rows2d = lambda c: c.reshape(n_batch, buf * nh, hd)
    resid = lambda c: c.reshape(n_batch, BAND, res16 * nh, hd)
    new_spec = pl.BlockSpec((nh, SLOTS, hd), lambda b: (0, b, 0))
    tail_spec = pl.BlockSpec((None, tail * nh, hd), lambda b: (b, buf // tail - 1, 0))
    res_spec = pl.BlockSpec((None, BAND, 32, hd), lambda b: (b, 0, 0, 0))
    return pl.pallas_call(
        functools.partial(_att_sample_kernel, n_new=n_new),
        grid=(n_batch,),
        in_specs=[new_spec, new_spec, new_spec, tail_spec, tail_spec, res_spec, res_spec],
        out_specs=pl.BlockSpec((None, nh, SLOTS, hd), lambda b: (b, 0, 0, 0)),
        out_shape=jax.ShapeDtypeStruct((n_batch, nh, SLOTS, hd), F32),
        compiler_params=_params(("parallel",)),
        name="attention_sample",
    )(q, k, v, rows2d(cache_k), rows2d(cache_v), resid(cache_k), resid(cache_v))


def _mix_kernel(att_ref, u_ref, vn_ref, x_ref, gt1_ref, sh2_ref, sc2_ref, g2_ref, wmix_ref, bmix_ref,
                wout_ref, x1_ref, h2_ref, h2t_ref, mix_scr):
    tm = x_ref.shape[0]
    for h in range(N_HEADS):
        mix_scr[:, h * HEAD_DIM:(h + 1) * HEAD_DIM] = att_ref[h].astype(BF16)
    for c in range(tm // BAND):
        rows = slice(c * BAND, (c + 1) * BAND)
        for g in range(N_HEADS):
            cs = slice(g * HEAD_DIM, (g + 1) * HEAD_DIM)
            m = jnp.dot(wmix_ref[g], vn_ref[rows, cs].astype(BF16),
                        preferred_element_type=F32) + bmix_ref[g]
            mix_scr[rows, ATT_WIDTH + g * HEAD_DIM:ATT_WIDTH + (g + 1) * HEAD_DIM] = (
                u_ref[rows, cs].astype(F32) * m).astype(BF16)
    y = jnp.dot(mix_scr[...], wout_ref[...], preferred_element_type=F32)
    x1 = x_ref[...] + gt1_ref[...] * y
    x1_ref[...] = x1
    h2 = _rms(x1, g2_ref[...]) * (1.0 + sc2_ref[...]) + sh2_ref[...]
    h2_ref[...] = h2.astype(BF16)
    h2t_ref[...] = h2.T.astype(BF16)


def _mix(att, u, vn, x, gt1, sh2, sc2, g2, wmix, bmix, w_out, tm):
    t, d = x.shape
    tpg = att.shape[2] // tm
    tok = lambda i: (i, 0)
    c2 = lambda i: (0, 0)
    c3 = lambda i: (0, 0, 0)
    return pl.pallas_call(
        _mix_kernel,
        grid=(t // tm,),
        in_specs=[pl.BlockSpec((None, N_HEADS, tm, HEAD_DIM), lambda i: (i // tpg, 0, i % tpg, 0)),
                  pl.BlockSpec((tm, ATT_WIDTH), tok), pl.BlockSpec((tm, ATT_WIDTH), tok),
                  pl.BlockSpec((tm, d), tok),
                  _mod_spec(gt1, tpg), _mod_spec(sh2, tpg), _mod_spec(sc2, tpg),
                  pl.BlockSpec((1, d), c2),
                  pl.BlockSpec(wmix.shape, c3), pl.BlockSpec(bmix.shape, c3),
                  pl.BlockSpec(w_out.shape, c2)],
        out_specs=[pl.BlockSpec((tm, d), tok), pl.BlockSpec((tm, d), tok),
                   pl.BlockSpec((d, tm), lambda i: (0, i))],
        out_shape=[jax.ShapeDtypeStruct((t, d), F32),
                   jax.ShapeDtypeStruct((t, d), BF16),
                   jax.ShapeDtypeStruct((d, t), BF16)],
        scratch_shapes=[pltpu.VMEM((tm, d), BF16)],
        compiler_params=_params(("parallel",)),
        name="mix_outproj",
    )(att, u, vn, x, gt1, sh2, sc2, g2, wmix, bmix, w_out)


def _cand_layout():
    rows = []
    rows += [(0, j) for j in range(16)]
    rows += [(1, j) for j in range(8)]
    rows += [(8 + r, 0) for r in range(8)]
    rows += [(2, j) if j < 5 else None for j in range(8)]
    rows += [(3, j) if j < 4 else None for j in range(8)]
    rows += [(4, j) if j < 3 else None for j in range(8)]
    rows += [(r, 0) if 5 <= r < 8 else None for r in range(8)]
    rows += [(r, 1) if 5 <= r < 8 else None for r in range(8)]
    return rows


N_CAND_ROWS = 72
BIG_FLAT = 1e9
NO_EXPERT = 1e30


def _cand_flat():
    flat = np.full((N_CAND_ROWS, LANES), BIG_FLAT, np.float32)
    for r, cell in enumerate(_cand_layout()):
        if cell is not None:
            flat[r, :] = 16 * cell[0] + cell[1]
    return jnp.asarray(flat)


def _top16_ranked(s, row_id, top_ref):
    rank = jnp.full(s.shape, float(TOPK), F32)
    for it in range(TOPK):
        m = jnp.max(s, axis=0, keepdims=True)
        idx = jnp.min(jnp.where(s == m, row_id, float(N_KEYS)), axis=0, keepdims=True)
        sel = row_id == idx
        rank = jnp.where(sel, float(it), rank)
        s = jnp.where(sel, -jnp.inf, s)
        top_ref[it:it + 1, :] = m
    return rank


def _sort16_network():
    def merge(lo, hi, r):
        step = r * 2
        if step < hi - lo:
            yield from merge(lo, hi, step)
            yield from merge(lo + r, hi, step)
            yield from [(i, i + r) for i in range(lo + r, hi - r, step)]
        else:
            yield (lo, lo + r)

    def sort(lo, hi):
        if hi > lo:
            mid = lo + (hi - lo) // 2
            yield from sort(lo, mid)
            yield from sort(mid + 1, hi)
            yield from merge(lo, hi, 1)

    return list(sort(0, TOPK - 1))


def _top16_sorted(s, top_ref):
    n = N_KEYS // SUBLANES
    cols = [s[SUBLANES * v:SUBLANES * (v + 1), :] for v in range(n)]
    for i, j in _sort16_network():
        cols[i], cols[j] = jnp.maximum(cols[i], cols[j]), jnp.minimum(cols[i], cols[j])
    ties = jnp.zeros((1, s.shape[1]), F32)
    prev = None
    for it in range(TOPK + 1):
        m = jnp.max(cols[0], axis=0, keepdims=True)
        if prev is not None:
            ties = ties + jnp.where(m == prev, 1.0, 0.0)
        prev = m
        if it == TOPK:
            break
        sel = cols[0] == m
        ties = ties + jnp.sum(jnp.where(sel, 1.0, 0.0), axis=0, keepdims=True) - 1.0
        top_ref[it:it + 1, :] = m
        for v in range(TOPK - it):
            nxt = cols[v + 1] if v + 1 < n else jnp.full_like(cols[v], -jnp.inf)
            cols[v] = jnp.where(sel, nxt, cols[v])
    return ties


def _select_block(h, lc, toks, exact, keys_ref, qp_scr, top1_scr, top2_scr, flat, row_id, row8,
                  thr_ref, g1_ref, s2_ref, e2_ref):
    valid = flat < BIG_FLAT
    sides = []
    for p, top_scr in ((0, top1_scr), (1, top2_scr)):
        col = pl.multiple_of((2 * h + p) * N_KEYS, N_KEYS)
        s = lax.dot_general(keys_ref[p], qp_scr[toks, pl.ds(col, N_KEYS)],
                            (((1,), (1,)), ((), ())), preferred_element_type=F32)
        sides.append((s, _top16_ranked(s, row_id, top_scr) if exact else _top16_sorted(s, top_scr)))
    (s1, info1), (s2, info2) = sides
    a1 = top1_scr[...]
    a2 = top2_scr[...]
    v = jnp.concatenate([
        a1[0:1] + a2[0:16],
        a1[1:2] + a2[0:8],
        a1[8:16] + a2[0:1],
        a1[2:3] + a2[0:8],
        a1[3:4] + a2[0:8],
        a1[4:5] + a2[0:8],
        a1[0:8] + a2[0:1],
        a1[0:8] + a2[1:2]], axis=0)
    vw = jnp.where(valid, v, -jnp.inf)
    selm = jnp.zeros(v.shape, F32)
    for _ in range(TOPK):
        m = jnp.max(vw, axis=0, keepdims=True)
        sel = vw == m
        if exact:
            f = jnp.min(jnp.where(sel, flat, BIG_FLAT), axis=0, keepdims=True)
            sel = flat == f
        selm = jnp.where(sel, 1.0, selm)
        vw = jnp.where(sel, -jnp.inf, vw)
    vmax = a1[0:1] + a2[0:1]
    z = jnp.sum(jnp.where(selm > 0.0, jnp.exp(v - vmax), 0.0), axis=0, keepdims=True)
    cnt_lo = selm[56:64] + selm[64:72]
    for i, (lo, hi) in enumerate(((0, 16), (16, 24), (32, 40), (40, 48), (48, 56))):
        ci = jnp.sum(selm[lo:hi], axis=0, keepdims=True)
        cnt_lo = jnp.where(row8 == i, ci, cnt_lo)
    cnt = jnp.concatenate([cnt_lo, selm[24:32]], axis=0)
    if exact:
        thr = jnp.where(cnt > 0.0, 0.5 - cnt, NO_EXPERT)
        key1, key2 = info1, -info2
        match = [float(i) for i in range(TOPK)]
    else:
        thr = jnp.full(cnt.shape, NO_EXPERT, F32)
        for j in range(TOPK):
            thr = jnp.where(cnt == float(j + 1), a2[j:j + 1], thr)
        key1, key2 = s1, s2
        match = [a1[i:i + 1] for i in range(TOPK)]
    thr_a = jnp.full((N_KEYS, LANES), NO_EXPERT, F32)
    for i in range(TOPK):
        thr_a = jnp.where(key1 == match[i], thr[i:i + 1], thr_a)
    thr_ref[h, lc] = thr_a
    g1_ref[h, lc] = jnp.exp(s1 - a1[0:1]) / z
    s2_ref[h, lc] = key2
    e2_ref[h, lc] = jnp.exp(s2 - a2[0:1])
    if exact:
        return None
    n3 = jnp.sum(selm, axis=0, keepdims=True) - float(TOPK)
    return jnp.max(info1 + info2 + n3)


def _select_kernel(h2_ref, wq_ref, keys_ref, flat_ref, thr_ref, g1_ref, s2_ref, e2_ref,
                   qp_scr, top1_scr, top2_scr):
    tm = h2_ref.shape[0]
    qp_scr[...] = jnp.dot(h2_ref[...], wq_ref[...], preferred_element_type=F32).astype(BF16)
    row_id = lax.broadcasted_iota(jnp.int32, (N_KEYS, LANES), 0).astype(F32)
    row8 = lax.broadcasted_iota(jnp.int32, (8, LANES), 0)
    flat = flat_ref[...]
    blocks = []
    for lc in range(tm // LANES):
        toks = slice(lc * LANES, (lc + 1) * LANES)
        blocks.append(functools.partial(
            _select_block, lc=lc, toks=toks, keys_ref=keys_ref, qp_scr=qp_scr, top1_scr=top1_scr.at[lc],
            top2_scr=top2_scr.at[lc], flat=flat, row_id=row_id, row8=row8, thr_ref=thr_ref,
            g1_ref=g1_ref, s2_ref=s2_ref, e2_ref=e2_ref))

    def head_pair_body(hh, carry):
        work = [(block, h) for h in (2 * hh, 2 * hh + 1) for block in blocks]
        ties = [block(h, exact=False) for block, h in work]
        for (block, h), tie in zip(work, ties):
            @pl.when(tie > 0.5)
            def _(block=block, h=h):
                block(h, exact=True)

        return carry

    lax.fori_loop(0, N_HEADS // 2, head_pair_body, 0)


def _select(h2, w_query, sub_keys, tm):
    t, d = h2.shape
    out_spec = pl.BlockSpec((N_HEADS, tm // LANES, N_KEYS, LANES), lambda i: (0, i, 0, 0))
    slab = lambda dt: jax.ShapeDtypeStruct((N_HEADS, t // LANES, N_KEYS, LANES), dt)
    return pl.pallas_call(
        _select_kernel,
        grid=(t // tm,),
        in_specs=[pl.BlockSpec((tm, d), lambda i: (i, 0)),
                  pl.BlockSpec(w_query.shape, lambda i: (0, 0)),
                  pl.BlockSpec(sub_keys.shape, lambda i: (0, 0, 0)),
                  pl.BlockSpec((N_CAND_ROWS, LANES), lambda i: (0, 0))],
        out_specs=[out_spec] * 4,
        out_shape=[slab(F32)] * 4,
        scratch_shapes=[pltpu.VMEM((tm, w_query.shape[1]), BF16),
                        pltpu.VMEM((tm // LANES, TOPK, LANES), F32),
                        pltpu.VMEM((tm // LANES, TOPK, LANES), F32)],
        compiler_params=_params(("parallel",)),
        name="peer_select",
    )(h2, w_query, sub_keys, _cand_flat())


def _coef_kernel(thr_ref, g1_ref, s2_ref, e2_ref, *rest, cast):
    if cast:
        u_ref, v_ref, c_ref, u16_ref, vt16_ref = rest
        u16_ref[...] = u_ref[...].astype(BF16)
        step = pl.program_id(0) * pl.num_programs(1) + pl.program_id(1)

        v_steps = vt16_ref.shape[1] // u_ref.shape[0]

        @pl.when(step % v_steps == 0)
        def _():
            vt16_ref[...] = v_ref[...].T.astype(BF16)
    else:
        (c_ref,) = rest
    n_l, n_a = thr_ref.shape[1], thr_ref.shape[2]
    for lc in range(n_l):
        for ai in range(n_a):
            rows = slice(ai * N_KEYS, (ai + 1) * N_KEYS)
            coef = jnp.zeros((N_KEYS, LANES), F32)
            for h in range(N_HEADS):
                thr = thr_ref[h, lc, ai:ai + 1, :]
                g1 = g1_ref[h, lc, ai:ai + 1, :]
                coef = coef + jnp.where(s2_ref[h, lc] >= thr, e2_ref[h, lc], 0.0) * g1
            c_ref[lc, rows, :] = coef.astype(BF16)


def _coef(sel, te, n_exp, tables=None):
    n_lc = sel[0].shape[1]
    n_l = min(n_lc, COEF_LANE_CHUNKS)
    grid = (n_lc // n_l, n_exp // te)
    a_spec = pl.BlockSpec((N_HEADS, n_l, te // N_KEYS, LANES), lambda i, j: (0, i, j, 0))
    b_spec = pl.BlockSpec((N_HEADS, n_l, N_KEYS, LANES), lambda i, j: (0, i, 0, 0))
    in_specs = [a_spec, a_spec, b_spec, b_spec]
    out_specs = [pl.BlockSpec((n_l, te, LANES), lambda i, j: (i, j, 0))]
    out_shape = [jax.ShapeDtypeStruct((n_lc, n_exp, LANES), BF16)]
    operands = list(sel)
    if tables is not None:
        d = tables[0].shape[1]
        steps = grid[0] * grid[1]
        u_rows = n_exp // steps
        assert u_rows * steps == n_exp and LANES % u_rows == 0
        v_steps = LANES // u_rows
        step = lambda i, j: i * grid[1] + j
        in_specs += [pl.BlockSpec((u_rows, d), lambda i, j: (step(i, j), 0)),
                     pl.BlockSpec((LANES, d), lambda i, j: (step(i, j) // v_steps, 0))]
        out_specs += [pl.BlockSpec((u_rows, d), lambda i, j: (step(i, j), 0)),
                      pl.BlockSpec((d, LANES), lambda i, j: (0, step(i, j) // v_steps))]
        out_shape += [jax.ShapeDtypeStruct((n_exp, d), BF16), jax.ShapeDtypeStruct((d, n_exp), BF16)]
        operands += list(tables)
    return pl.pallas_call(
        functools.partial(_coef_kernel, cast=tables is not None),
        grid=grid,
        in_specs=in_specs,
        out_specs=out_specs,
        out_shape=out_shape,
        compiler_params=_params(("arbitrary", "arbitrary")),
        name="peer_coef",
    )(*operands)


def _peer_kernel(h2t_ref, u_ref, vt_ref, c_ref, x1_ref, gt2_ref, gf_ref, y_ref, acc_scr):
    j = pl.program_id(1)
    tt = h2t_ref.shape[1]

    @pl.when(j == 0)
    def _():
        acc_scr[...] = jnp.zeros_like(acc_scr)

    act = jnp.dot(u_ref[...], h2t_ref[...], preferred_element_type=F32)
    p = jnp.concatenate(
        [_gelu(act[:, lc * LANES:(lc + 1) * LANES]).astype(BF16) * c_ref[lc]
         for lc in range(tt // LANES)], axis=1)
    acc_scr[...] += jnp.dot(vt_ref[...], p, preferred_element_type=F32)

    @pl.when(j == pl.num_programs(1) - 1)
    def _():
        x2 = x1_ref[...] + gt2_ref[...] * acc_scr[...].T
        y_ref[...] = _rms(x2, gf_ref[...])


def _peer(h2t, experts, sel, x1, gt2, g_final, rows_per_group, tt, te):
    kind, u, vt = experts
    n_exp = u.shape[0]
    if kind == "f32":
        coef, u, vt = _coef(sel, te, n_exp, tables=(u, vt))
    else:
        (coef,) = _coef(sel, te, n_exp)
    d, t = h2t.shape
    y = pl.pallas_call(
        _peer_kernel,
        grid=(t // tt, n_exp // te),
        in_specs=[pl.BlockSpec((d, tt), lambda i, j: (0, i)),
                  pl.BlockSpec((te, d), lambda i, j: (j, 0)),
                  pl.BlockSpec((d, te), lambda i, j: (0, j)),
                  pl.BlockSpec((tt // LANES, te, LANES), lambda i, j: (i, j, 0)),
                  pl.BlockSpec((tt, d), lambda i, j: (i, 0)),
                  _mod_spec(gt2, rows_per_group // tt),
                  pl.BlockSpec((1, d), lambda i, j: (0, 0))],
        out_specs=pl.BlockSpec((tt, d), lambda i, j: (i, 0)),
        out_shape=jax.ShapeDtypeStruct((t, d), F32),
        scratch_shapes=[pltpu.VMEM((d, tt), F32)],
        compiler_params=_params(("parallel", "arbitrary")),
        name="peer_dense",
    )(h2t, u, vt, coef, x1, gt2, g_final)
    return y, ("bf16", u, vt)


def _trunk(x, mods, n_groups, window, att_fn, w, experts, wmix, bmix, tm_in, tm_mix, tm_sel, tt, te):
    sh1, sc1, gt1, sh2, sc2, gt2 = mods
    rows = x.shape[0] // n_groups
    q, k, v, u, vn, *wins = _inproj(x, sh1, sc1, w["g1"], w["w_in"], w["lng"], w["lnb"], tm_in,
                                    n_groups, window)
    att, w_out, w_query = att_fn(q, k, v)
    x1, h2, h2t = _mix(att, u, vn, x, gt1, sh2, sc2, w["g2"], wmix, bmix, w_out, tm_mix)
    sel = _select(h2, w_query, w["sub_keys"], tm_sel)
    y, experts = _peer(h2t, experts, sel, x1, gt2, w["g_final"], rows, tt, te)
    return y, k, v, vn, wins, experts, (w_out, w_query)


def kernel(x_prompt, x_sample, cache_k, cache_v, c_prompt, c_sample, w_ada, b_ada, g_norm1, w_in,
           ln_v_g, ln_v_b, w_spatial, b_spatial, w_out, g_norm2, w_query, sub_keys, expert_u,
           expert_v, g_final):
    depth = w_ada.shape[0]
    assert depth == 1, "single-layer trunk"
    batch, seq, d = x_prompt.shape
    n_dec, n_new, _ = x_sample.shape
    buf = cache_k.shape[2]
    assert seq % SPAN == 0 and buf == SPAN and n_new <= SLOTS
    win = min(BRANCHES[-1][0], seq)
    pad_s = SAMPLE_ROWS
    assert n_dec * SLOTS <= pad_s

    n_c = batch + n_dec
    c_all = jnp.pad(jnp.concatenate([c_prompt, c_sample], axis=0), ((0, -n_c % SUBLANES), (0, 0)))
    mod = _adaln(c_all, w_ada[0], b_ada[0][None, :])
    mods = [mod[:, i * d:(i + 1) * d] for i in range(6)]
    mods_p = [m[:batch, None, :] for m in mods]
    mods_s = [jnp.pad(jnp.repeat(m[batch:n_c], SLOTS, axis=0), ((0, pad_s - n_dec * SLOTS), (0, 0)))[None]
              for m in mods]

    tri = jnp.tril(jnp.ones((BAND, BAND), dtype=bool))
    ws = jnp.where(tri, w_spatial[0], 0.0)
    bs = b_spatial[0]
    wmix_p = ws.astype(BF16)
    bmix_p = jnp.broadcast_to(bs[:, :, None], (N_HEADS, BAND, HEAD_DIM))
    eye = jnp.eye(pad_s // SLOTS, dtype=F32)
    slot_pad = ((0, 0), (0, SLOTS - n_new))
    ws_new = jnp.pad(ws[:, :n_new, :n_new], ((0, 0),) + slot_pad[1:] + slot_pad[1:])
    wmix_s = jax.vmap(lambda m: jnp.kron(eye, m))(ws_new).astype(BF16)
    bmix_s = jnp.broadcast_to(jnp.tile(jnp.pad(bs[:, :n_new], slot_pad), (1, pad_s // SLOTS))[:, :, None],
                              (N_HEADS, pad_s, HEAD_DIM))

    w = dict(g1=g_norm1[0][None, :], w_in=w_in[0].astype(BF16), lng=ln_v_g[0][None, :],
             lnb=ln_v_b[0][None, :], g2=g_norm2[0][None, :], sub_keys=sub_keys[0].astype(BF16),
             g_final=g_final[None, :])

    xp = x_prompt.reshape(batch * seq, d)
    prompt_att = functools.partial(_prompt_attention, weights=(w_out[0], w_query[0]))
    y_p, _, _, _, (k_win, v_win), experts, w_bf16 = _trunk(
        xp, mods_p, batch, win, prompt_att, w, ("f32", expert_u[0], expert_v[0]), wmix_p, bmix_p,
        te=EXPERT_TILE, **PROMPT_TILES)

    xs = jnp.pad(x_sample, ((0, 0), (0, SLOTS - n_new), (0, 0))).reshape(n_dec * SLOTS, d)
    xs = jnp.pad(xs, ((0, pad_s - n_dec * SLOTS), (0, 0)))

    def sample_att(q, k, v):
        o = _sample_attention(q[0], k[0], v[0], cache_k[0], cache_v[0], n_new)
        o = o.transpose(1, 0, 2, 3).reshape(N_HEADS, n_dec * SLOTS, HEAD_DIM)
        return (jnp.pad(o, ((0, 0), (0, pad_s - n_dec * SLOTS), (0, 0)))[None],) + w_bf16

    y_s, k_s, v_s, vn_s, _, _, _ = _trunk(
        xs, mods_s, 1, 0, sample_att, w, experts, wmix_s, bmix_s,
        tm_in=pad_s, tm_mix=pad_s, tm_sel=pad_s, tt=pad_s, te=EXPERT_TILE)

    new_rows = lambda t: t[:n_dec * SLOTS].reshape(n_dec, SLOTS, -1)[:, :n_new]
    new_heads = lambda t: t[0, :, :n_dec * SLOTS].reshape(N_HEADS, n_dec, SLOTS, HEAD_DIM)[:, :, :n_new
                                                                                         ].transpose(1, 2, 0, 3)
    return (y_p.reshape(batch, seq, d),
            new_rows(y_s),
            k_win.reshape(1, batch, win, N_HEADS, HEAD_DIM),
            v_win.reshape(1, batch, win, N_HEADS, HEAD_DIM),
            new_heads(k_s)[None],
            new_heads(v_s)[None],
            new_rows(vn_s)[None])
```

```python
import functools

import numpy as np
import jax
import jax.numpy as jnp
from jax import lax
from jax.experimental import pallas as pl
from jax.experimental.pallas import tpu as pltpu

F32 = jnp.float32
BF16 = jnp.bfloat16

EPS = 1e-6
LANES = 128
SUBLANES = 8
HEAD_DIM = 128
N_HEADS = 8
ATT_WIDTH = N_HEADS * HEAD_DIM
BRANCHES = ((128, 1), (512, 4), (2048, 16))
BAND = 128
SPAN = BAND * BRANCHES[-1][1]
N_KEYS = 128
TOPK = 16
NEG = -1e30
VMEM_LIMIT = 56 * 1024 * 1024
COEF_LANE_CHUNKS = 16
SLOTS = SUBLANES
SAMPLE_ROWS = LANES
EXPERT_TILE = SUBLANES * N_KEYS
PROMPT_TILES = dict(tm_in=512, tm_mix=512, tm_sel=512, tt=512)


def _params(sem):
    return pltpu.CompilerParams(dimension_semantics=sem, vmem_limit_bytes=VMEM_LIMIT)


def _gelu(x):
    return 0.5 * x * (1.0 + lax.erf(x * (2.0 ** -0.5)))


def _rms(x, g):
    return x * lax.rsqrt(jnp.mean(x * x, axis=-1, keepdims=True) + EPS) * g


def _slope(h):
    return 2.0 ** (-8.0 * (h + 1) / N_HEADS)


def _adaln_kernel(c_ref, w_ref, b_ref, o_ref):
    c = c_ref[...]
    a = c / (1.0 + jnp.exp(-c))
    o_ref[...] = jnp.dot(a.astype(BF16), w_ref[...].astype(BF16),
                         preferred_element_type=F32) + b_ref[...]


def _adaln(c_all, w_ada, b_ada):
    rows, d = c_all.shape
    n = w_ada.shape[1]
    tn = 1024
    return pl.pallas_call(
        _adaln_kernel,
        grid=(n // tn,),
        in_specs=[pl.BlockSpec((rows, d), lambda j: (0, 0)),
                  pl.BlockSpec((d, tn), lambda j: (0, j)),
                  pl.BlockSpec((1, tn), lambda j: (0, j))],
        out_specs=pl.BlockSpec((rows, tn), lambda j: (0, j)),
        out_shape=jax.ShapeDtypeStruct((rows, n), F32),
        compiler_params=_params(("arbitrary",)),
        name="adaln",
    )(c_all, w_ada, b_ada)


def _inproj_kernel(x_ref, sh_ref, sc_ref, g1_ref, w_ref, lng_ref, lnb_ref,
                   q_ref, k_ref, v_ref, u_ref, vn_ref, *rest, win_tiles):
    if win_tiles:
        kwin_ref, vwin_ref, h_scr = rest
        tpg, wt = win_tiles
        in_window = pl.program_id(0) % tpg >= tpg - wt
    else:
        (h_scr,) = rest
    j = pl.program_id(1)

    @pl.when(j == 0)
    def _():
        y = _rms(x_ref[...], g1_ref[...])
        h_scr[...] = (y * (1.0 + sc_ref[...]) + sh_ref[...]).astype(BF16)

    p = jnp.dot(h_scr[...], w_ref[...], preferred_element_type=F32)

    def heads_out(ref, val, win_ref=None):
        for h in range(N_HEADS):
            ref[h] = val[:, h * HEAD_DIM:(h + 1) * HEAD_DIM]
        if win_ref is not None:
            @pl.when(in_window)
            def _():
                for h in range(N_HEADS):
                    win_ref[:, h, :] = val[:, h * HEAD_DIM:(h + 1) * HEAD_DIM]

    @pl.when(j == 0)
    def _():
        heads_out(q_ref, p * (HEAD_DIM ** -0.5))

    @pl.when(j == 1)
    def _():
        heads_out(k_ref, p, kwin_ref if win_tiles else None)

    @pl.when(j == 2)
    def _():
        heads_out(v_ref, p, vwin_ref if win_tiles else None)

    @pl.when(j == 3)
    def _():
        u_ref[...] = _gelu(p).astype(BF16)

    @pl.when(j == 4)
    def _():
        g = _gelu(p)
        mu = jnp.mean(g, axis=-1, keepdims=True)
        gc = g - mu
        var = jnp.mean(gc * gc, axis=-1, keepdims=True)
        vn_ref[...] = gc * lax.rsqrt(var + EPS) * lng_ref[...] + lnb_ref[...]


def _mod_spec(mod, tiles_per_group):
    _, r, d = mod.shape
    return pl.BlockSpec((None, r, d), lambda i, *_: (i // tiles_per_group, 0, 0))


def _inproj(x, sh1, sc1, g1, w_in, lng, lnb, tm, n_groups, window):
    t, d = x.shape
    sec = ATT_WIDTH
    n_sec = w_in.shape[1] // sec
    rows = t // n_groups
    tpg = rows // tm
    tok = lambda i, j: (i, 0)
    const = lambda i, j: (0, 0)
    out_spec = pl.BlockSpec((tm, sec), tok)
    head_spec = pl.BlockSpec((None, N_HEADS, tm, HEAD_DIM), lambda i, j: (i // tpg, 0, i % tpg, 0))
    head_shape = jax.ShapeDtypeStruct((n_groups, N_HEADS, rows, HEAD_DIM), F32)
    out_specs = [head_spec] * 3 + [out_spec] * 2
    out_shape = [head_shape] * 3 + [jax.ShapeDtypeStruct((t, sec), BF16),
                                    jax.ShapeDtypeStruct((t, sec), F32)]
    if window:
        wt = window // tm
        win_spec = pl.BlockSpec((tm, N_HEADS, HEAD_DIM), lambda i, j: (
            (i // tpg) * wt + jnp.maximum(i % tpg - (tpg - wt), 0), 0, 0))
        win_shape = jax.ShapeDtypeStruct((n_groups * window, N_HEADS, HEAD_DIM), F32)
        out_specs += [win_spec] * 2
        out_shape += [win_shape] * 2
    return pl.pallas_call(
        functools.partial(_inproj_kernel, win_tiles=(tpg, window // tm) if window else None),
        grid=(t // tm, n_sec),
        in_specs=[pl.BlockSpec((tm, d), tok),
                  _mod_spec(sh1, tpg), _mod_spec(sc1, tpg),
                  pl.BlockSpec((1, d), const),
                  pl.BlockSpec((d, sec), lambda i, j: (0, j)),
                  pl.BlockSpec((1, sec), const), pl.BlockSpec((1, sec), const)],
        out_specs=out_specs,
        out_shape=out_shape,
        scratch_shapes=[pltpu.VMEM((tm, d), BF16)],
        compiler_params=_params(("arbitrary", "arbitrary")),
        name="inproj",
    )(x, sh1, sc1, g1, w_in, lng, lnb)


def _att_bias():
    a = np.arange(BAND)[:, None]
    kk = np.arange(2 * BAND)[None, :]
    steps = BAND + a - kk
    valid = (steps >= 0) & (steps <= BAND)
    slopes = np.array([_slope(h) for h in range(N_HEADS)])
    out = []
    for _, dil in BRANCHES:
        bias = -slopes[:, None, None] * (dil * steps).astype(np.float64)[None]
        out.append(np.where(valid[None], bias, NEG))
    return jnp.asarray(np.stack(out), dtype=F32)


def _att_prompt_kernel(q_ref, kc_ref, kp_ref, vc_ref, vp_ref, bias_ref, *rest):
    n_w = (len(rest) - 4) // 2
    w_refs, o_ref, w16_refs = rest[:n_w], rest[n_w], rest[n_w + 1:2 * n_w + 1]
    m_scr, l_scr, acc_scr = rest[2 * n_w + 1:]
    for w_ref, w16_ref in zip(w_refs, w16_refs):
        w16_ref[...] = w_ref[...].astype(BF16)
    first_span = pl.program_id(2) == 0
    span = q_ref.shape[0]
    is_prev = lax.broadcasted_iota(jnp.int32, (BAND, 2 * BAND), 1) < BAND

    def rows_at(start, dil):
        return pl.ds(start, BAND) if dil == 1 else pl.ds(start, BAND, stride=dil)

    for bi, (_, dil) in enumerate(BRANCHES):
        for r in range(dil):
            for n in range(span // (BAND * dil)):
                start = r + dil * BAND * n
                rows = rows_at(start, dil)
                if n > 0:
                    prev = rows_at(start - dil * BAND, dil)
                    kp, vp = kc_ref[prev, :], vc_ref[prev, :]
                else:
                    prev = rows_at(span - dil * BAND + r, dil)
                    kp, vp = kp_ref[prev, :], vp_ref[prev, :]
                kw = jnp.concatenate([kp, kc_ref[rows, :]], axis=0).astype(BF16)
                vw = jnp.concatenate([vp, vc_ref[rows, :]], axis=0).astype(BF16)
                s = lax.dot_general(q_ref[rows, :].astype(BF16), kw, (((1,), (1,)), ((), ())),
                                    preferred_element_type=F32) + bias_ref[bi]
                if n == 0:
                    s = jnp.where(jnp.logical_and(first_span, is_prev), NEG, s)
                m_b = jnp.max(s, axis=-1, keepdims=True)
                p = jnp.exp(s - m_b)
                l_b = jnp.sum(p, axis=-1, keepdims=True)
                pv = jnp.dot(p.astype(BF16), vw, preferred_element_type=F32)
                m_scr[bi, rows, :] = jnp.broadcast_to(m_b, (BAND, LANES))
                l_scr[bi, rows, :] = jnp.broadcast_to(l_b, (BAND, LANES))
                acc_scr[bi, rows, :] = pv
    ms = [m_scr[bi] for bi in range(len(BRANCHES))]
    m_all = functools.reduce(jnp.maximum, ms)
    es = [jnp.exp(m - m_all) for m in ms]
    l_all = sum(e * l_scr[bi] for bi, e in enumerate(es))
    acc_all = sum(e * acc_scr[bi] for bi, e in enumerate(es))
    o_ref[...] = (acc_all / l_all).astype(o_ref.dtype)


def _prompt_attention(q, k, v, weights=()):
    b, nh, seq, hd = q.shape
    n_span = seq // SPAN
    steps = b * nh * n_span
    cur = lambda bb, h, s: (bb, h, s, 0)
    prev = lambda bb, h, s: (bb, h, jnp.maximum(s - 1, 0), 0)
    blk = lambda m: pl.BlockSpec((None, None, SPAN, hd), m)
    step = lambda bb, h, s: ((bb * nh + h) * n_span + s, 0)
    w_specs = []
    for wm in weights:
        assert wm.shape[0] % (16 * steps) == 0
        w_specs.append(pl.BlockSpec((wm.shape[0] // steps, wm.shape[1]), step))
    return pl.pallas_call(
        _att_prompt_kernel,
        grid=(b, nh, n_span),
        in_specs=[blk(cur), blk(cur), blk(prev), blk(cur), blk(prev),
                  pl.BlockSpec((len(BRANCHES), None, BAND, 2 * BAND), lambda bb, h, s: (0, h, 0, 0))
                  ] + w_specs,
        out_specs=[blk(cur)] + w_specs,
        out_shape=[jax.ShapeDtypeStruct(q.shape, BF16)]
        + [jax.ShapeDtypeStruct(wm.shape, BF16) for wm in weights],
        scratch_shapes=[pltpu.VMEM((len(BRANCHES), SPAN, LANES), F32)] * 3,
        compiler_params=_params(("parallel", "parallel", "arbitrary")),
        name="attention_prompt",
    )(q, k, k, v, v, _att_bias(), *weights)


def _att_sample_kernel(q_ref, kn_ref, vn_ref, kt_ref, vt_ref, kr_ref, vr_ref, o_ref, *, n_new):
    o_ref[...] = jnp.zeros_like(o_ref)
    col = lax.broadcasted_iota(jnp.int32, (BAND, 1), 0).astype(F32)
    slot = lax.broadcasted_iota(jnp.int32, (SLOTS, 1), 0)
    tail = kt_ref.shape[0] // N_HEADS
    for h in range(N_HEADS):
        slope = _slope(h)
        knh, vnh = kn_ref[h], vn_ref[h]
        for t in range(n_new):
            q_row = q_ref[h, t:t + 1, :]
            parts = []
            for _, dil in BRANCHES:
                a = t if dil == 1 else 0
                if dil == 1:
                    rows = pl.ds((tail - BAND) * N_HEADS + h, BAND, stride=N_HEADS)
                    kp, vp = kt_ref[rows, :], vt_ref[rows, :]
                elif dil * BAND == tail:
                    rows = pl.ds(t * N_HEADS + h, BAND, stride=dil * N_HEADS)
                    kp, vp = kt_ref[rows, :], vt_ref[rows, :]
                else:
                    kp, vp = kr_ref[:, t * N_HEADS + h, :], vr_ref[:, t * N_HEADS + h, :]
                steps = float(BAND + a) - col
                s_p = jnp.sum(kp * q_row, axis=1, keepdims=True) - (slope * dil) * steps
                if a > 0:
                    s_p = jnp.where(steps <= float(BAND), s_p, NEG)
                s_n = jnp.sum(knh * q_row, axis=1, keepdims=True)
                if dil == 1:
                    s_n = jnp.where(slot <= t, s_n - slope * (t - slot).astype(F32), NEG)
                else:
                    s_n = jnp.where(slot == t, s_n, NEG)
                m = jnp.maximum(jnp.max(s_p, axis=0, keepdims=True), jnp.max(s_n, axis=0, keepdims=True))
                p_p = jnp.exp(s_p - m)
                p_n = jnp.exp(s_n - m)
                l = jnp.sum(p_p, axis=0, keepdims=True) + jnp.sum(p_n, axis=0, keepdims=True)
                acc = (jnp.sum(p_p * vp, axis=0, keepdims=True)
                       + jnp.sum(p_n * vnh, axis=0, keepdims=True))
                parts.append((m, l, acc))
            m_all = functools.reduce(jnp.maximum, [m for m, _, _ in parts])
            l_all = sum(l * jnp.exp(m - m_all) for m, l, _ in parts)
            acc_all = sum(acc * jnp.exp(m - m_all) for m, _, acc in parts)
            o_ref[h, t:t + 1, :] = acc_all / l_all


def _sample_attention(q, k, v, cache_k, cache_v, n_new):
    n_batch, buf, nh, hd = cache_k.shape
    res16 = BRANCHES[-1][1]
    tail = 4 * BAND
    assert buf == res16 * BAND and n_new * nh <= 32
    rows2d = lambda c: c.reshape(n_batch, buf * nh, hd)
    resid = lambda c: c.reshape(n_batch, BAND, res16 * nh, hd)
    new_spec = pl.BlockSpec((nh, SLOTS, hd), lambda b: (0, b, 0))
    tail_spec = pl.BlockSpec((None, tail * nh, hd), lambda b: (b, buf // tail - 1, 0))
    res_spec = pl.BlockSpec((None, BAND, 32, hd), lambda b: (b, 0, 0, 0))
    return pl.pallas_call(
        functools.partial(_att_sample_kernel, n_new=n_new),
        grid=(n_batch,),
        in_specs=[new_spec, new_spec, new_spec, tail_spec, tail_spec, res_spec, res_spec],
        out_specs=pl.BlockSpec((None, nh, SLOTS, hd), lambda b: (b, 0, 0, 0)),
        out_shape=jax.ShapeDtypeStruct((n_batch, nh, SLOTS, hd), F32),
        compiler_params=_params(("parallel",)),
        name="attention_sample",
    )(q, k, v, rows2d(cache_k), rows2d(cache_v), resid(cache_k), resid(cache_v))


def _mix_kernel(att_ref, u_ref, vn_ref, x_ref, gt1_ref, sh2_ref, sc2_ref, g2_ref, wmix_ref, bmix_ref,
                wout_ref, x1_ref, h2_ref, h2t_ref, mix_scr):
    tm = x_ref.shape[0]
    for h in range(N_HEADS):
        mix_scr[:, h * HEAD_DIM:(h + 1) * HEAD_DIM] = att_ref[h].astype(BF16)
    for c in range(tm // BAND):
        rows = slice(c * BAND, (c + 1) * BAND)
        for g in range(N_HEADS):
            cs = slice(g * HEAD_DIM, (g + 1) * HEAD_DIM)
            m = jnp.dot(wmix_ref[g], vn_ref[rows, cs].astype(BF16),
                        preferred_element_type=F32) + bmix_ref[g]
            mix_scr[rows, ATT_WIDTH + g * HEAD_DIM:ATT_WIDTH + (g + 1) * HEAD_DIM] = (
                u_ref[rows, cs].astype(F32) * m).astype(BF16)
    y = jnp.dot(mix_scr[...], wout_ref[...], preferred_element_type=F32)
    x1 = x_ref[...] + gt1_ref[...] * y
    x1_ref[...] = x1
    h2 = _rms(x1, g2_ref[...]) * (1.0 + sc2_ref[...]) + sh2_ref[...]
    h2_ref[...] = h2.astype(BF16)
    h2t_ref[...] = h2.T.astype(BF16)


def _mix(att, u, vn, x, gt1, sh2, sc2, g2, wmix, bmix, w_out, tm):
    t, d = x.shape
    tpg = att.shape[2] // tm
    tok = lambda i: (i, 0)
    c2 = lambda i: (0, 0)
    c3 = lambda i: (0, 0, 0)
    return pl.pallas_call(
        _mix_kernel,
        grid=(t // tm,),
        in_specs=[pl.BlockSpec((None, N_HEADS, tm, HEAD_DIM), lambda i: (i // tpg, 0, i % tpg, 0)),
                  pl.BlockSpec((tm, ATT_WIDTH), tok), pl.BlockSpec((tm, ATT_WIDTH), tok),
                  pl.BlockSpec((tm, d), tok),
                  _mod_spec(gt1, tpg), _mod_spec(sh2, tpg), _mod_spec(sc2, tpg),
                  pl.BlockSpec((1, d), c2),
                  pl.BlockSpec(wmix.shape, c3), pl.BlockSpec(bmix.shape, c3),
                  pl.BlockSpec(w_out.shape, c2)],
        out_specs=[pl.BlockSpec((tm, d), tok), pl.BlockSpec((tm, d), tok),
                   pl.BlockSpec((d, tm), lambda i: (0, i))],
        out_shape=[jax.ShapeDtypeStruct((t, d), F32),
                   jax.ShapeDtypeStruct((t, d), BF16),
                   jax.ShapeDtypeStruct((d, t), BF16)],
        scratch_shapes=[pltpu.VMEM((tm, d), BF16)],
        compiler_params=_params(("parallel",)),
        name="mix_outproj",
    )(att, u, vn, x, gt1, sh2, sc2, g2, wmix, bmix, w_out)


def _cand_layout():
    rows = []
    rows += [(0, j) for j in range(16)]
    rows += [(1, j) for j in range(8)]
    rows += [(8 + r, 0) for r in range(8)]
    rows += [(2, j) if j < 5 else None for j in range(8)]
    rows += [(3, j) if j < 4 else None for j in range(8)]
    rows += [(4, j) if j < 3 else None for j in range(8)]
    rows += [(r, 0) if 5 <= r < 8 else None for r in range(8)]
    rows += [(r, 1) if 5 <= r < 8 else None for r in range(8)]
    return rows


N_CAND_ROWS = 72
BIG_FLAT = 1e9
NO_EXPERT = 1e30


def _cand_flat():
    flat = np.full((N_CAND_ROWS, LANES), BIG_FLAT, np.float32)
    for r, cell in enumerate(_cand_layout()):
        if cell is not None:
            flat[r, :] = 16 * cell[0] + cell[1]
    return jnp.asarray(flat)


def _top16_ranked(s, row_id, top_ref):
    rank = jnp.full(s.shape, float(TOPK), F32)
    for it in range(TOPK):
        m = jnp.max(s, axis=0, keepdims=True)
        idx = jnp.min(jnp.where(s == m, row_id, float(N_KEYS)), axis=0, keepdims=True)
        sel = row_id == idx
        rank = jnp.where(sel, float(it), rank)
        s = jnp.where(sel, -jnp.inf, s)
        top_ref[it:it + 1, :] = m
    return rank


def _sort16_network():
    def merge(lo, hi, r):
        step = r * 2
        if step < hi - lo:
            yield from merge(lo, hi, step)
            yield from merge(lo + r, hi, step)
            yield from [(i, i + r) for i in range(lo + r, hi - r, step)]
        else:
            yield (lo, lo + r)

    def sort(lo, hi):
        if hi > lo:
            mid = lo + (hi - lo) // 2
            yield from sort(lo, mid)
            yield from sort(mid + 1, hi)
            yield from merge(lo, hi, 1)

    return list(sort(0, TOPK - 1))


def _top16_sorted(s, top_ref):
    n = N_KEYS // SUBLANES
    cols = [s[SUBLANES * v:SUBLANES * (v + 1), :] for v in range(n)]
    for i, j in _sort16_network():
        cols[i], cols[j] = jnp.maximum(cols[i], cols[j]), jnp.minimum(cols[i], cols[j])
    ties = jnp.zeros((1, s.shape[1]), F32)
    prev = None
    for it in range(TOPK + 1):
        m = jnp.max(cols[0], axis=0, keepdims=True)
        if prev is not None:
            ties = ties + jnp.where(m == prev, 1.0, 0.0)
        prev = m
        if it == TOPK:
            break
        sel = cols[0] == m
        ties = ties + jnp.sum(jnp.where(sel, 1.0, 0.0), axis=0, keepdims=True) - 1.0
        top_ref[it:it + 1, :] = m
        for v in range(TOPK - it):
            nxt = cols[v + 1] if v + 1 < n else jnp.full_like(cols[v], -jnp.inf)
            cols[v] = jnp.where(sel, nxt, cols[v])
    return ties


def _select_block(h, lc, toks, exact, keys_ref, qp_scr, top1_scr, top2_scr, flat, row_id, row8,
                  thr_ref, g1_ref, s2_ref, e2_ref):
    valid = flat < BIG_FLAT
    sides = []
    for p, top_scr in ((0, top1_scr), (1, top2_scr)):
        col = pl.multiple_of((2 * h + p) * N_KEYS, N_KEYS)
        s = lax.dot_general(keys_ref[p], qp_scr[toks, pl.ds(col, N_KEYS)],
                            (((1,), (1,)), ((), ())), preferred_element_type=F32)
        sides.append((s, _top16_ranked(s, row_id, top_scr) if exact else _top16_sorted(s, top_scr)))
    (s1, info1), (s2, info2) = sides
    a1 = top1_scr[...]
    a2 = top2_scr[...]
    v = jnp.concatenate([
        a1[0:1] + a2[0:16],
        a1[1:2] + a2[0:8],
        a1[8:16] + a2[0:1],
        a1[2:3] + a2[0:8],
        a1[3:4] + a2[0:8],
        a1[4:5] + a2[0:8],
        a1[0:8] + a2[0:1],
        a1[0:8] + a2[1:2]], axis=0)
    vw = jnp.where(valid, v, -jnp.inf)
    selm = jnp.zeros(v.shape, F32)
    for _ in range(TOPK):
        m = jnp.max(vw, axis=0, keepdims=True)
        sel = vw == m
        if exact:
            f = jnp.min(jnp.where(sel, flat, BIG_FLAT), axis=0, keepdims=True)
            sel = flat == f
        selm = jnp.where(sel, 1.0, selm)
        vw = jnp.where(sel, -jnp.inf, vw)
    vmax = a1[0:1] + a2[0:1]
    z = jnp.sum(jnp.where(selm > 0.0, jnp.exp(v - vmax), 0.0), axis=0, keepdims=True)
    cnt_lo = selm[56:64] + selm[64:72]
    for i, (lo, hi) in enumerate(((0, 16), (16, 24), (32, 40), (40, 48), (48, 56))):
        ci = jnp.sum(selm[lo:hi], axis=0, keepdims=True)
        cnt_lo = jnp.where(row8 == i, ci, cnt_lo)
    cnt = jnp.concatenate([cnt_lo, selm[24:32]], axis=0)
    if exact:
        thr = jnp.where(cnt > 0.0, 0.5 - cnt, NO_EXPERT)
        key1, key2 = info1, -info2
        match = [float(i) for i in range(TOPK)]
    else:
        thr = jnp.full(cnt.shape, NO_EXPERT, F32)
        for j in range(TOPK):
            thr = jnp.where(cnt == float(j + 1), a2[j:j + 1], thr)
        key1, key2 = s1, s2
        match = [a1[i:i + 1] for i in range(TOPK)]
    thr_a = jnp.full((N_KEYS, LANES), NO_EXPERT, F32)
    for i in range(TOPK):
        thr_a = jnp.where(key1 == match[i], thr[i:i + 1], thr_a)
    thr_ref[h, lc] = thr_a
    g1_ref[h, lc] = jnp.exp(s1 - a1[0:1]) / z
    s2_ref[h, lc] = key2
    e2_ref[h, lc] = jnp.exp(s2 - a2[0:1])
    if exact:
        return None
    n3 = jnp.sum(selm, axis=0, keepdims=True) - float(TOPK)
    return jnp.max(info1 + info2 + n3)


def _select_kernel(h2_ref, wq_ref, keys_ref, flat_ref, thr_ref, g1_ref, s2_ref, e2_ref,
                   qp_scr, top1_scr, top2_scr):
    tm = h2_ref.shape[0]
    qp_scr[...] = jnp.dot(h2_ref[...], wq_ref[...], preferred_element_type=F32).astype(BF16)
    row_id = lax.broadcasted_iota(jnp.int32, (N_KEYS, LANES), 0).astype(F32)
    row8 = lax.broadcasted_iota(jnp.int32, (8, LANES), 0)
    flat = flat_ref[...]
    blocks = []
    for lc in range(tm // LANES):
        toks = slice(lc * LANES, (lc + 1) * LANES)
        blocks.append(functools.partial(
            _select_block, lc=lc, toks=toks, keys_ref=keys_ref, qp_scr=qp_scr, top1_scr=top1_scr.at[lc],
            top2_scr=top2_scr.at[lc], flat=flat, row_id=row_id, row8=row8, thr_ref=thr_ref,
            g1_ref=g1_ref, s2_ref=s2_ref, e2_ref=e2_ref))

    def head_pair_body(hh, carry):
        work = [(block, h) for h in (2 * hh, 2 * hh + 1) for block in blocks]
        ties = [block(h, exact=False) for block, h in work]
        for (block, h), tie in zip(work, ties):
            @pl.when(tie > 0.5)
            def _(block=block, h=h):
                block(h, exact=True)

        return carry

    lax.fori_loop(0, N_HEADS // 2, head_pair_body, 0)


def _select(h2, w_query, sub_keys, tm):
    t, d = h2.shape
    out_spec = pl.BlockSpec((N_HEADS, tm // LANES, N_KEYS, LANES), lambda i: (0, i, 0, 0))
    slab = lambda dt: jax.ShapeDtypeStruct((N_HEADS, t // LANES, N_KEYS, LANES), dt)
    return pl.pallas_call(
        _select_kernel,
        grid=(t // tm,),
        in_specs=[pl.BlockSpec((tm, d), lambda i: (i, 0)),
                  pl.BlockSpec(w_query.shape, lambda i: (0, 0)),
                  pl.BlockSpec(sub_keys.shape, lambda i: (0, 0, 0)),
                  pl.BlockSpec((N_CAND_ROWS, LANES), lambda i: (0, 0))],
        out_specs=[out_spec] * 4,
        out_shape=[slab(F32)] * 4,
        scratch_shapes=[pltpu.VMEM((tm, w_query.shape[1]), BF16),
                        pltpu.VMEM((tm // LANES, TOPK, LANES), F32),
                        pltpu.VMEM((tm // LANES, TOPK, LANES), F32)],
        compiler_params=_params(("parallel",)),
        name="peer_select",
    )(h2, w_query, sub_keys, _cand_flat())


def _coef_kernel(thr_ref, g1_ref, s2_ref, e2_ref, *rest, cast):
    if cast:
        u_ref, v_ref, c_ref, u16_ref, vt16_ref = rest
        u16_ref[...] = u_ref[...].astype(BF16)
        step = pl.program_id(0) * pl.num_programs(1) + pl.program_id(1)

        v_steps = vt16_ref.shape[1] // u_ref.shape[0]

        @pl.when(step % v_steps == 0)
        def _():
            vt16_ref[...] = v_ref[...].T.astype(BF16)
    else:
        (c_ref,) = rest
    n_l, n_a = thr_ref.shape[1], thr_ref.shape[2]
    for lc in range(n_l):
        for ai in range(n_a):
            rows = slice(ai * N_KEYS, (ai + 1) * N_KEYS)
            coef = jnp.zeros((N_KEYS, LANES), F32)
            for h in range(N_HEADS):
                thr = thr_ref[h, lc, ai:ai + 1, :]
                g1 = g1_ref[h, lc, ai:ai + 1, :]
                coef = coef + jnp.where(s2_ref[h, lc] >= thr, e2_ref[h, lc], 0.0) * g1
            c_ref[lc, rows, :] = coef.astype(BF16)


def _coef(sel, te, n_exp, tables=None):
    n_lc = sel[0].shape[1]
    n_l = min(n_lc, COEF_LANE_CHUNKS)
    grid = (n_lc // n_l, n_exp // te)
    a_spec = pl.BlockSpec((N_HEADS, n_l, te // N_KEYS, LANES), lambda i, j: (0, i, j, 0))
    b_spec = pl.BlockSpec((N_HEADS, n_l, N_KEYS, LANES), lambda i, j: (0, i, 0, 0))
    in_specs = [a_spec, a_spec, b_spec, b_spec]
    out_specs = [pl.BlockSpec((n_l, te, LANES), lambda i, j: (i, j, 0))]
    out_shape = [jax.ShapeDtypeStruct((n_lc, n_exp, LANES), BF16)]
    operands = list(sel)
    if tables is not None:
        d = tables[0].shape[1]
        steps = grid[0] * grid[1]
        u_rows = n_exp // steps
        v_rows = max(u_rows, LANES)
        assert u_rows * steps == n_exp and v_rows % u_rows == 0
        v_steps = v_rows // u_rows
        step = lambda i, j: i * grid[1] + j
        in_specs += [pl.BlockSpec((u_rows, d), lambda i, j: (step(i, j), 0)),
                     pl.BlockSpec((v_rows, d), lambda i, j: (step(i, j) // v_steps, 0))]
        out_specs += [pl.BlockSpec((u_rows, d), lambda i, j: (step(i, j), 0)),
                      pl.BlockSpec((d, v_rows), lambda i, j: (0, step(i, j) // v_steps))]
        out_shape += [jax.ShapeDtypeStruct((n_exp, d), BF16), jax.ShapeDtypeStruct((d, n_exp), BF16)]
        operands += list(tables)
    return pl.pallas_call(
        functools.partial(_coef_kernel, cast=tables is not None),
        grid=grid,
        in_specs=in_specs,
        out_specs=out_specs,
        out_shape=out_shape,
        compiler_params=_params(("arbitrary", "arbitrary")),
        name="peer_coef",
    )(*operands)


def _peer_kernel(h2t_ref, u_ref, vt_ref, c_ref, x1_ref, gt2_ref, gf_ref, y_ref, acc_scr):
    j = pl.program_id(1)
    tt = h2t_ref.shape[1]

    @pl.when(j == 0)
    def _():
        acc_scr[...] = jnp.zeros_like(acc_scr)

    act = jnp.dot(u_ref[...], h2t_ref[...], preferred_element_type=F32)
    p = jnp.concatenate(
        [_gelu(act[:, lc * LANES:(lc + 1) * LANES]).astype(BF16) * c_ref[lc]
         for lc in range(tt // LANES)], axis=1)
    acc_scr[...] += jnp.dot(vt_ref[...], p, preferred_element_type=F32)

    @pl.when(j == pl.num_programs(1) - 1)
    def _():
        x2 = x1_ref[...] + gt2_ref[...] * acc_scr[...].T
        y_ref[...] = _rms(x2, gf_ref[...])


def _peer(h2t, experts, sel, x1, gt2, g_final, rows_per_group, tt, te):
    kind, u, vt = experts
    n_exp = u.shape[0]
    if kind == "f32":
        coef, u, vt = _coef(sel, te, n_exp, tables=(u, vt))
    else:
        (coef,) = _coef(sel, te, n_exp)
    d, t = h2t.shape
    y = pl.pallas_call(
        _peer_kernel,
        grid=(t // tt, n_exp // te),
        in_specs=[pl.BlockSpec((d, tt), lambda i, j: (0, i)),
                  pl.BlockSpec((te, d), lambda i, j: (j, 0)),
                  pl.BlockSpec((d, te), lambda i, j: (0, j)),
                  pl.BlockSpec((tt // LANES, te, LANES), lambda i, j: (i, j, 0)),
                  pl.BlockSpec((tt, d), lambda i, j: (i, 0)),
                  _mod_spec(gt2, rows_per_group // tt),
                  pl.BlockSpec((1, d), lambda i, j: (0, 0))],
        out_specs=pl.BlockSpec((tt, d), lambda i, j: (i, 0)),
        out_shape=jax.ShapeDtypeStruct((t, d), F32),
        scratch_shapes=[pltpu.VMEM((d, tt), F32)],
        compiler_params=_params(("parallel", "arbitrary")),
        name="peer_dense",
    )(h2t, u, vt, coef, x1, gt2, g_final)
    return y, ("bf16", u, vt)


def _trunk(x, mods, n_groups, window, att_fn, w, experts, wmix, bmix, tm_in, tm_mix, tm_sel, tt, te):
    sh1, sc1, gt1, sh2, sc2, gt2 = mods
    rows = x.shape[0] // n_groups
    q, k, v, u, vn, *wins = _inproj(x, sh1, sc1, w["g1"], w["w_in"], w["lng"], w["lnb"], tm_in,
                                    n_groups, window)
    att, w_out, w_query = att_fn(q, k, v)
    x1, h2, h2t = _mix(att, u, vn, x, gt1, sh2, sc2, w["g2"], wmix, bmix, w_out, tm_mix)
    sel = _select(h2, w_query, w["sub_keys"], tm_sel)
    y, experts = _peer(h2t, experts, sel, x1, gt2, w["g_final"], rows, tt, te)
    return y, k, v, vn, wins, experts, (w_out, w_query)


def kernel(x_prompt, x_sample, cache_k, cache_v, c_prompt, c_sample, w_ada, b_ada, g_norm1, w_in,
           ln_v_g, ln_v_b, w_spatial, b_spatial, w_out, g_norm2, w_query, sub_keys, expert_u,
           expert_v, g_final):
    depth = w_ada.shape[0]
    assert depth == 1, "single-layer trunk"
    batch, seq, d = x_prompt.shape
    n_dec, n_new, _ = x_sample.shape
    buf = cache_k.shape[2]
    assert seq % SPAN == 0 and buf == SPAN and n_new <= SLOTS
    win = min(BRANCHES[-1][0], seq)
    pad_s = SAMPLE_ROWS
    assert n_dec * SLOTS <= pad_s

    n_c = batch + n_dec
    c_all = jnp.pad(jnp.concatenate([c_prompt, c_sample], axis=0), ((0, -n_c % SUBLANES), (0, 0)))
    mod = _adaln(c_all, w_ada[0], b_ada[0][None, :])
    mods = [mod[:, i * d:(i + 1) * d] for i in range(6)]
    mods_p = [m[:batch, None, :] for m in mods]
    mods_s = [jnp.pad(jnp.repeat(m[batch:n_c], SLOTS, axis=0), ((0, pad_s - n_dec * SLOTS), (0, 0)))[None]
              for m in mods]

    tri = jnp.tril(jnp.ones((BAND, BAND), dtype=bool))
    ws = jnp.where(tri, w_spatial[0], 0.0)
    bs = b_spatial[0]
    wmix_p = ws.astype(BF16)
    bmix_p = jnp.broadcast_to(bs[:, :, None], (N_HEADS, BAND, HEAD_DIM))
    eye = jnp.eye(pad_s // SLOTS, dtype=F32)
    slot_pad = ((0, 0), (0, SLOTS - n_new))
    ws_new = jnp.pad(ws[:, :n_new, :n_new], ((0, 0),) + slot_pad[1:] + slot_pad[1:])
    wmix_s = jax.vmap(lambda m: jnp.kron(eye, m))(ws_new).astype(BF16)
    bmix_s = jnp.broadcast_to(jnp.tile(jnp.pad(bs[:, :n_new], slot_pad), (1, pad_s // SLOTS))[:, :, None],
                              (N_HEADS, pad_s, HEAD_DIM))

    w = dict(g1=g_norm1[0][None, :], w_in=w_in[0].astype(BF16), lng=ln_v_g[0][None, :],
             lnb=ln_v_b[0][None, :], g2=g_norm2[0][None, :], sub_keys=sub_keys[0].astype(BF16),
             g_final=g_final[None, :])

    xp = x_prompt.reshape(batch * seq, d)
    prompt_att = functools.partial(_prompt_attention, weights=(w_out[0], w_query[0]))
    y_p, _, _, _, (k_win, v_win), experts, w_bf16 = _trunk(
        xp, mods_p, batch, win, prompt_att, w, ("f32", expert_u[0], expert_v[0]), wmix_p, bmix_p,
        te=EXPERT_TILE, **PROMPT_TILES)

    xs = jnp.pad(x_sample, ((0, 0), (0, SLOTS - n_new), (0, 0))).reshape(n_dec * SLOTS, d)
    xs = jnp.pad(xs, ((0, pad_s - n_dec * SLOTS), (0, 0)))

    def sample_att(q, k, v):
        o = _sample_attention(q[0], k[0], v[0], cache_k[0], cache_v[0], n_new)
        o = o.transpose(1, 0, 2, 3).reshape(N_HEADS, n_dec * SLOTS, HEAD_DIM)
        return (jnp.pad(o, ((0, 0), (0, pad_s - n_dec * SLOTS), (0, 0)))[None],) + w_bf16

    y_s, k_s, v_s, vn_s, _, _, _ = _trunk(
        xs, mods_s, 1, 0, sample_att, w, experts, wmix_s, bmix_s,
        tm_in=pad_s, tm_mix=pad_s, tm_sel=pad_s, tt=pad_s, te=EXPERT_TILE)

    new_rows = lambda t: t[:n_dec * SLOTS].reshape(n_dec, SLOTS, -1)[:, :n_new]
    new_heads = lambda t: t[0, :, :n_dec * SLOTS].reshape(N_HEADS, n_dec, SLOTS, HEAD_DIM)[:, :, :n_new
                                                                                         ].transpose(1, 2, 0, 3)
    return (y_p.reshape(batch, seq, d),
            new_rows(y_s),
            k_win.reshape(1, batch, win, N_HEADS, HEAD_DIM),
            v_win.reshape(1, batch, win, N_HEADS, HEAD_DIM),
            new_heads(k_s)[None],
            new_heads(v_s)[None],
            new_rows(vn_s)[None])
```

```python
import functools

import numpy as np
import jax
import jax.numpy as jnp
from jax import lax
from jax.experimental import pallas as pl
from jax.experimental.pallas import tpu as pltpu

F32 = jnp.float32
BF16 = jnp.bfloat16

EPS = 1e-6
LANES = 128
SUBLANES = 8
HEAD_DIM = 128
N_HEADS = 8
ATT_WIDTH = N_HEADS * HEAD_DIM
BRANCHES = ((128, 1), (512, 4), (2048, 16))
BAND = 128
SPAN = BAND * BRANCHES[-1][1]
N_KEYS = 128
TOPK = 16
NEG = -1e30
VMEM_LIMIT = 56 * 1024 * 1024
COEF_LANE_CHUNKS = 16
SELECT_HEADS = 4
SLOTS = SUBLANES
SAMPLE_ROWS = LANES
EXPERT_TILE = SUBLANES * N_KEYS
PROMPT_TILES = dict(tm_in=512, tm_mix=512, tm_sel=512, tt=512)


def _params(sem):
    return pltpu.CompilerParams(dimension_semantics=sem, vmem_limit_bytes=VMEM_LIMIT)


def _gelu(x):
    return 0.5 * x * (1.0 + lax.erf(x * (2.0 ** -0.5)))


def _rms(x, g):
    return x * lax.rsqrt(jnp.mean(x * x, axis=-1, keepdims=True) + EPS) * g


def _slope(h):
    return 2.0 ** (-8.0 * (h + 1) / N_HEADS)


def _adaln_kernel(c_ref, w_ref, b_ref, o_ref):
    c = c_ref[...]
    a = c / (1.0 + jnp.exp(-c))
    o_ref[...] = jnp.dot(a.astype(BF16), w_ref[...].astype(BF16),
                         preferred_element_type=F32) + b_ref[...]


def _adaln(c_all, w_ada, b_ada):
    rows, d = c_all.shape
    n = w_ada.shape[1]
    tn = 1024
    return pl.pallas_call(
        _adaln_kernel,
        grid=(n // tn,),
        in_specs=[pl.BlockSpec((rows, d), lambda j: (0, 0)),
                  pl.BlockSpec((d, tn), lambda j: (0, j)),
                  pl.BlockSpec((1, tn), lambda j: (0, j))],
        out_specs=pl.BlockSpec((rows, tn), lambda j: (0, j)),
        out_shape=jax.ShapeDtypeStruct((rows, n), F32),
        compiler_params=_params(("arbitrary",)),
        name="adaln",
    )(c_all, w_ada, b_ada)


def _inproj_kernel(x_ref, sh_ref, sc_ref, g1_ref, w_ref, lng_ref, lnb_ref,
                   q_ref, k_ref, v_ref, u_ref, vn_ref, *rest, win_tiles):
    if win_tiles:
        kwin_ref, vwin_ref, h_scr = rest
        tpg, wt = win_tiles
        in_window = pl.program_id(0) % tpg >= tpg - wt
    else:
        (h_scr,) = rest
    j = pl.program_id(1)

    @pl.when(j == 0)
    def _():
        y = _rms(x_ref[...], g1_ref[...])
        h_scr[...] = (y * (1.0 + sc_ref[...]) + sh_ref[...]).astype(BF16)

    p = jnp.dot(h_scr[...], w_ref[...], preferred_element_type=F32)

    def heads_out(ref, val, win_ref=None):
        for h in range(N_HEADS):
            ref[h] = val[:, h * HEAD_DIM:(h + 1) * HEAD_DIM]
        if win_ref is not None:
            @pl.when(in_window)
            def _():
                for h in range(N_HEADS):
                    win_ref[:, h, :] = val[:, h * HEAD_DIM:(h + 1) * HEAD_DIM]

    @pl.when(j == 0)
    def _():
        heads_out(q_ref, p * (HEAD_DIM ** -0.5))

    @pl.when(j == 1)
    def _():
        heads_out(k_ref, p, kwin_ref if win_tiles else None)

    @pl.when(j == 2)
    def _():
        heads_out(v_ref, p, vwin_ref if win_tiles else None)

    @pl.when(j == 3)
    def _():
        u_ref[...] = _gelu(p).astype(BF16)

    @pl.when(j == 4)
    def _():
        g = _gelu(p)
        mu = jnp.mean(g, axis=-1, keepdims=True)
        gc = g - mu
        var = jnp.mean(gc * gc, axis=-1, keepdims=True)
        vn_ref[...] = gc * lax.rsqrt(var + EPS) * lng_ref[...] + lnb_ref[...]


def _mod_spec(mod, tiles_per_group):
    _, r, d = mod.shape
    return pl.BlockSpec((None, r, d), lambda i, *_: (i // tiles_per_group, 0, 0))


def _inproj(x, sh1, sc1, g1, w_in, lng, lnb, tm, n_groups, window):
    t, d = x.shape
    sec = ATT_WIDTH
    n_sec = w_in.shape[1] // sec
    rows = t // n_groups
    tpg = rows // tm
    tok = lambda i, j: (i, 0)
    const = lambda i, j: (0, 0)
    out_spec = pl.BlockSpec((tm, sec), tok)
    head_spec = pl.BlockSpec((None, N_HEADS, tm, HEAD_DIM), lambda i, j: (i // tpg, 0, i % tpg, 0))
    head_shape = jax.ShapeDtypeStruct((n_groups, N_HEADS, rows, HEAD_DIM), F32)
    out_specs = [head_spec] * 3 + [out_spec] * 2
    out_shape = [head_shape] * 3 + [jax.ShapeDtypeStruct((t, sec), BF16),
                                    jax.ShapeDtypeStruct((t, sec), F32)]
    if window:
        wt = window // tm
        win_spec = pl.BlockSpec((tm, N_HEADS, HEAD_DIM), lambda i, j: (
            (i // tpg) * wt + jnp.maximum(i % tpg - (tpg - wt), 0), 0, 0))
        win_shape = jax.ShapeDtypeStruct((n_groups * window, N_HEADS, HEAD_DIM), F32)
        out_specs += [win_spec] * 2
        out_shape += [win_shape] * 2
    return pl.pallas_call(
        functools.partial(_inproj_kernel, win_tiles=(tpg, window // tm) if window else None),
        grid=(t // tm, n_sec),
        in_specs=[pl.BlockSpec((tm, d), tok),
                  _mod_spec(sh1, tpg), _mod_spec(sc1, tpg),
                  pl.BlockSpec((1, d), const),
                  pl.BlockSpec((d, sec), lambda i, j: (0, j)),
                  pl.BlockSpec((1, sec), const), pl.BlockSpec((1, sec), const)],
        out_specs=out_specs,
        out_shape=out_shape,
        scratch_shapes=[pltpu.VMEM((tm, d), BF16)],
        compiler_params=_params(("arbitrary", "arbitrary")),
        name="inproj",
    )(x, sh1, sc1, g1, w_in, lng, lnb)


def _att_bias():
    a = np.arange(BAND)[:, None]
    kk = np.arange(2 * BAND)[None, :]
    steps = BAND + a - kk
    valid = (steps >= 0) & (steps <= BAND)
    slopes = np.array([_slope(h) for h in range(N_HEADS)])
    out = []
    for _, dil in BRANCHES:
        bias = -slopes[:, None, None] * (dil * steps).astype(np.float64)[None]
        out.append(np.where(valid[None], bias, NEG))
    return jnp.asarray(np.stack(out), dtype=F32)


def _att_prompt_kernel(q_ref, kc_ref, kp_ref, vc_ref, vp_ref, bias_ref, *rest):
    n_w = (len(rest) - 4) // 2
    w_refs, o_ref, w16_refs = rest[:n_w], rest[n_w], rest[n_w + 1:2 * n_w + 1]
    m_scr, l_scr, acc_scr = rest[2 * n_w + 1:]
    for w_ref, w16_ref in zip(w_refs, w16_refs):
        w16_ref[...] = w_ref[...].astype(BF16)
    first_span = pl.program_id(2) == 0
    span = q_ref.shape[0]
    is_prev = lax.broadcasted_iota(jnp.int32, (BAND, 2 * BAND), 1) < BAND

    def rows_at(start, dil):
        return pl.ds(start, BAND) if dil == 1 else pl.ds(start, BAND, stride=dil)

    for bi, (_, dil) in enumerate(BRANCHES):
        for r in range(dil):
            for n in range(span // (BAND * dil)):
                start = r + dil * BAND * n
                rows = rows_at(start, dil)
                if n > 0:
                    prev = rows_at(start - dil * BAND, dil)
                    kp, vp = kc_ref[prev, :], vc_ref[prev, :]
                else:
                    prev = rows_at(span - dil * BAND + r, dil)
                    kp, vp = kp_ref[prev, :], vp_ref[prev, :]
                kw = jnp.concatenate([kp, kc_ref[rows, :]], axis=0).astype(BF16)
                vw = jnp.concatenate([vp, vc_ref[rows, :]], axis=0).astype(BF16)
                s = lax.dot_general(q_ref[rows, :].astype(BF16), kw, (((1,), (1,)), ((), ())),
                                    preferred_element_type=F32) + bias_ref[bi]
                if n == 0:
                    s = jnp.where(jnp.logical_and(first_span, is_prev), NEG, s)
                m_b = jnp.max(s, axis=-1, keepdims=True)
                p = jnp.exp(s - m_b)
                l_b = jnp.sum(p, axis=-1, keepdims=True)
                pv = jnp.dot(p.astype(BF16), vw, preferred_element_type=F32)
                m_scr[bi, rows, :] = jnp.broadcast_to(m_b, (BAND, LANES))
                l_scr[bi, rows, :] = jnp.broadcast_to(l_b, (BAND, LANES))
                acc_scr[bi, rows, :] = pv
    ms = [m_scr[bi] for bi in range(len(BRANCHES))]
    m_all = functools.reduce(jnp.maximum, ms)
    es = [jnp.exp(m - m_all) for m in ms]
    l_all = sum(e * l_scr[bi] for bi, e in enumerate(es))
    acc_all = sum(e * acc_scr[bi] for bi, e in enumerate(es))
    o_ref[...] = (acc_all / l_all).astype(o_ref.dtype)


def _prompt_attention(q, k, v, weights=()):
    b, nh, seq, hd = q.shape
    n_span = seq // SPAN
    steps = b * nh * n_span
    cur = lambda bb, h, s: (bb, h, s, 0)
    prev = lambda bb, h, s: (bb, h, jnp.maximum(s - 1, 0), 0)
    blk = lambda m: pl.BlockSpec((None, None, SPAN, hd), m)
    step = lambda bb, h, s: ((bb * nh + h) * n_span + s, 0)
    w_specs = []
    for wm in weights:
        assert wm.shape[0] % (16 * steps) == 0
        w_specs.append(pl.BlockSpec((wm.shape[0] // steps, wm.shape[1]), step))
    return pl.pallas_call(
        _att_prompt_kernel,
        grid=(b, nh, n_span),
        in_specs=[blk(cur), blk(cur), blk(prev), blk(cur), blk(prev),
                  pl.BlockSpec((len(BRANCHES), None, BAND, 2 * BAND), lambda bb, h, s: (0, h, 0, 0))
                  ] + w_specs,
        out_specs=[blk(cur)] + w_specs,
        out_shape=[jax.ShapeDtypeStruct(q.shape, BF16)]
        + [jax.ShapeDtypeStruct(wm.shape, BF16) for wm in weights],
        scratch_shapes=[pltpu.VMEM((len(BRANCHES), SPAN, LANES), F32)] * 3,
        compiler_params=_params(("parallel", "parallel", "arbitrary")),
        name="attention_prompt",
    )(q, k, k, v, v, _att_bias(), *weights)


def _att_sample_kernel(q_ref, kn_ref, vn_ref, kt_ref, vt_ref, kr_ref, vr_ref, o_ref, *, n_new):
    o_ref[...] = jnp.zeros_like(o_ref)
    col = lax.broadcasted_iota(jnp.int32, (BAND, 1), 0).astype(F32)
    slot = lax.broadcasted_iota(jnp.int32, (SLOTS, 1), 0)
    tail = kt_ref.shape[0] // N_HEADS
    for h in range(N_HEADS):
        slope = _slope(h)
        knh, vnh = kn_ref[h], vn_ref[h]
        for t in range(n_new):
            q_row = q_ref[h, t:t + 1, :]
            parts = []
            for _, dil in BRANCHES:
                a = t if dil == 1 else 0
                if dil == 1:
                    rows = pl.ds((tail - BAND) * N_HEADS + h, BAND, stride=N_HEADS)
                    kp, vp = kt_ref[rows, :], vt_ref[rows, :]
                elif dil * BAND == tail:
                    rows = pl.ds(t * N_HEADS + h, BAND, stride=dil * N_HEADS)
                    kp, vp = kt_ref[rows, :], vt_ref[rows, :]
                else:
                    kp, vp = kr_ref[:, t * N_HEADS + h, :], vr_ref[:, t * N_HEADS + h, :]
                steps = float(BAND + a) - col
                s_p = jnp.sum(kp * q_row, axis=1, keepdims=True) - (slope * dil) * steps
                if a > 0:
                    s_p = jnp.where(steps <= float(BAND), s_p, NEG)
                s_n = jnp.sum(knh * q_row, axis=1, keepdims=True)
                if dil == 1:
                    s_n = jnp.where(slot <= t, s_n - slope * (t - slot).astype(F32), NEG)
                else:
                    s_n = jnp.where(slot == t, s_n, NEG)
                m = jnp.maximum(jnp.max(s_p, axis=0, keepdims=True), jnp.max(s_n, axis=0, keepdims=True))
                p_p = jnp.exp(s_p - m)
                p_n = jnp.exp(s_n - m)
                l = jnp.sum(p_p, axis=0, keepdims=True) + jnp.sum(p_n, axis=0, keepdims=True)
                acc = (jnp.sum(p_p * vp, axis=0, keepdims=True)
                       + jnp.sum(p_n * vnh, axis=0, keepdims=True))
                parts.append((m, l, acc))
            m_all = functools.reduce(jnp.maximum, [m for m, _, _ in parts])
            l_all = sum(l * jnp.exp(m - m_all) for m, l, _ in parts)
            acc_all = sum(acc * jnp.exp(m - m_all) for m, _, acc in parts)
            o_ref[h, t:t + 1, :] = acc_all / l_all


def _sample_attention(q, k, v, cache_k, cache_v, n_new):
    n_batch, buf, nh, hd = cache_k.shape
    res16 = BRANCHES[-1][1]
    tail = 4 * BAND
    assert buf == res16 * BAND and n_new * nh <= 32
    rows2d = lambda c: c.reshape(n_batch, buf * nh, hd)
    resid = lambda c: c.reshape(n_batch, BAND, res16 * nh, hd)
    new_spec = pl.BlockSpec((nh, SLOTS, hd), lambda b: (0, b, 0))
    tail_spec = pl.BlockSpec((None, tail * nh, hd), lambda b: (b, buf // tail - 1, 0))
    res_spec = pl.BlockSpec((None, BAND, 32, hd), lambda b: (b, 0, 0, 0))
    return pl.pallas_call(
        functools.partial(_att_sample_kernel, n_new=n_new),
        grid=(n_batch,),
        in_specs=[new_spec, new_spec, new_spec, tail_spec, tail_spec, res_spec, res_spec],
        out_specs=pl.BlockSpec((None, nh, SLOTS, hd), lambda b: (b, 0, 0, 0)),
        out_shape=jax.ShapeDtypeStruct((n_batch, nh, SLOTS, hd), F32),
        compiler_params=_params(("parallel",)),
        name="attention_sample",
    )(q, k, v, rows2d(cache_k), rows2d(cache_v), resid(cache_k), resid(cache_v))


def _mix_kernel(att_ref, u_ref, vn_ref, x_ref, gt1_ref, sh2_ref, sc2_ref, g2_ref, wmix_ref, bmix_ref,
                wout_ref, x1_ref, h2_ref, h2t_ref, mix_scr):
    tm = x_ref.shape[0]
    for h in range(N_HEADS):
        mix_scr[:, h * HEAD_DIM:(h + 1) * HEAD_DIM] = att_ref[h].astype(BF16)
    for c in range(tm // BAND):
        rows = slice(c * BAND, (c + 1) * BAND)
        for g in range(N_HEADS):
            cs = slice(g * HEAD_DIM, (g + 1) * HEAD_DIM)
            m = jnp.dot(wmix_ref[g], vn_ref[rows, cs].astype(BF16),
                        preferred_element_type=F32) + bmix_ref[g]
            mix_scr[rows, ATT_WIDTH + g * HEAD_DIM:ATT_WIDTH + (g + 1) * HEAD_DIM] = (
                u_ref[rows, cs].astype(F32) * m).astype(BF16)
    y = jnp.dot(mix_scr[...], wout_ref[...], preferred_element_type=F32)
    x1 = x_ref[...] + gt1_ref[...] * y
    x1_ref[...] = x1
    h2 = _rms(x1, g2_ref[...]) * (1.0 + sc2_ref[...]) + sh2_ref[...]
    h2_ref[...] = h2.astype(BF16)
    h2t_ref[...] = h2.T.astype(BF16)


def _mix(att, u, vn, x, gt1, sh2, sc2, g2, wmix, bmix, w_out, tm):
    t, d = x.shape
    tpg = att.shape[2] // tm
    tok = lambda i: (i, 0)
    c2 = lambda i: (0, 0)
    c3 = lambda i: (0, 0, 0)
    return pl.pallas_call(
        _mix_kernel,
        grid=(t // tm,),
        in_specs=[pl.BlockSpec((None, N_HEADS, tm, HEAD_DIM), lambda i: (i // tpg, 0, i % tpg, 0)),
                  pl.BlockSpec((tm, ATT_WIDTH), tok), pl.BlockSpec((tm, ATT_WIDTH), tok),
                  pl.BlockSpec((tm, d), tok),
                  _mod_spec(gt1, tpg), _mod_spec(sh2, tpg), _mod_spec(sc2, tpg),
                  pl.BlockSpec((1, d), c2),
                  pl.BlockSpec(wmix.shape, c3), pl.BlockSpec(bmix.shape, c3),
                  pl.BlockSpec(w_out.shape, c2)],
        out_specs=[pl.BlockSpec((tm, d), tok), pl.BlockSpec((tm, d), tok),
                   pl.BlockSpec((d, tm), lambda i: (0, i))],
        out_shape=[jax.ShapeDtypeStruct((t, d), F32),
                   jax.ShapeDtypeStruct((t, d), BF16),
                   jax.ShapeDtypeStruct((d, t), BF16)],
        scratch_shapes=[pltpu.VMEM((tm, d), BF16)],
        compiler_params=_params(("parallel",)),
        name="mix_outproj",
    )(att, u, vn, x, gt1, sh2, sc2, g2, wmix, bmix, w_out)


def _cand_layout():
    rows = []
    rows += [(0, j) for j in range(16)]
    rows += [(1, j) for j in range(8)]
    rows += [(8 + r, 0) for r in range(8)]
    rows += [(2, j) if j < 5 else None for j in range(8)]
    rows += [(3, j) if j < 4 else None for j in range(8)]
    rows += [(4, j) if j < 3 else None for j in range(8)]
    rows += [(r, 0) if 5 <= r < 8 else None for r in range(8)]
    rows += [(r, 1) if 5 <= r < 8 else None for r in range(8)]
    return rows


N_CAND_ROWS = 72
BIG_FLAT = 1e9
NO_EXPERT = 1e30


def _cand_flat():
    flat = np.full((N_CAND_ROWS, LANES), BIG_FLAT, np.float32)
    for r, cell in enumerate(_cand_layout()):
        if cell is not None:
            flat[r, :] = 16 * cell[0] + cell[1]
    return jnp.asarray(flat)


def _top16_ranked(s, row_id, top_ref):
    rank = jnp.full(s.shape, float(TOPK), F32)
    for it in range(TOPK):
        m = jnp.max(s, axis=0, keepdims=True)
        idx = jnp.min(jnp.where(s == m, row_id, float(N_KEYS)), axis=0, keepdims=True)
        sel = row_id == idx
        rank = jnp.where(sel, float(it), rank)
        s = jnp.where(sel, -jnp.inf, s)
        top_ref[it:it + 1, :] = m
    return rank


def _sort16_network():
    def merge(lo, hi, r):
        step = r * 2
        if step < hi - lo:
            yield from merge(lo, hi, step)
            yield from merge(lo + r, hi, step)
            yield from [(i, i + r) for i in range(lo + r, hi - r, step)]
        else:
            yield (lo, lo + r)

    def sort(lo, hi):
        if hi > lo:
            mid = lo + (hi - lo) // 2
            yield from sort(lo, mid)
            yield from sort(mid + 1, hi)
            yield from merge(lo, hi, 1)

    return list(sort(0, TOPK - 1))


def _top16_sorted(s, top_ref):
    n = N_KEYS // SUBLANES
    cols = [s[SUBLANES * v:SUBLANES * (v + 1), :] for v in range(n)]
    for i, j in _sort16_network():
        cols[i], cols[j] = jnp.maximum(cols[i], cols[j]), jnp.minimum(cols[i], cols[j])
    ties = jnp.zeros((1, s.shape[1]), F32)
    prev = None
    for it in range(TOPK + 1):
        m = jnp.max(cols[0], axis=0, keepdims=True)
        if prev is not None:
            ties = ties + jnp.where(m == prev, 1.0, 0.0)
        prev = m
        if it == TOPK:
            break
        sel = cols[0] == m
        ties = ties + jnp.sum(jnp.where(sel, 1.0, 0.0), axis=0, keepdims=True) - 1.0
        top_ref[it:it + 1, :] = m
        for v in range(TOPK - it):
            nxt = cols[v + 1] if v + 1 < n else jnp.full_like(cols[v], -jnp.inf)
            cols[v] = jnp.where(sel, nxt, cols[v])
    return ties


def _select_block(h, lc, toks, exact, keys_ref, qp_scr, top1_scr, top2_scr, flat, row_id, row8,
                  thr_ref, g1_ref, s2_ref, e2_ref):
    valid = flat < BIG_FLAT
    sides = []
    for p, top_scr in ((0, top1_scr), (1, top2_scr)):
        col = pl.multiple_of((2 * h + p) * N_KEYS, N_KEYS)
        s = lax.dot_general(keys_ref[p], qp_scr[toks, pl.ds(col, N_KEYS)],
                            (((1,), (1,)), ((), ())), preferred_element_type=F32)
        sides.append((s, _top16_ranked(s, row_id, top_scr) if exact else _top16_sorted(s, top_scr)))
    (s1, info1), (s2, info2) = sides
    a1 = top1_scr[...]
    a2 = top2_scr[...]
    v = jnp.concatenate([
        a1[0:1] + a2[0:16],
        a1[1:2] + a2[0:8],
        a1[8:16] + a2[0:1],
        a1[2:3] + a2[0:8],
        a1[3:4] + a2[0:8],
        a1[4:5] + a2[0:8],
        a1[0:8] + a2[0:1],
        a1[0:8] + a2[1:2]], axis=0)
    vw = jnp.where(valid, v, -jnp.inf)
    selm = jnp.zeros(v.shape, F32)
    for _ in range(TOPK):
        m = jnp.max(vw, axis=0, keepdims=True)
        sel = vw == m
        if exact:
            f = jnp.min(jnp.where(sel, flat, BIG_FLAT), axis=0, keepdims=True)
            sel = flat == f
        selm = jnp.where(sel, 1.0, selm)
        vw = jnp.where(sel, -jnp.inf, vw)
    vmax = a1[0:1] + a2[0:1]
    z = jnp.sum(jnp.where(selm > 0.0, jnp.exp(v - vmax), 0.0), axis=0, keepdims=True)
    cnt_lo = selm[56:64] + selm[64:72]
    for i, (lo, hi) in enumerate(((0, 16), (16, 24), (32, 40), (40, 48), (48, 56))):
        ci = jnp.sum(selm[lo:hi], axis=0, keepdims=True)
        cnt_lo = jnp.where(row8 == i, ci, cnt_lo)
    cnt = jnp.concatenate([cnt_lo, selm[24:32]], axis=0)
    if exact:
        thr = jnp.where(cnt > 0.0, 0.5 - cnt, NO_EXPERT)
        key1, key2 = info1, -info2
        match = [float(i) for i in range(TOPK)]
    else:
        thr = jnp.full(cnt.shape, NO_EXPERT, F32)
        for j in range(TOPK):
            thr = jnp.where(cnt == float(j + 1), a2[j:j + 1], thr)
        key1, key2 = s1, s2
        match = [a1[i:i + 1] for i in range(TOPK)]
    thr_a = jnp.full((N_KEYS, LANES), NO_EXPERT, F32)
    for i in range(TOPK):
        thr_a = jnp.where(key1 == match[i], thr[i:i + 1], thr_a)
    thr_ref[h, lc] = thr_a
    g1_ref[h, lc] = jnp.exp(s1 - a1[0:1]) / z
    s2_ref[h, lc] = key2
    e2_ref[h, lc] = jnp.exp(s2 - a2[0:1])
    if exact:
        return None
    n3 = jnp.sum(selm, axis=0, keepdims=True) - float(TOPK)
    return jnp.max(info1 + info2 + n3)


def _select_kernel(h2_ref, wq_ref, keys_ref, flat_ref, thr_ref, g1_ref, s2_ref, e2_ref,
                   qp_scr, top1_scr, top2_scr):
    tm = h2_ref.shape[0]
    qp_scr[...] = jnp.dot(h2_ref[...], wq_ref[...], preferred_element_type=F32).astype(BF16)
    row_id = lax.broadcasted_iota(jnp.int32, (N_KEYS, LANES), 0).astype(F32)
    row8 = lax.broadcasted_iota(jnp.int32, (8, LANES), 0)
    flat = flat_ref[...]
    blocks = []
    for lc in range(tm // LANES):
        toks = slice(lc * LANES, (lc + 1) * LANES)
        blocks.append(functools.partial(
            _select_block, lc=lc, toks=toks, keys_ref=keys_ref, qp_scr=qp_scr, top1_scr=top1_scr.at[lc],
            top2_scr=top2_scr.at[lc], flat=flat, row_id=row_id, row8=row8, thr_ref=thr_ref,
            g1_ref=g1_ref, s2_ref=s2_ref, e2_ref=e2_ref))

    def head_group_body(hh, carry):
        work = [(block, SELECT_HEADS * hh + k) for k in range(SELECT_HEADS) for block in blocks]
        ties = [block(h, exact=False) for block, h in work]
        for (block, h), tie in zip(work, ties):
            @pl.when(tie > 0.5)
            def _(block=block, h=h):
                block(h, exact=True)

        return carry

    lax.fori_loop(0, N_HEADS // SELECT_HEADS, head_group_body, 0)


def _select(h2, w_query, sub_keys, tm):
    t, d = h2.shape
    out_spec = pl.BlockSpec((N_HEADS, tm // LANES, N_KEYS, LANES), lambda i: (0, i, 0, 0))
    slab = lambda dt: jax.ShapeDtypeStruct((N_HEADS, t // LANES, N_KEYS, LANES), dt)
    return pl.pallas_call(
        _select_kernel,
        grid=(t // tm,),
        in_specs=[pl.BlockSpec((tm, d), lambda i: (i, 0)),
                  pl.BlockSpec(w_query.shape, lambda i: (0, 0)),
                  pl.BlockSpec(sub_keys.shape, lambda i: (0, 0, 0)),
                  pl.BlockSpec((N_CAND_ROWS, LANES), lambda i: (0, 0))],
        out_specs=[out_spec] * 4,
        out_shape=[slab(F32)] * 4,
        scratch_shapes=[pltpu.VMEM((tm, w_query.shape[1]), BF16),
                        pltpu.VMEM((tm // LANES, TOPK, LANES), F32),
                        pltpu.VMEM((tm // LANES, TOPK, LANES), F32)],
        compiler_params=_params(("parallel",)),
        name="peer_select",
    )(h2, w_query, sub_keys, _cand_flat())


def _coef_kernel(thr_ref, g1_ref, s2_ref, e2_ref, *rest, cast):
    if cast:
        u_ref, v_ref, c_ref, u16_ref, vt16_ref = rest
        u16_ref[...] = u_ref[...].astype(BF16)
        step = pl.program_id(0) * pl.num_programs(1) + pl.program_id(1)

        v_steps = vt16_ref.shape[1] // u_ref.shape[0]

        @pl.when(step % v_steps == 0)
        def _():
            vt16_ref[...] = v_ref[...].T.astype(BF16)
    else:
        (c_ref,) = rest
    n_l, n_a = thr_ref.shape[1], thr_ref.shape[2]
    for lc in range(n_l):
        for ai in range(n_a):
            rows = slice(ai * N_KEYS, (ai + 1) * N_KEYS)
            coef = jnp.zeros((N_KEYS, LANES), F32)
            for h in range(N_HEADS):
                thr = thr_ref[h, lc, ai:ai + 1, :]
                g1 = g1_ref[h, lc, ai:ai + 1, :]
                coef = coef + jnp.where(s2_ref[h, lc] >= thr, e2_ref[h, lc], 0.0) * g1
            c_ref[lc, rows, :] = coef.astype(BF16)


def _coef(sel, te, n_exp, tables=None):
    n_lc = sel[0].shape[1]
    n_l = min(n_lc, COEF_LANE_CHUNKS)
    grid = (n_lc // n_l, n_exp // te)
    a_spec = pl.BlockSpec((N_HEADS, n_l, te // N_KEYS, LANES), lambda i, j: (0, i, j, 0))
    b_spec = pl.BlockSpec((N_HEADS, n_l, N_KEYS, LANES), lambda i, j: (0, i, 0, 0))
    in_specs = [a_spec, a_spec, b_spec, b_spec]
    out_specs = [pl.BlockSpec((n_l, te, LANES), lambda i, j: (i, j, 0))]
    out_shape = [jax.ShapeDtypeStruct((n_lc, n_exp, LANES), BF16)]
    operands = list(sel)
    if tables is not None:
        d = tables[0].shape[1]
        steps = grid[0] * grid[1]
        u_rows = n_exp // steps
        v_rows = max(u_rows, LANES)
        assert u_rows * steps == n_exp and v_rows % u_rows == 0
        v_steps = v_rows // u_rows
        step = lambda i, j: i * grid[1] + j
        in_specs += [pl.BlockSpec((u_rows, d), lambda i, j: (step(i, j), 0)),
                     pl.BlockSpec((v_rows, d), lambda i, j: (step(i, j) // v_steps, 0))]
        out_specs += [pl.BlockSpec((u_rows, d), lambda i, j: (step(i, j), 0)),
                      pl.BlockSpec((d, v_rows), lambda i, j: (0, step(i, j) // v_steps))]
        out_shape += [jax.ShapeDtypeStruct((n_exp, d), BF16), jax.ShapeDtypeStruct((d, n_exp), BF16)]
        operands += list(tables)
    return pl.pallas_call(
        functools.partial(_coef_kernel, cast=tables is not None),
        grid=grid,
        in_specs=in_specs,
        out_specs=out_specs,
        out_shape=out_shape,
        compiler_params=_params(("arbitrary", "arbitrary")),
        name="peer_coef",
    )(*operands)


def _peer_kernel(h2t_ref, u_ref, vt_ref, c_ref, x1_ref, gt2_ref, gf_ref, y_ref, acc_scr):
    j = pl.program_id(1)
    tt = h2t_ref.shape[1]

    @pl.when(j == 0)
    def _():
        acc_scr[...] = jnp.zeros_like(acc_scr)

    act = jnp.dot(u_ref[...], h2t_ref[...], preferred_element_type=F32)
    p = jnp.concatenate(
        [_gelu(act[:, lc * LANES:(lc + 1) * LANES]).astype(BF16) * c_ref[lc]
         for lc in range(tt // LANES)], axis=1)
    acc_scr[...] += jnp.dot(vt_ref[...], p, preferred_element_type=F32)

    @pl.when(j == pl.num_programs(1) - 1)
    def _():
        x2 = x1_ref[...] + gt2_ref[...] * acc_scr[...].T
        y_ref[...] = _rms(x2, gf_ref[...])


def _peer(h2t, experts, sel, x1, gt2, g_final, rows_per_group, tt, te):
    kind, u, vt = experts
    n_exp = u.shape[0]
    if kind == "f32":
        coef, u, vt = _coef(sel, te, n_exp, tables=(u, vt))
    else:
        (coef,) = _coef(sel, te, n_exp)
    d, t = h2t.shape
    y = pl.pallas_call(
        _peer_kernel,
        grid=(t // tt, n_exp // te),
        in_specs=[pl.BlockSpec((d, tt), lambda i, j: (0, i)),
                  pl.BlockSpec((te, d), lambda i, j: (j, 0)),
                  pl.BlockSpec((d, te), lambda i, j: (0, j)),
                  pl.BlockSpec((tt // LANES, te, LANES), lambda i, j: (i, j, 0)),
                  pl.BlockSpec((tt, d), lambda i, j: (i, 0)),
                  _mod_spec(gt2, rows_per_group // tt),
                  pl.BlockSpec((1, d), lambda i, j: (0, 0))],
        out_specs=pl.BlockSpec((tt, d), lambda i, j: (i, 0)),
        out_shape=jax.ShapeDtypeStruct((t, d), F32),
        scratch_shapes=[pltpu.VMEM((d, tt), F32)],
        compiler_params=_params(("parallel", "arbitrary")),
        name="peer_dense",
    )(h2t, u, vt, coef, x1, gt2, g_final)
    return y, ("bf16", u, vt)


def _trunk(x, mods, n_groups, window, att_fn, w, experts, wmix, bmix, tm_in, tm_mix, tm_sel, tt, te):
    sh1, sc1, gt1, sh2, sc2, gt2 = mods
    rows = x.shape[0] // n_groups
    q, k, v, u, vn, *wins = _inproj(x, sh1, sc1, w["g1"], w["w_in"], w["lng"], w["lnb"], tm_in,
                                    n_groups, window)
    att, w_out, w_query = att_fn(q, k, v)
    x1, h2, h2t = _mix(att, u, vn, x, gt1, sh2, sc2, w["g2"], wmix, bmix, w_out, tm_mix)
    sel = _select(h2, w_query, w["sub_keys"], tm_sel)
    y, experts = _peer(h2t, experts, sel, x1, gt2, w["g_final"], rows, tt, te)
    return y, k, v, vn, wins, experts, (w_out, w_query)


def kernel(x_prompt, x_sample, cache_k, cache_v, c_prompt, c_sample, w_ada, b_ada, g_norm1, w_in,
           ln_v_g, ln_v_b, w_spatial, b_spatial, w_out, g_norm2, w_query, sub_keys, expert_u,
           expert_v, g_final):
    depth = w_ada.shape[0]
    assert depth == 1, "single-layer trunk"
    batch, seq, d = x_prompt.shape
    n_dec, n_new, _ = x_sample.shape
    buf = cache_k.shape[2]
    assert seq % SPAN == 0 and buf == SPAN and n_new <= SLOTS
    win = min(BRANCHES[-1][0], seq)
    pad_s = SAMPLE_ROWS
    assert n_dec * SLOTS <= pad_s

    n_c = batch + n_dec
    c_all = jnp.pad(jnp.concatenate([c_prompt, c_sample], axis=0), ((0, -n_c % SUBLANES), (0, 0)))
    mod = _adaln(c_all, w_ada[0], b_ada[0][None, :])
    mods = [mod[:, i * d:(i + 1) * d] for i in range(6)]
    mods_p = [m[:batch, None, :] for m in mods]
    mods_s = [jnp.pad(jnp.repeat(m[batch:n_c], SLOTS, axis=0), ((0, pad_s - n_dec * SLOTS), (0, 0)))[None]
              for m in mods]

    tri = jnp.tril(jnp.ones((BAND, BAND), dtype=bool))
    ws = jnp.where(tri, w_spatial[0], 0.0)
    bs = b_spatial[0]
    wmix_p = ws.astype(BF16)
    bmix_p = jnp.broadcast_to(bs[:, :, None], (N_HEADS, BAND, HEAD_DIM))
    eye = jnp.eye(pad_s // SLOTS, dtype=F32)
    slot_pad = ((0, 0), (0, SLOTS - n_new))
    ws_new = jnp.pad(ws[:, :n_new, :n_new], ((0, 0),) + slot_pad[1:] + slot_pad[1:])
    wmix_s = jax.vmap(lambda m: jnp.kron(eye, m))(ws_new).astype(BF16)
    bmix_s = jnp.broadcast_to(jnp.tile(jnp.pad(bs[:, :n_new], slot_pad), (1, pad_s // SLOTS))[:, :, None],
                              (N_HEADS, pad_s, HEAD_DIM))

    w = dict(g1=g_norm1[0][None, :], w_in=w_in[0].astype(BF16), lng=ln_v_g[0][None, :],
             lnb=ln_v_b[0][None, :], g2=g_norm2[0][None, :], sub_keys=sub_keys[0].astype(BF16),
             g_final=g_final[None, :])

    xp = x_prompt.reshape(batch * seq, d)
    prompt_att = functools.partial(_prompt_attention, weights=(w_out[0], w_query[0]))
    y_p, _, _, _, (k_win, v_win), experts, w_bf16 = _trunk(
        xp, mods_p, batch, win, prompt_att, w, ("f32", expert_u[0], expert_v[0]), wmix_p, bmix_p,
        te=EXPERT_TILE, **PROMPT_TILES)

    xs = jnp.pad(x_sample, ((0, 0), (0, SLOTS - n_new), (0, 0))).reshape(n_dec * SLOTS, d)
    xs = jnp.pad(xs, ((0, pad_s - n_dec * SLOTS), (0, 0)))

    def sample_att(q, k, v):
        o = _sample_attention(q[0], k[0], v[0], cache_k[0], cache_v[0], n_new)
        o = o.transpose(1, 0, 2, 3).reshape(N_HEADS, n_dec * SLOTS, HEAD_DIM)
        return (jnp.pad(o, ((0, 0), (0, pad_s - n_dec * SLOTS), (0, 0)))[None],) + w_bf16

    y_s, k_s, v_s, vn_s, _, _, _ = _trunk(
        xs, mods_s, 1, 0, sample_att, w, experts, wmix_s, bmix_s,
        tm_in=pad_s, tm_mix=pad_s, tm_sel=pad_s, tt=pad_s, te=EXPERT_TILE)

    new_rows = lambda t: t[:n_dec * SLOTS].reshape(n_dec, SLOTS, -1)[:, :n_new]
    new_heads = lambda t: t[0, :, :n_dec * SLOTS].reshape(N_HEADS, n_dec, SLOTS, HEAD_DIM)[:, :, :n_new
                                                                                         ].transpose(1, 2, 0, 3)
    return (y_p.reshape(batch, seq, d),
            new_rows(y_s),
            k_win.reshape(1, batch, win, N_HEADS, HEAD_DIM),
            v_win.reshape(1, batch, win, N_HEADS, HEAD_DIM),
            new_heads(k_s)[None],
            new_heads(v_s)[None],
            new_rows(vn_s)[None])
```

```python
import functools

import numpy as np
import jax
import jax.numpy as jnp
from jax import lax
from jax.experimental import pallas as pl
from jax.experimental.pallas import tpu as pltpu

F32 = jnp.float32
BF16 = jnp.bfloat16

EPS = 1e-6
LANES = 128
SUBLANES = 8
HEAD_DIM = 128
N_HEADS = 8
ATT_WIDTH = N_HEADS * HEAD_DIM
BRANCHES = ((128, 1), (512, 4), (2048, 16))
BAND = 128
SPAN = BAND * BRANCHES[-1][1]
N_KEYS = 128
TOPK = 16
NEG = -1e30
VMEM_LIMIT = 56 * 1024 * 1024
DENSE_EXPERT_TILES = 2
DENSE_VMEM_LIMIT = 62 * 1024 * 1024
COEF_LANE_CHUNKS = 16
SLOTS = SUBLANES
SAMPLE_ROWS = LANES
EXPERT_TILE = SUBLANES * N_KEYS
PROMPT_TILES = dict(tm_in=512, tm_mix=512, tm_sel=512, tt=512)


def _params(sem):
    return pltpu.CompilerParams(dimension_semantics=sem, vmem_limit_bytes=VMEM_LIMIT)


def _gelu(x):
    return 0.5 * x * (1.0 + lax.erf(x * (2.0 ** -0.5)))


def _rms(x, g):
    return x * lax.rsqrt(jnp.mean(x * x, axis=-1, keepdims=True) + EPS) * g


def _slope(h):
    return 2.0 ** (-8.0 * (h + 1) / N_HEADS)


def _adaln_kernel(c_ref, w_ref, b_ref, o_ref):
    c = c_ref[...]
    a = c / (1.0 + jnp.exp(-c))
    o_ref[...] = jnp.dot(a.astype(BF16), w_ref[...].astype(BF16),
                         preferred_element_type=F32) + b_ref[...]


def _adaln(c_all, w_ada, b_ada):
    rows, d = c_all.shape
    n = w_ada.shape[1]
    tn = 1024
    return pl.pallas_call(
        _adaln_kernel,
        grid=(n // tn,),
        in_specs=[pl.BlockSpec((rows, d), lambda j: (0, 0)),
                  pl.BlockSpec((d, tn), lambda j: (0, j)),
                  pl.BlockSpec((1, tn), lambda j: (0, j))],
        out_specs=pl.BlockSpec((rows, tn), lambda j: (0, j)),
        out_shape=jax.ShapeDtypeStruct((rows, n), F32),
        compiler_params=_params(("arbitrary",)),
        name="adaln",
    )(c_all, w_ada, b_ada)


def _inproj_kernel(x_ref, sh_ref, sc_ref, g1_ref, w_ref, lng_ref, lnb_ref,
                   q_ref, k_ref, v_ref, u_ref, vn_ref, *rest, win_tiles):
    if win_tiles:
        kwin_ref, vwin_ref, h_scr = rest
        tpg, wt = win_tiles
        in_window = pl.program_id(0) % tpg >= tpg - wt
    else:
        (h_scr,) = rest
    j = pl.program_id(1)

    @pl.when(j == 0)
    def _():
        y = _rms(x_ref[...], g1_ref[...])
        h_scr[...] = (y * (1.0 + sc_ref[...]) + sh_ref[...]).astype(BF16)

    p = jnp.dot(h_scr[...], w_ref[...], preferred_element_type=F32)

    def heads_out(ref, val, win_ref=None):
        for h in range(N_HEADS):
            ref[h] = val[:, h * HEAD_DIM:(h + 1) * HEAD_DIM]
        if win_ref is not None:
            @pl.when(in_window)
            def _():
                for h in range(N_HEADS):
                    win_ref[:, h, :] = val[:, h * HEAD_DIM:(h + 1) * HEAD_DIM]

    @pl.when(j == 0)
    def _():
        heads_out(q_ref, p * (HEAD_DIM ** -0.5))

    @pl.when(j == 1)
    def _():
        heads_out(k_ref, p, kwin_ref if win_tiles else None)

    @pl.when(j == 2)
    def _():
        heads_out(v_ref, p, vwin_ref if win_tiles else None)

    @pl.when(j == 3)
    def _():
        u_ref[...] = _gelu(p).astype(BF16)

    @pl.when(j == 4)
    def _():
        g = _gelu(p)
        mu = jnp.mean(g, axis=-1, keepdims=True)
        gc = g - mu
        var = jnp.mean(gc * gc, axis=-1, keepdims=True)
        vn_ref[...] = gc * lax.rsqrt(var + EPS) * lng_ref[...] + lnb_ref[...]


def _mod_spec(mod, tiles_per_group):
    _, r, d = mod.shape
    return pl.BlockSpec((None, r, d), lambda i, *_: (i // tiles_per_group, 0, 0))


def _inproj(x, sh1, sc1, g1, w_in, lng, lnb, tm, n_groups, window):
    t, d = x.shape
    sec = ATT_WIDTH
    n_sec = w_in.shape[1] // sec
    rows = t // n_groups
    tpg = rows // tm
    tok = lambda i, j: (i, 0)
    const = lambda i, j: (0, 0)
    out_spec = pl.BlockSpec((tm, sec), tok)
    head_spec = pl.BlockSpec((None, N_HEADS, tm, HEAD_DIM), lambda i, j: (i // tpg, 0, i % tpg, 0))
    head_shape = jax.ShapeDtypeStruct((n_groups, N_HEADS, rows, HEAD_DIM), F32)
    out_specs = [head_spec] * 3 + [out_spec] * 2
    out_shape = [head_shape] * 3 + [jax.ShapeDtypeStruct((t, sec), BF16),
                                    jax.ShapeDtypeStruct((t, sec), F32)]
    if window:
        wt = window // tm
        win_spec = pl.BlockSpec((tm, N_HEADS, HEAD_DIM), lambda i, j: (
            (i // tpg) * wt + jnp.maximum(i % tpg - (tpg - wt), 0), 0, 0))
        win_shape = jax.ShapeDtypeStruct((n_groups * window, N_HEADS, HEAD_DIM), F32)
        out_specs += [win_spec] * 2
        out_shape += [win_shape] * 2
    return pl.pallas_call(
        functools.partial(_inproj_kernel, win_tiles=(tpg, window // tm) if window else None),
        grid=(t // tm, n_sec),
        in_specs=[pl.BlockSpec((tm, d), tok),
                  _mod_spec(sh1, tpg), _mod_spec(sc1, tpg),
                  pl.BlockSpec((1, d), const),
                  pl.BlockSpec((d, sec), lambda i, j: (0, j)),
                  pl.BlockSpec((1, sec), const), pl.BlockSpec((1, sec), const)],
        out_specs=out_specs,
        out_shape=out_shape,
        scratch_shapes=[pltpu.VMEM((tm, d), BF16)],
        compiler_params=_params(("arbitrary", "arbitrary")),
        name="inproj",
    )(x, sh1, sc1, g1, w_in, lng, lnb)


def _att_bias():
    a = np.arange(BAND)[:, None]
    kk = np.arange(2 * BAND)[None, :]
    steps = BAND + a - kk
    valid = (steps >= 0) & (steps <= BAND)
    slopes = np.array([_slope(h) for h in range(N_HEADS)])
    out = []
    for _, dil in BRANCHES:
        bias = -slopes[:, None, None] * (dil * steps).astype(np.float64)[None]
        out.append(np.where(valid[None], bias, NEG))
    return jnp.asarray(np.stack(out), dtype=F32)


def _att_prompt_kernel(q_ref, kc_ref, kp_ref, vc_ref, vp_ref, bias_ref, *rest):
    n_w = (len(rest) - 4) // 2
    w_refs, o_ref, w16_refs = rest[:n_w], rest[n_w], rest[n_w + 1:2 * n_w + 1]
    m_scr, l_scr, acc_scr = rest[2 * n_w + 1:]
    for w_ref, w16_ref in zip(w_refs, w16_refs):
        w16_ref[...] = w_ref[...].astype(BF16)
    first_span = pl.program_id(2) == 0
    span = q_ref.shape[0]
    is_prev = lax.broadcasted_iota(jnp.int32, (BAND, 2 * BAND), 1) < BAND

    def rows_at(start, dil):
        return pl.ds(start, BAND) if dil == 1 else pl.ds(start, BAND, stride=dil)

    for bi, (_, dil) in enumerate(BRANCHES):
        for r in range(dil):
            for n in range(span // (BAND * dil)):
                start = r + dil * BAND * n
                rows = rows_at(start, dil)
                if n > 0:
                    prev = rows_at(start - dil * BAND, dil)
                    kp, vp = kc_ref[prev, :], vc_ref[prev, :]
                else:
                    prev = rows_at(span - dil * BAND + r, dil)
                    kp, vp = kp_ref[prev, :], vp_ref[prev, :]
                kw = jnp.concatenate([kp, kc_ref[rows, :]], axis=0).astype(BF16)
                vw = jnp.concatenate([vp, vc_ref[rows, :]], axis=0).astype(BF16)
                s = lax.dot_general(q_ref[rows, :].astype(BF16), kw, (((1,), (1,)), ((), ())),
                                    preferred_element_type=F32) + bias_ref[bi]
                if n == 0:
                    s = jnp.where(jnp.logical_and(first_span, is_prev), NEG, s)
                m_b = jnp.max(s, axis=-1, keepdims=True)
                p = jnp.exp(s - m_b)
                l_b = jnp.sum(p, axis=-1, keepdims=True)
                pv = jnp.dot(p.astype(BF16), vw, preferred_element_type=F32)
                m_scr[bi, rows, :] = jnp.broadcast_to(m_b, (BAND, LANES))
                l_scr[bi, rows, :] = jnp.broadcast_to(l_b, (BAND, LANES))
                acc_scr[bi, rows, :] = pv
    ms = [m_scr[bi] for bi in range(len(BRANCHES))]
    m_all = functools.reduce(jnp.maximum, ms)
    es = [jnp.exp(m - m_all) for m in ms]
    l_all = sum(e * l_scr[bi] for bi, e in enumerate(es))
    acc_all = sum(e * acc_scr[bi] for bi, e in enumerate(es))
    o_ref[...] = (acc_all / l_all).astype(o_ref.dtype)


def _prompt_attention(q, k, v, weights=()):
    b, nh, seq, hd = q.shape
    n_span = seq // SPAN
    steps = b * nh * n_span
    cur = lambda bb, h, s: (bb, h, s, 0)
    prev = lambda bb, h, s: (bb, h, jnp.maximum(s - 1, 0), 0)
    blk = lambda m: pl.BlockSpec((None, None, SPAN, hd), m)
    step = lambda bb, h, s: ((bb * nh + h) * n_span + s, 0)
    w_specs = []
    for wm in weights:
        assert wm.shape[0] % (16 * steps) == 0
        w_specs.append(pl.BlockSpec((wm.shape[0] // steps, wm.shape[1]), step))
    return pl.pallas_call(
        _att_prompt_kernel,
        grid=(b, nh, n_span),
        in_specs=[blk(cur), blk(cur), blk(prev), blk(cur), blk(prev),
                  pl.BlockSpec((len(BRANCHES), None, BAND, 2 * BAND), lambda bb, h, s: (0, h, 0, 0))
                  ] + w_specs,
        out_specs=[blk(cur)] + w_specs,
        out_shape=[jax.ShapeDtypeStruct(q.shape, BF16)]
        + [jax.ShapeDtypeStruct(wm.shape, BF16) for wm in weights],
        scratch_shapes=[pltpu.VMEM((len(BRANCHES), SPAN, LANES), F32)] * 3,
        compiler_params=_params(("parallel", "parallel", "arbitrary")),
        name="attention_prompt",
    )(q, k, k, v, v, _att_bias(), *weights)


def _att_sample_kernel(q_ref, kn_ref, vn_ref, kt_ref, vt_ref, kr_ref, vr_ref, o_ref, *, n_new):
    o_ref[...] = jnp.zeros_like(o_ref)
    col = lax.broadcasted_iota(jnp.int32, (BAND, 1), 0).astype(F32)
    slot = lax.broadcasted_iota(jnp.int32, (SLOTS, 1), 0)
    tail = kt_ref.shape[0] // N_HEADS
    for h in range(N_HEADS):
        slope = _slope(h)
        knh, vnh = kn_ref[h], vn_ref[h]
        for t in range(n_new):
            q_row = q_ref[h, t:t + 1, :]
            parts = []
            for _, dil in BRANCHES:
                a = t if dil == 1 else 0
                if dil == 1:
                    rows = pl.ds((tail - BAND) * N_HEADS + h, BAND, stride=N_HEADS)
                    kp, vp = kt_ref[rows, :], vt_ref[rows, :]
                elif dil * BAND == tail:
                    rows = pl.ds(t * N_HEADS + h, BAND, stride=dil * N_HEADS)
                    kp, vp = kt_ref[rows, :], vt_ref[rows, :]
                else:
                    kp, vp = kr_ref[:, t * N_HEADS + h, :], vr_ref[:, t * N_HEADS + h, :]
                steps = float(BAND + a) - col
                s_p = jnp.sum(kp * q_row, axis=1, keepdims=True) - (slope * dil) * steps
                if a > 0:
                    s_p = jnp.where(steps <= float(BAND), s_p, NEG)
                s_n = jnp.sum(knh * q_row, axis=1, keepdims=True)
                if dil == 1:
                    s_n = jnp.where(slot <= t, s_n - slope * (t - slot).astype(F32), NEG)
                else:
                    s_n = jnp.where(slot == t, s_n, NEG)
                m = jnp.maximum(jnp.max(s_p, axis=0, keepdims=True), jnp.max(s_n, axis=0, keepdims=True))
                p_p = jnp.exp(s_p - m)
                p_n = jnp.exp(s_n - m)
                l = jnp.sum(p_p, axis=0, keepdims=True) + jnp.sum(p_n, axis=0, keepdims=True)
                acc = (jnp.sum(p_p * vp, axis=0, keepdims=True)
                       + jnp.sum(p_n * vnh, axis=0, keepdims=True))
                parts.append((m, l, acc))
            m_all = functools.reduce(jnp.maximum, [m for m, _, _ in parts])
            l_all = sum(l * jnp.exp(m - m_all) for m, l, _ in parts)
            acc_all = sum(acc * jnp.exp(m - m_all) for m, _, acc in parts)
            o_ref[h, t:t + 1, :] = acc_all / l_all


def _sample_attention(q, k, v, cache_k, cache_v, n_new):
    n_batch, buf, nh, hd = cache_k.shape
    res16 = BRANCHES[-1][1]
    tail = 4 * BAND
    assert buf == res16 * BAND and n_new * nh <= 32
    rows2d = lambda c: c.reshape(n_batch, buf * nh, hd)
    resid = lambda c: c.reshape(n_batch, BAND, res16 * nh, hd)
    new_spec = pl.BlockSpec((nh, SLOTS, hd), lambda b: (0, b, 0))
    tail_spec = pl.BlockSpec((None, tail * nh, hd), lambda b: (b, buf // tail - 1, 0))
    res_spec = pl.BlockSpec((None, BAND, 32, hd), lambda b: (b, 0, 0, 0))
    return pl.pallas_call(
        functools.partial(_att_sample_kernel, n_new=n_new),
        grid=(n_batch,),
        in_specs=[new_spec, new_spec, new_spec, tail_spec, tail_spec, res_spec, res_spec],
        out_specs=pl.BlockSpec((None, nh, SLOTS, hd), lambda b: (b, 0, 0, 0)),
        out_shape=jax.ShapeDtypeStruct((n_batch, nh, SLOTS, hd), F32),
        compiler_params=_params(("parallel",)),
        name="attention_sample",
    )(q, k, v, rows2d(cache_k), rows2d(cache_v), resid(cache_k), resid(cache_v))


def _mix_kernel(att_ref, u_ref, vn_ref, x_ref, gt1_ref, sh2_ref, sc2_ref, g2_ref, wmix_ref, bmix_ref,
                wout_ref, x1_ref, h2_ref, h2t_ref, mix_scr):
    tm = x_ref.shape[0]
    for h in range(N_HEADS):
        mix_scr[:, h * HEAD_DIM:(h + 1) * HEAD_DIM] = att_ref[h].astype(BF16)
    for c in range(tm // BAND):
        rows = slice(c * BAND, (c + 1) * BAND)
        for g in range(N_HEADS):
            cs = slice(g * HEAD_DIM, (g + 1) * HEAD_DIM)
            m = jnp.dot(wmix_ref[g], vn_ref[rows, cs].astype(BF16),
                        preferred_element_type=F32) + bmix_ref[g]
            mix_scr[rows, ATT_WIDTH + g * HEAD_DIM:ATT_WIDTH + (g + 1) * HEAD_DIM] = (
                u_ref[rows, cs].astype(F32) * m).astype(BF16)
    y = jnp.dot(mix_scr[...], wout_ref[...], preferred_element_type=F32)
    x1 = x_ref[...] + gt1_ref[...] * y
    x1_ref[...] = x1
    h2 = _rms(x1, g2_ref[...]) * (1.0 + sc2_ref[...]) + sh2_ref[...]
    h2_ref[...] = h2.astype(BF16)
    h2t_ref[...] = h2.T.astype(BF16)


def _mix(att, u, vn, x, gt1, sh2, sc2, g2, wmix, bmix, w_out, tm):
    t, d = x.shape
    tpg = att.shape[2] // tm
    tok = lambda i: (i, 0)
    c2 = lambda i: (0, 0)
    c3 = lambda i: (0, 0, 0)
    return pl.pallas_call(
        _mix_kernel,
        grid=(t // tm,),
        in_specs=[pl.BlockSpec((None, N_HEADS, tm, HEAD_DIM), lambda i: (i // tpg, 0, i % tpg, 0)),
                  pl.BlockSpec((tm, ATT_WIDTH), tok), pl.BlockSpec((tm, ATT_WIDTH), tok),
                  pl.BlockSpec((tm, d), tok),
                  _mod_spec(gt1, tpg), _mod_spec(sh2, tpg), _mod_spec(sc2, tpg),
                  pl.BlockSpec((1, d), c2),
                  pl.BlockSpec(wmix.shape, c3), pl.BlockSpec(bmix.shape, c3),
                  pl.BlockSpec(w_out.shape, c2)],
        out_specs=[pl.BlockSpec((tm, d), tok), pl.BlockSpec((tm, d), tok),
                   pl.BlockSpec((d, tm), lambda i: (0, i))],
        out_shape=[jax.ShapeDtypeStruct((t, d), F32),
                   jax.ShapeDtypeStruct((t, d), BF16),
                   jax.ShapeDtypeStruct((d, t), BF16)],
        scratch_shapes=[pltpu.VMEM((tm, d), BF16)],
        compiler_params=_params(("parallel",)),
        name="mix_outproj",
    )(att, u, vn, x, gt1, sh2, sc2, g2, wmix, bmix, w_out)


def _cand_layout():
    rows = []
    rows += [(0, j) for j in range(16)]
    rows += [(1, j) for j in range(8)]
    rows += [(8 + r, 0) for r in range(8)]
    rows += [(2, j) if j < 5 else None for j in range(8)]
    rows += [(3, j) if j < 4 else None for j in range(8)]
    rows += [(4, j) if j < 3 else None for j in range(8)]
    rows += [(r, 0) if 5 <= r < 8 else None for r in range(8)]
    rows += [(r, 1) if 5 <= r < 8 else None for r in range(8)]
    return rows


N_CAND_ROWS = 72
BIG_FLAT = 1e9
NO_EXPERT = 1e30


def _cand_flat():
    flat = np.full((N_CAND_ROWS, LANES), BIG_FLAT, np.float32)
    for r, cell in enumerate(_cand_layout()):
        if cell is not None:
            flat[r, :] = 16 * cell[0] + cell[1]
    return jnp.asarray(flat)


def _top16_ranked(s, row_id, top_ref):
    rank = jnp.full(s.shape, float(TOPK), F32)
    for it in range(TOPK):
        m = jnp.max(s, axis=0, keepdims=True)
        idx = jnp.min(jnp.where(s == m, row_id, float(N_KEYS)), axis=0, keepdims=True)
        sel = row_id == idx
        rank = jnp.where(sel, float(it), rank)
        s = jnp.where(sel, -jnp.inf, s)
        top_ref[it:it + 1, :] = m
    return rank


def _sort16_network():
    def merge(lo, hi, r):
        step = r * 2
        if step < hi - lo:
            yield from merge(lo, hi, step)
            yield from merge(lo + r, hi, step)
            yield from [(i, i + r) for i in range(lo + r, hi - r, step)]
        else:
            yield (lo, lo + r)

    def sort(lo, hi):
        if hi > lo:
            mid = lo + (hi - lo) // 2
            yield from sort(lo, mid)
            yield from sort(mid + 1, hi)
            yield from merge(lo, hi, 1)

    return list(sort(0, TOPK - 1))


def _top16_sorted(s, top_ref):
    n = N_KEYS // SUBLANES
    cols = [s[SUBLANES * v:SUBLANES * (v + 1), :] for v in range(n)]
    for i, j in _sort16_network():
        cols[i], cols[j] = jnp.maximum(cols[i], cols[j]), jnp.minimum(cols[i], cols[j])
    ties = jnp.zeros((1, s.shape[1]), F32)
    prev = None
    for it in range(TOPK + 1):
        m = jnp.max(cols[0], axis=0, keepdims=True)
        if prev is not None:
            ties = ties + jnp.where(m == prev, 1.0, 0.0)
        prev = m
        if it == TOPK:
            break
        sel = cols[0] == m
        ties = ties + jnp.sum(jnp.where(sel, 1.0, 0.0), axis=0, keepdims=True) - 1.0
        top_ref[it:it + 1, :] = m
        for v in range(TOPK - it):
            nxt = cols[v + 1] if v + 1 < n else jnp.full_like(cols[v], -jnp.inf)
            cols[v] = jnp.where(sel, nxt, cols[v])
    return ties


def _select_block(h, lc, toks, exact, keys_ref, qp_scr, top1_scr, top2_scr, flat, row_id, row8,
                  thr_ref, g1_ref, s2_ref, e2_ref):
    valid = flat < BIG_FLAT
    sides = []
    for p, top_scr in ((0, top1_scr), (1, top2_scr)):
        col = pl.multiple_of((2 * h + p) * N_KEYS, N_KEYS)
        s = lax.dot_general(keys_ref[p], qp_scr[toks, pl.ds(col, N_KEYS)],
                            (((1,), (1,)), ((), ())), preferred_element_type=F32)
        sides.append((s, _top16_ranked(s, row_id, top_scr) if exact else _top16_sorted(s, top_scr)))
    (s1, info1), (s2, info2) = sides
    a1 = top1_scr[...]
    a2 = top2_scr[...]
    v = jnp.concatenate([
        a1[0:1] + a2[0:16],
        a1[1:2] + a2[0:8],
        a1[8:16] + a2[0:1],
        a1[2:3] + a2[0:8],
        a1[3:4] + a2[0:8],
        a1[4:5] + a2[0:8],
        a1[0:8] + a2[0:1],
        a1[0:8] + a2[1:2]], axis=0)
    vw = jnp.where(valid, v, -jnp.inf)
    selm = jnp.zeros(v.shape, F32)
    for _ in range(TOPK):
        m = jnp.max(vw, axis=0, keepdims=True)
        sel = vw == m
        if exact:
            f = jnp.min(jnp.where(sel, flat, BIG_FLAT), axis=0, keepdims=True)
            sel = flat == f
        selm = jnp.where(sel, 1.0, selm)
        vw = jnp.where(sel, -jnp.inf, vw)
    vmax = a1[0:1] + a2[0:1]
    z = jnp.sum(jnp.where(selm > 0.0, jnp.exp(v - vmax), 0.0), axis=0, keepdims=True)
    cnt_lo = selm[56:64] + selm[64:72]
    for i, (lo, hi) in enumerate(((0, 16), (16, 24), (32, 40), (40, 48), (48, 56))):
        ci = jnp.sum(selm[lo:hi], axis=0, keepdims=True)
        cnt_lo = jnp.where(row8 == i, ci, cnt_lo)
    cnt = jnp.concatenate([cnt_lo, selm[24:32]], axis=0)
    if exact:
        thr = jnp.where(cnt > 0.0, 0.5 - cnt, NO_EXPERT)
        key1, key2 = info1, -info2
        match = [float(i) for i in range(TOPK)]
    else:
        thr = jnp.full(cnt.shape, NO_EXPERT, F32)
        for j in range(TOPK):
            thr = jnp.where(cnt == float(j + 1), a2[j:j + 1], thr)
        key1, key2 = s1, s2
        match = [a1[i:i + 1] for i in range(TOPK)]
    thr_a = jnp.full((N_KEYS, LANES), NO_EXPERT, F32)
    for i in range(TOPK):
        thr_a = jnp.where(key1 == match[i], thr[i:i + 1], thr_a)
    thr_ref[h, lc] = thr_a
    g1_ref[h, lc] = jnp.exp(s1 - a1[0:1]) / z
    s2_ref[h, lc] = key2
    e2_ref[h, lc] = jnp.exp(s2 - a2[0:1])
    if exact:
        return None
    n3 = jnp.sum(selm, axis=0, keepdims=True) - float(TOPK)
    return jnp.max(info1 + info2 + n3)


def _select_kernel(h2_ref, wq_ref, keys_ref, flat_ref, thr_ref, g1_ref, s2_ref, e2_ref,
                   qp_scr, top1_scr, top2_scr):
    tm = h2_ref.shape[0]
    qp_scr[...] = jnp.dot(h2_ref[...], wq_ref[...], preferred_element_type=F32).astype(BF16)
    row_id = lax.broadcasted_iota(jnp.int32, (N_KEYS, LANES), 0).astype(F32)
    row8 = lax.broadcasted_iota(jnp.int32, (8, LANES), 0)
    flat = flat_ref[...]
    blocks = []
    for lc in range(tm // LANES):
        toks = slice(lc * LANES, (lc + 1) * LANES)
        blocks.append(functools.partial(
            _select_block, lc=lc, toks=toks, keys_ref=keys_ref, qp_scr=qp_scr, top1_scr=top1_scr.at[lc],
            top2_scr=top2_scr.at[lc], flat=flat, row_id=row_id, row8=row8, thr_ref=thr_ref,
            g1_ref=g1_ref, s2_ref=s2_ref, e2_ref=e2_ref))

    def head_pair_body(hh, carry):
        work = [(block, h) for h in (2 * hh, 2 * hh + 1) for block in blocks]
        ties = [block(h, exact=False) for block, h in work]
        for (block, h), tie in zip(work, ties):
            @pl.when(tie > 0.5)
            def _(block=block, h=h):
                block(h, exact=True)

        return carry

    lax.fori_loop(0, N_HEADS // 2, head_pair_body, 0)


def _select(h2, w_query, sub_keys, tm):
    t, d = h2.shape
    out_spec = pl.BlockSpec((N_HEADS, tm // LANES, N_KEYS, LANES), lambda i: (0, i, 0, 0))
    slab = lambda dt: jax.ShapeDtypeStruct((N_HEADS, t // LANES, N_KEYS, LANES), dt)
    return pl.pallas_call(
        _select_kernel,
        grid=(t // tm,),
        in_specs=[pl.BlockSpec((tm, d), lambda i: (i, 0)),
                  pl.BlockSpec(w_query.shape, lambda i: (0, 0)),
                  pl.BlockSpec(sub_keys.shape, lambda i: (0, 0, 0)),
                  pl.BlockSpec((N_CAND_ROWS, LANES), lambda i: (0, 0))],
        out_specs=[out_spec] * 4,
        out_shape=[slab(F32)] * 4,
        scratch_shapes=[pltpu.VMEM((tm, w_query.shape[1]), BF16),
                        pltpu.VMEM((tm // LANES, TOPK, LANES), F32),
                        pltpu.VMEM((tm // LANES, TOPK, LANES), F32)],
        compiler_params=_params(("parallel",)),
        name="peer_select",
    )(h2, w_query, sub_keys, _cand_flat())


def _coef_kernel(thr_ref, g1_ref, s2_ref, e2_ref, *rest, cast):
    if cast:
        u_ref, v_ref, c_ref, u16_ref, vt16_ref = rest
        u16_ref[...] = u_ref[...].astype(BF16)
        step = pl.program_id(0) * pl.num_programs(1) + pl.program_id(1)

        v_steps = vt16_ref.shape[1] // u_ref.shape[0]

        @pl.when(step % v_steps == 0)
        def _():
            vt16_ref[...] = v_ref[...].T.astype(BF16)
    else:
        (c_ref,) = rest
    n_l, n_a = thr_ref.shape[1], thr_ref.shape[2]
    for lc in range(n_l):
        for ai in range(n_a):
            rows = slice(ai * N_KEYS, (ai + 1) * N_KEYS)
            coef = jnp.zeros((N_KEYS, LANES), F32)
            for h in range(N_HEADS):
                thr = thr_ref[h, lc, ai:ai + 1, :]
                g1 = g1_ref[h, lc, ai:ai + 1, :]
                coef = coef + jnp.where(s2_ref[h, lc] >= thr, e2_ref[h, lc], 0.0) * g1
            c_ref[lc, rows, :] = coef.astype(BF16)


def _coef(sel, te, n_exp, tables=None):
    n_lc = sel[0].shape[1]
    n_l = min(n_lc, COEF_LANE_CHUNKS)
    grid = (n_lc // n_l, n_exp // te)
    a_spec = pl.BlockSpec((N_HEADS, n_l, te // N_KEYS, LANES), lambda i, j: (0, i, j, 0))
    b_spec = pl.BlockSpec((N_HEADS, n_l, N_KEYS, LANES), lambda i, j: (0, i, 0, 0))
    in_specs = [a_spec, a_spec, b_spec, b_spec]
    out_specs = [pl.BlockSpec((n_l, te, LANES), lambda i, j: (i, j, 0))]
    out_shape = [jax.ShapeDtypeStruct((n_lc, n_exp, LANES), BF16)]
    operands = list(sel)
    if tables is not None:
        d = tables[0].shape[1]
        steps = grid[0] * grid[1]
        u_rows = n_exp // steps
        v_rows = max(u_rows, LANES)
        assert u_rows * steps == n_exp and v_rows % u_rows == 0
        v_steps = v_rows // u_rows
        step = lambda i, j: i * grid[1] + j
        in_specs += [pl.BlockSpec((u_rows, d), lambda i, j: (step(i, j), 0)),
                     pl.BlockSpec((v_rows, d), lambda i, j: (step(i, j) // v_steps, 0))]
        out_specs += [pl.BlockSpec((u_rows, d), lambda i, j: (step(i, j), 0)),
                      pl.BlockSpec((d, v_rows), lambda i, j: (0, step(i, j) // v_steps))]
        out_shape += [jax.ShapeDtypeStruct((n_exp, d), BF16), jax.ShapeDtypeStruct((d, n_exp), BF16)]
        operands += list(tables)
    return pl.pallas_call(
        functools.partial(_coef_kernel, cast=tables is not None),
        grid=grid,
        in_specs=in_specs,
        out_specs=out_specs,
        out_shape=out_shape,
        compiler_params=_params(("arbitrary", "arbitrary")),
        name="peer_coef",
    )(*operands)


def _peer_kernel(h2t_ref, u_ref, vt_ref, c_ref, x1_ref, gt2_ref, gf_ref, y_ref, acc_scr):
    j = pl.program_id(1)
    tt = h2t_ref.shape[1]

    @pl.when(j == 0)
    def _():
        acc_scr[...] = jnp.zeros_like(acc_scr)

    act = jnp.dot(u_ref[...], h2t_ref[...], preferred_element_type=F32)
    p = jnp.concatenate(
        [_gelu(act[:, lc * LANES:(lc + 1) * LANES]).astype(BF16) * c_ref[lc]
         for lc in range(tt // LANES)], axis=1)
    acc_scr[...] += jnp.dot(vt_ref[...], p, preferred_element_type=F32)

    @pl.when(j == pl.num_programs(1) - 1)
    def _():
        x2 = x1_ref[...] + gt2_ref[...] * acc_scr[...].T
        y_ref[...] = _rms(x2, gf_ref[...])


def _peer(h2t, experts, sel, x1, gt2, g_final, rows_per_group, tt, te):
    kind, u, vt = experts
    n_exp = u.shape[0]
    if kind == "f32":
        coef, u, vt = _coef(sel, te, n_exp, tables=(u, vt))
    else:
        (coef,) = _coef(sel, te, n_exp)
    d, t = h2t.shape
    dte = te * DENSE_EXPERT_TILES if tt > LANES else te
    once = dict(pipeline_mode=pl.Buffered(1))
    y = pl.pallas_call(
        _peer_kernel,
        grid=(t // tt, n_exp // dte),
        in_specs=[pl.BlockSpec((d, tt), lambda i, j: (0, i), **once),
                  pl.BlockSpec((dte, d), lambda i, j: (j, 0)),
                  pl.BlockSpec((d, dte), lambda i, j: (0, j)),
                  pl.BlockSpec((tt // LANES, dte, LANES), lambda i, j: (i, j, 0)),
                  pl.BlockSpec((tt, d), lambda i, j: (i, 0), **once),
                  _mod_spec(gt2, rows_per_group // tt),
                  pl.BlockSpec((1, d), lambda i, j: (0, 0))],
        out_specs=pl.BlockSpec((tt, d), lambda i, j: (i, 0), **once),
        out_shape=jax.ShapeDtypeStruct((t, d), F32),
        scratch_shapes=[pltpu.VMEM((d, tt), F32)],
        compiler_params=pltpu.CompilerParams(dimension_semantics=("parallel", "arbitrary"),
                                             vmem_limit_bytes=DENSE_VMEM_LIMIT),
        name="peer_dense",
    )(h2t, u, vt, coef, x1, gt2, g_final)
    return y, ("bf16", u, vt)


def _trunk(x, mods, n_groups, window, att_fn, w, experts, wmix, bmix, tm_in, tm_mix, tm_sel, tt, te):
    sh1, sc1, gt1, sh2, sc2, gt2 = mods
    rows = x.shape[0] // n_groups
    q, k, v, u, vn, *wins = _inproj(x, sh1, sc1, w["g1"], w["w_in"], w["lng"], w["lnb"], tm_in,
                                    n_groups, window)
    att, w_out, w_query = att_fn(q, k, v)
    x1, h2, h2t = _mix(att, u, vn, x, gt1, sh2, sc2, w["g2"], wmix, bmix, w_out, tm_mix)
    sel = _select(h2, w_query, w["sub_keys"], tm_sel)
    y, experts = _peer(h2t, experts, sel, x1, gt2, w["g_final"], rows, tt, te)
    return y, k, v, vn, wins, experts, (w_out, w_query)


def kernel(x_prompt, x_sample, cache_k, cache_v, c_prompt, c_sample, w_ada, b_ada, g_norm1, w_in,
           ln_v_g, ln_v_b, w_spatial, b_spatial, w_out, g_norm2, w_query, sub_keys, expert_u,
           expert_v, g_final):
    depth = w_ada.shape[0]
    assert depth == 1, "single-layer trunk"
    batch, seq, d = x_prompt.shape
    n_dec, n_new, _ = x_sample.shape
    buf = cache_k.shape[2]
    assert seq % SPAN == 0 and buf == SPAN and n_new <= SLOTS
    win = min(BRANCHES[-1][0], seq)
    pad_s = SAMPLE_ROWS
    assert n_dec * SLOTS <= pad_s

    n_c = batch + n_dec
    c_all = jnp.pad(jnp.concatenate([c_prompt, c_sample], axis=0), ((0, -n_c % SUBLANES), (0, 0)))
    mod = _adaln(c_all, w_ada[0], b_ada[0][None, :])
    mods = [mod[:, i * d:(i + 1) * d] for i in range(6)]
    mods_p = [m[:batch, None, :] for m in mods]
    mods_s = [jnp.pad(jnp.repeat(m[batch:n_c], SLOTS, axis=0), ((0, pad_s - n_dec * SLOTS), (0, 0)))[None]
              for m in mods]

    tri = jnp.tril(jnp.ones((BAND, BAND), dtype=bool))
    ws = jnp.where(tri, w_spatial[0], 0.0)
    bs = b_spatial[0]
    wmix_p = ws.astype(BF16)
    bmix_p = jnp.broadcast_to(bs[:, :, None], (N_HEADS, BAND, HEAD_DIM))
    eye = jnp.eye(pad_s // SLOTS, dtype=F32)
    slot_pad = ((0, 0), (0, SLOTS - n_new))
    ws_new = jnp.pad(ws[:, :n_new, :n_new], ((0, 0),) + slot_pad[1:] + slot_pad[1:])
    wmix_s = jax.vmap(lambda m: jnp.kron(eye, m))(ws_new).astype(BF16)
    bmix_s = jnp.broadcast_to(jnp.tile(jnp.pad(bs[:, :n_new], slot_pad), (1, pad_s // SLOTS))[:, :, None],
                              (N_HEADS, pad_s, HEAD_DIM))

    w = dict(g1=g_norm1[0][None, :], w_in=w_in[0].astype(BF16), lng=ln_v_g[0][None, :],
             lnb=ln_v_b[0][None, :], g2=g_norm2[0][None, :], sub_keys=sub_keys[0].astype(BF16),
             g_final=g_final[None, :])

    xp = x_prompt.reshape(batch * seq, d)
    prompt_att = functools.partial(_prompt_attention, weights=(w_out[0], w_query[0]))
    y_p, _, _, _, (k_win, v_win), experts, w_bf16 = _trunk(
        xp, mods_p, batch, win, prompt_att, w, ("f32", expert_u[0], expert_v[0]), wmix_p, bmix_p,
        te=EXPERT_TILE, **PROMPT_TILES)

    xs = jnp.pad(x_sample, ((0, 0), (0, SLOTS - n_new), (0, 0))).reshape(n_dec * SLOTS, d)
    xs = jnp.pad(xs, ((0, pad_s - n_dec * SLOTS), (0, 0)))

    def sample_att(q, k, v):
        o = _sample_attention(q[0], k[0], v[0], cache_k[0], cache_v[0], n_new)
        o = o.transpose(1, 0, 2, 3).reshape(N_HEADS, n_dec * SLOTS, HEAD_DIM)
        return (jnp.pad(o, ((0, 0), (0, pad_s - n_dec * SLOTS), (0, 0)))[None],) + w_bf16

    y_s, k_s, v_s, vn_s, _, _, _ = _trunk(
        xs, mods_s, 1, 0, sample_att, w, experts, wmix_s, bmix_s,
        tm_in=pad_s, tm_mix=pad_s, tm_sel=pad_s, tt=pad_s, te=EXPERT_TILE)

    new_rows = lambda t: t[:n_dec * SLOTS].reshape(n_dec, SLOTS, -1)[:, :n_new]
    new_heads = lambda t: t[0, :, :n_dec * SLOTS].reshape(N_HEADS, n_dec, SLOTS, HEAD_DIM)[:, :, :n_new
                                                                                         ].transpose(1, 2, 0, 3)
    return (y_p.reshape(batch, seq, d),
            new_rows(y_s),
            k_win.reshape(1, batch, win, N_HEADS, HEAD_DIM),
            v_win.reshape(1, batch, win, N_HEADS, HEAD_DIM),
            new_heads(k_s)[None],
            new_heads(v_s)[None],
            new_rows(vn_s)[None])
```
